```python
import math
import jax, jax.numpy as jnp
from jax import lax
import numpy as np

D_MODEL = 2048
BATCH = 16
SEQ = 2048
DEPTH = 1
DEC_BATCH = 2
DEC_SEQ = 8192
PAST_LEN = 128

HYENA_WIDTH = 1024
SHORT_CONV = 3
FILTER_EMB = 33
FILTER_HIDDEN = 64
DECAY_TARGET = 1e-2
FAST_DECAY_PCT = 0.3
SLOW_DECAY_PCT = 1.5
ATT_GROUPS = ((128, 1), (512, 4), (2048, 16))
N_GROUPS = 3
HEADS_PER_GROUP = 8
HEAD_DIM = 64
N_ATT_HEADS = N_GROUPS * HEADS_PER_GROUP
ATT_WIDTH = HEADS_PER_GROUP * HEAD_DIM
N_BRANCH = 2
HYENA_COLS = 3 * HYENA_WIDTH
ATT_COLS = 3 * N_GROUPS * ATT_WIDTH
GATE_COLS = N_BRANCH * D_MODEL
IN_COLS = HYENA_COLS + ATT_COLS + GATE_COLS
PEER_HEADS = 8
N_KEYS = 128
N_EXPERTS = N_KEYS * N_KEYS
PEER_TOPK = 16
D_KEY = 256
PEER_CHUNK = 128
EPS = 1e-6
NEG_INF = -1e30

kernel_name = "hyena_dilated_attn_peer_encoder"


def rms_norm(x, g):
    xf = x.astype(jnp.float32)
    y = xf * lax.rsqrt(jnp.mean(xf * xf, axis=-1, keepdims=True) + EPS)
    return (y * g.astype(jnp.float32)).astype(x.dtype)


def short_conv(u, w, b):
    K = w.shape[0]
    pad = K // 2
    L = u.shape[1]
    up = jnp.pad(u, ((0, 0), (pad, pad), (0, 0)))
    out = b
    for k in range(K):
        out = out + up[:, k:k + L] * w[k]
    return out


def hyena_filters(L, w1, b1, w2, b2, w3, b3, w4, freq):
    f32 = jnp.float32
    t = jnp.linspace(0.0, 1.0, L, dtype=f32)[:, None]
    bands = (FILTER_EMB - 1) // 2
    ang = 2.0 * math.pi * jnp.arange(L, dtype=f32)[:, None] / L
    fb = jnp.linspace(1e-4, bands - 1, bands, dtype=f32)[None, :]
    z = jnp.concatenate([t, jnp.cos(fb * ang), -jnp.sin(fb * ang)], axis=-1)
    fr = freq.astype(f32)
    h = jnp.sin(fr * (z @ w1.astype(f32) + b1.astype(f32)))
    h = jnp.sin(fr * (h @ w2.astype(f32) + b2.astype(f32)))
    h = jnp.sin(fr * (h @ w3.astype(f32) + b3.astype(f32)))
    h = (h @ w4.astype(f32)).reshape(L, 2, HYENA_WIDTH)
    min_decay = math.log(DECAY_TARGET) / SLOW_DECAY_PCT
    max_decay = math.log(DECAY_TARGET) / FAST_DECAY_PCT
    deltas = jnp.abs(jnp.linspace(min_decay, max_decay, HYENA_WIDTH, dtype=f32))
    h = h * jnp.exp(-t * deltas)[:, None, :]
    h = h / (jnp.sum(jnp.abs(h), axis=(0, 1), keepdims=True) + EPS)
    k_full = jnp.concatenate(
        [h[:, 0], jnp.zeros((1, HYENA_WIDTH), f32), h[1:, 1][::-1]], axis=0)
    return k_full


def bidir_long_conv(u, k_full, bias):
    L = u.shape[1]
    U = jnp.fft.rfft(u, n=2 * L, axis=1)
    Kf = jnp.fft.rfft(k_full, n=2 * L, axis=0)
    y = jnp.fft.irfft(U * Kf[None], n=2 * L, axis=1)[:, :L]
    return y + u * bias


def hyena_branch(hproj, conv_w, conv_b, w1, b1, w2, b2, w3, b3, w4, freq, hyena_bias):
    f32 = jnp.float32
    L = hproj.shape[1]
    u = short_conv(hproj.astype(f32), conv_w.astype(f32), conv_b.astype(f32))
    x0, x1, v = jnp.split(u, 3, axis=-1)
    k_full = hyena_filters(L, w1, b1, w2, b2, w3, b3, w4, freq)
    z = bidir_long_conv(v * x1, k_full, hyena_bias.astype(f32))
    return (x0 * z).astype(hproj.dtype)


def alibi_slopes():
    return 2.0 ** (-8.0 * (jnp.arange(N_ATT_HEADS, dtype=jnp.float32) + 1.0) / N_ATT_HEADS)


def dilated_window_attention(q, k, v, window, dilation, slopes):
    B, S, H, Dh = q.shape
    half = window // (2 * dilation)
    blk = half
    n_cls = S // dilation
    n_blk = -(-n_cls // blk)
    pad_len = n_blk * blk - n_cls

    def to_classes(t):
        return t.reshape(B, n_cls, dilation, H, Dh).transpose(0, 2, 1, 3, 4)

    qc = jnp.pad(to_classes(q), ((0, 0), (0, 0), (0, pad_len), (0, 0), (0, 0)))
    qc = qc.reshape(B, dilation, n_blk, blk, H, Dh)

    def key_windows(t):
        tc = jnp.pad(to_classes(t), ((0, 0), (0, 0), (blk, pad_len + blk), (0, 0), (0, 0)))
        parts = [tc[:, :, j * blk:(j + n_blk) * blk].reshape(B, dilation, n_blk, blk, H, Dh)
                 for j in range(3)]
        return jnp.concatenate(parts, axis=3)

    kw = key_windows(k)
    vw = key_windows(v)
    scale = 1.0 / math.sqrt(Dh)
    s = jnp.einsum('bcnqhd,bcnkhd->bcnhqk', qc, kw,
                   preferred_element_type=jnp.float32) * scale
    dist = (jnp.arange(3 * blk)[None, :] - blk) - jnp.arange(blk)[:, None]
    kj = (jnp.arange(n_blk)[:, None] - 1) * blk + jnp.arange(3 * blk)[None, :]
    mask = (jnp.abs(dist) <= half)[None] & ((kj >= 0) & (kj < n_cls))[:, None, :]
    bias = -slopes[:, None, None] * (dilation * jnp.abs(dist)).astype(jnp.float32)[None]
    s = jnp.where(mask[None, None, :, None], s + bias[None, None, None], NEG_INF)
    lse = jax.nn.logsumexp(s, axis=-1)
    p = jnp.exp(s - lse[..., None])
    o = jnp.einsum('bcnhqk,bcnkhd->bcnqhd', p, vw.astype(jnp.float32))
    o = o.reshape(B, dilation, n_blk * blk, H, Dh)[:, :, :n_cls]
    o = o.transpose(0, 2, 1, 3, 4).reshape(B, S, H, Dh)
    lse = lse.transpose(0, 1, 2, 4, 3).reshape(B, dilation, n_blk * blk, H)[:, :, :n_cls]
    lse = lse.transpose(0, 2, 1, 3).reshape(B, S, H)
    return o, lse


def dilated_branch(aproj):
    B, S, _ = aproj.shape
    qkv = aproj.reshape(B, S, 3, N_GROUPS, HEADS_PER_GROUP, HEAD_DIM)
    slopes = alibi_slopes()
    outs, lses = [], []
    for g, (window, dilation) in enumerate(ATT_GROUPS):
        o, l = dilated_window_attention(qkv[:, :, 0, g], qkv[:, :, 1, g], qkv[:, :, 2, g], window, dilation,
                                        slopes[g * HEADS_PER_GROUP:(g + 1) * HEADS_PER_GROUP])
        outs.append(o)
        lses.append(l)
    outs = jnp.stack(outs, 0)
    wts = jax.nn.softmax(jnp.stack(lses, 0), axis=0)
    out = jnp.sum(wts[..., None] * outs, axis=0)
    return out.reshape(B, S, ATT_WIDTH).astype(aproj.dtype)


def peer_mixer(xn, wq, subkeys, u_tab, v_tab):
    B, L, D = xn.shape
    T = B * L
    xt = xn.reshape(T, D)
    q = (xt @ wq).reshape(T, PEER_HEADS, 2, D_KEY // 2)
    s = jnp.einsum('thpk,hpnk->thpn', q, subkeys, preferred_element_type=jnp.float32)
    top_s, top_i = lax.top_k(s, PEER_TOPK)
    cand_s = (top_s[:, :, 0, :, None] + top_s[:, :, 1, None, :]).reshape(T, PEER_HEADS, PEER_TOPK * PEER_TOPK)
    cand_i = (top_i[:, :, 0, :, None] * N_KEYS + top_i[:, :, 1, None, :]).reshape(T, PEER_HEADS, PEER_TOPK * PEER_TOPK)
    best_s, pos = lax.top_k(cand_s, PEER_TOPK)
    idx = jnp.take_along_axis(cand_i, pos, axis=-1)
    gate = jax.nn.softmax(best_s, axis=-1)
    n_chunks = T // PEER_CHUNK

    def expert_chunk(args):
        xc, ic, gc = args
        u = u_tab[ic]
        a = jax.nn.gelu(jnp.einsum('chkd,cd->chk', u, xc, preferred_element_type=jnp.float32))
        vv = v_tab[ic]
        return jnp.einsum('chk,chkd->cd', gc * a, vv.astype(jnp.float32)).astype(xc.dtype)

    out = lax.map(expert_chunk, (xt.reshape(n_chunks, PEER_CHUNK, D),
                                 idx.reshape(n_chunks, PEER_CHUNK, PEER_HEADS, PEER_TOPK),
                                 gate.reshape(n_chunks, PEER_CHUNK, PEER_HEADS, PEER_TOPK)))
    return out.reshape(B, L, D)


def encoder(x, norm1_g, w_in, conv_w, conv_b, filt_w1, filt_b1, filt_w2, filt_b2, filt_w3, filt_b3,
            filt_w4, filt_freq, hyena_bias, w_branch_a, w_branch_b, w_out, norm2_g, peer_wq,
            peer_subkeys, peer_u, peer_v, final_g):
    for l in range(DEPTH):
        xn = rms_norm(x, norm1_g[l])
        proj = xn @ w_in[l]
        hproj = proj[..., :HYENA_COLS]
        aproj = proj[..., HYENA_COLS:HYENA_COLS + ATT_COLS]
        glog = proj[..., HYENA_COLS + ATT_COLS:]
        y_a = hyena_branch(hproj, conv_w[l], conv_b[l], filt_w1[l], filt_b1[l], filt_w2[l], filt_b2[l],
                           filt_w3[l], filt_b3[l], filt_w4[l], filt_freq[l], hyena_bias[l])
        y_b = dilated_branch(aproj)
        merged = (jax.nn.sigmoid(glog[..., :D_MODEL]) * (y_a @ w_branch_a[l])
                  + jax.nn.sigmoid(glog[..., D_MODEL:]) * (y_b @ w_branch_b[l]))
        x = x + merged @ w_out[l]
        x = x + peer_mixer(rms_norm(x, norm2_g[l]), peer_wq[l], peer_subkeys[l], peer_u[l], peer_v[l])
    return rms_norm(x, final_g)


def setup_inputs(seed: int = 0) -> dict:
    key = jax.random.key(seed)
    ks = jax.random.split(key, 26)
    f32 = jnp.float32
    nrm = lambda k, shape, sc: jax.random.normal(k, shape, f32) * sc
    return {
        "x_prompt": nrm(ks[0], (BATCH, SEQ, D_MODEL), 1.0),
        "x_sample": nrm(ks[1], (DEC_BATCH, DEC_SEQ, D_MODEL), 1.0),
        "norm1_g": 1.0 + nrm(ks[2], (DEPTH, D_MODEL), 0.01),
        "w_in": nrm(ks[3], (DEPTH, D_MODEL, IN_COLS), D_MODEL ** -0.5),
        "conv_w": nrm(ks[4], (DEPTH, SHORT_CONV, HYENA_COLS), SHORT_CONV ** -0.5),
        "conv_b": nrm(ks[5], (DEPTH, HYENA_COLS), 0.01),
        "filt_w1": nrm(ks[6], (DEPTH, FILTER_EMB, FILTER_HIDDEN), FILTER_EMB ** -0.5),
        "filt_b1": nrm(ks[7], (DEPTH, FILTER_HIDDEN), 0.1),
        "filt_w2": nrm(ks[8], (DEPTH, FILTER_HIDDEN, FILTER_HIDDEN), FILTER_HIDDEN ** -0.5),
        "filt_b2": nrm(ks[9], (DEPTH, FILTER_HIDDEN), 0.1),
        "filt_w3": nrm(ks[10], (DEPTH, FILTER_HIDDEN, FILTER_HIDDEN), FILTER_HIDDEN ** -0.5),
        "filt_b3": nrm(ks[11], (DEPTH, FILTER_HIDDEN), 0.1),
        "filt_w4": nrm(ks[12], (DEPTH, FILTER_HIDDEN, 2 * HYENA_WIDTH), FILTER_HIDDEN ** -0.5),
        "filt_freq": 1.0 + nrm(ks[13], (DEPTH, FILTER_HIDDEN), 0.01),
        "hyena_bias": nrm(ks[14], (DEPTH, HYENA_WIDTH), 1.0),
        "w_branch_a": nrm(ks[15], (DEPTH, HYENA_WIDTH, D_MODEL), HYENA_WIDTH ** -0.5),
        "w_branch_b": nrm(ks[16], (DEPTH, ATT_WIDTH, D_MODEL), ATT_WIDTH ** -0.5),
        "w_out": nrm(ks[17], (DEPTH, D_MODEL, D_MODEL), D_MODEL ** -0.5),
        "norm2_g": 1.0 + nrm(ks[18], (DEPTH, D_MODEL), 0.01),
        "peer_wq": nrm(ks[19], (DEPTH, D_MODEL, PEER_HEADS * D_KEY), D_MODEL ** -0.5),
        "peer_subkeys": nrm(ks[20], (DEPTH, PEER_HEADS, 2, N_KEYS, D_KEY // 2), (D_KEY // 2) ** -0.5),
        "peer_u": nrm(ks[21], (DEPTH, N_EXPERTS, D_MODEL), D_MODEL ** -0.5),
        "peer_v": nrm(ks[22], (DEPTH, N_EXPERTS, D_MODEL), PEER_HEADS ** -0.5),
        "final_g": 1.0 + nrm(ks[23], (D_MODEL,), 0.01),
    }


def reference(x_prompt, x_sample, norm1_g, w_in, conv_w, conv_b, filt_w1, filt_b1, filt_w2, filt_b2,
              filt_w3, filt_b3, filt_w4, filt_freq, hyena_bias, w_branch_a, w_branch_b, w_out, norm2_g,
              peer_wq, peer_subkeys, peer_u, peer_v, final_g):
    y_prompt = encoder(x_prompt, norm1_g, w_in, conv_w, conv_b, filt_w1, filt_b1, filt_w2, filt_b2,
                       filt_w3, filt_b3, filt_w4, filt_freq, hyena_bias, w_branch_a, w_branch_b, w_out,
                       norm2_g, peer_wq, peer_subkeys, peer_u, peer_v, final_g)
    y_sample = encoder(x_sample, norm1_g, w_in, conv_w, conv_b, filt_w1, filt_b1, filt_w2, filt_b2,
                       filt_w3, filt_b3, filt_w4, filt_freq, hyena_bias, w_branch_a, w_branch_b, w_out,
                       norm2_g, peer_wq, peer_subkeys, peer_u, peer_v, final_g)
    return (y_prompt, y_sample)
```

```python
import functools
import math

import jax
import jax.numpy as jnp
from jax import lax
from jax.experimental import pallas as pl
from jax.experimental.pallas import tpu as pltpu

F32 = jnp.float32
BF16 = jnp.bfloat16

D_MODEL = 2048
HYENA_WIDTH = 1024
FILTER_EMB = 33
FILTER_EMB_PAD = 128
FILTER_HIDDEN = 64
DECAY_TARGET = 1e-2
FAST_DECAY_PCT = 0.3
SLOW_DECAY_PCT = 1.5
ATT_GROUPS = ((128, 1), (512, 4), (2048, 16))
N_GROUPS = 3
HEADS_PER_GROUP = 8
HEAD_DIM = 64
N_ATT_HEADS = N_GROUPS * HEADS_PER_GROUP
ATT_WIDTH = HEADS_PER_GROUP * HEAD_DIM
HYENA_COLS = 3 * HYENA_WIDTH
ATT_COLS = 3 * N_GROUPS * ATT_WIDTH
GATE_COLS = 2 * D_MODEL
PEER_HEADS = 8
N_KEYS = 128
N_EXPERTS = N_KEYS * N_KEYS
PEER_TOPK = 16
EPS = 1e-6
NEG_INF = -1e30
ATT_HALF = 64
LANES = 128
VMEM_LIMIT = 56 * 1024 * 1024
HIGHEST = lax.Precision.HIGHEST


def _params(*sem):
    return pltpu.CompilerParams(dimension_semantics=sem, vmem_limit_bytes=VMEM_LIMIT)


def _norm_matmul_kernel(x_ref, g_ref, w_ref, o_ref, xn_ref):
    @pl.when(pl.program_id(1) == 0)
    def _():
        x = x_ref[...]
        ms = jnp.mean(x * x, axis=-1, keepdims=True)
        xn_ref[...] = (x * lax.rsqrt(ms + EPS) * g_ref[...]).astype(BF16)

    o_ref[...] = jnp.dot(xn_ref[...], w_ref[...], preferred_element_type=F32).astype(o_ref.dtype)


def norm_matmul(x, g, w, col0, ncols, out_dtype, tm=1024, tn=512):
    t, d = x.shape
    tm = min(tm, t)
    assert t % tm == 0 and ncols % tn == 0 and col0 % tn == 0
    cb0 = col0 // tn
    return pl.pallas_call(
        _norm_matmul_kernel,
        grid=(t // tm, ncols // tn),
        in_specs=[
            pl.BlockSpec((tm, d), lambda i, j: (i, 0)),
            pl.BlockSpec((1, d), lambda i, j: (0, 0)),
            pl.BlockSpec((d, tn), lambda i, j: (0, cb0 + j)),
        ],
        out_specs=pl.BlockSpec((tm, tn), lambda i, j: (i, j)),
        out_shape=jax.ShapeDtypeStruct((t, ncols), out_dtype),
        scratch_shapes=[pltpu.VMEM((tm, d), BF16)],
        compiler_params=_params("parallel", "arbitrary"),
        name="norm_matmul",
    )(x, g.reshape(1, d), w)


def _filter_mlp_kernel(z_ref, t_ref, w1_ref, b1_ref, w2_ref, b2_ref, w3_ref, b3_ref, w4_ref, fr_ref,
                       dl_ref, h_ref, l1_ref):
    fr = fr_ref[...]
    h = jnp.sin(fr * (jnp.dot(z_ref[...], w1_ref[...], precision=HIGHEST, preferred_element_type=F32)
                      + b1_ref[...]))
    h = jnp.sin(fr * (jnp.dot(h, w2_ref[...], precision=HIGHEST, preferred_element_type=F32) + b2_ref[...]))
    h = jnp.sin(fr * (jnp.dot(h, w3_ref[...], precision=HIGHEST, preferred_element_type=F32) + b3_ref[...]))
    h = jnp.dot(h, w4_ref[...], precision=HIGHEST, preferred_element_type=F32)
    h = h * jnp.exp(-t_ref[...] * dl_ref[...])
    h_ref[...] = h

    @pl.when(pl.program_id(0) == 0)
    def _():
        l1_ref[...] = jnp.zeros_like(l1_ref)

    l1_ref[...] += jnp.sum(jnp.abs(h), axis=0, keepdims=True)


def filter_mlp(seq, w1, b1, w2, b2, w3, b3, w4, freq, tl=512):
    t = jnp.linspace(0.0, 1.0, seq, dtype=F32)[:, None]
    bands = (FILTER_EMB - 1) // 2
    ang = 2.0 * math.pi * jnp.arange(seq, dtype=F32)[:, None] / seq
    fb = jnp.linspace(1e-4, bands - 1, bands, dtype=F32)[None, :]
    z = jnp.concatenate([t, jnp.cos(fb * ang), -jnp.sin(fb * ang)], axis=-1)
    z = jnp.pad(z, ((0, 0), (0, FILTER_EMB_PAD - FILTER_EMB)))
    w1p = jnp.pad(w1.astype(F32), ((0, FILTER_EMB_PAD - FILTER_EMB), (0, 0)))
    min_decay = math.log(DECAY_TARGET) / SLOW_DECAY_PCT
    max_decay = math.log(DECAY_TARGET) / FAST_DECAY_PCT
    deltas = jnp.abs(jnp.linspace(min_decay, max_decay, HYENA_WIDTH, dtype=F32))
    deltas2 = jnp.concatenate([deltas, deltas])[None, :]
    tl = min(tl, seq)
    assert seq % tl == 0
    c2 = 2 * HYENA_WIDTH
    hid = FILTER_HIDDEN
    full = lambda r, c: pl.BlockSpec((r, c), lambda i: (0, 0))
    return pl.pallas_call(
        _filter_mlp_kernel,
        grid=(seq // tl,),
        in_specs=[
            pl.BlockSpec((tl, FILTER_EMB_PAD), lambda i: (i, 0)),
            pl.BlockSpec((tl, 1), lambda i: (i, 0)),
            full(FILTER_EMB_PAD, hid), full(1, hid), full(hid, hid), full(1, hid), full(hid, hid),
            full(1, hid), full(hid, c2), full(1, hid), full(1, c2),
        ],
        out_specs=[pl.BlockSpec((tl, c2), lambda i: (i, 0)), pl.BlockSpec((1, c2), lambda i: (0, 0))],
        out_shape=[jax.ShapeDtypeStruct((seq, c2), F32), jax.ShapeDtypeStruct((1, c2), F32)],
        compiler_params=_params("arbitrary"),
        name="filter_mlp",
    )(z, t, w1p, b1.reshape(1, hid).astype(F32), w2.astype(F32), b2.reshape(1, hid).astype(F32),
      w3.astype(F32), b3.reshape(1, hid).astype(F32), w4.astype(F32), freq.reshape(1, hid).astype(F32), deltas2)


def _fft_split(n):
    lg = int(math.log2(n))
    assert 1 << lg == n
    n1 = 1 << ((lg + 1) // 2)
    return n1, n // n1


def _slow_dft_tables(n1):
    a = jnp.arange(n1, dtype=jnp.int32)
    ang = ((a[:, None] * a[None, :]) % n1).astype(F32) * (2.0 * math.pi / n1)
    return jnp.cos(ang), jnp.sin(ang)


def _fast_dft_tables(n1, n2):
    n = n1 * n2
    k1 = jnp.arange(n1, dtype=jnp.int32)[:, None, None]
    k2 = jnp.arange(n2, dtype=jnp.int32)[None, :, None]
    f = jnp.arange(n2, dtype=jnp.int32)[None, None, :]
    ang = ((f * (k1 + n1 * k2)) % n).astype(F32) * (2.0 * math.pi / n)
    c, s = jnp.cos(ang), jnp.sin(ang)
    top = jnp.concatenate([c, s], axis=2)
    bot = jnp.concatenate([-s, c], axis=2)
    return jnp.concatenate([top, bot], axis=1)


def _left_matmul_kernel(f_ref, x_ref, o_ref, *, precise):
    if precise:
        o = jnp.dot(f_ref[...], x_ref[...], precision=HIGHEST, preferred_element_type=F32)
    else:
        o = jnp.dot(f_ref[...], x_ref[...].astype(BF16), preferred_element_type=F32)
    o_ref[...] = o.astype(o_ref.dtype)


def left_matmul(f, x, out_dtype, precise, ln=4096):
    b, k, lanes = x.shape
    m = f.shape[0]
    ln = min(ln, lanes)
    assert lanes % ln == 0 and f.shape[1] == k
    return pl.pallas_call(
        functools.partial(_left_matmul_kernel, precise=precise),
        grid=(b, lanes // ln),
        in_specs=[pl.BlockSpec((m, k), lambda i, j: (0, 0)),
                  pl.BlockSpec((None, k, ln), lambda i, j: (i, 0, j))],
        out_specs=pl.BlockSpec((None, m, ln), lambda i, j: (i, 0, j)),
        out_shape=jax.ShapeDtypeStruct((b, m, lanes), out_dtype),
        compiler_params=_params("parallel", "parallel"),
        name="left_matmul",
    )(f, x)


def _spectrum_mid_kernel(g_ref, a_ref, sc_ref, o_ref):
    n2 = a_ref.shape[1]
    a = a_ref[...].reshape(2 * n2, a_ref.shape[2])
    x = jnp.dot(g_ref[...], a, precision=HIGHEST, preferred_element_type=F32) * sc_ref[...]
    o_ref[...] = x.reshape(o_ref.shape)


def spectrum_mid(g, a, scale, cb=512):
    _, n1, n2, c = a.shape
    cb = min(cb, c)
    return pl.pallas_call(
        _spectrum_mid_kernel,
        grid=(n1, c // cb),
        in_specs=[pl.BlockSpec((None, 2 * n2, 2 * n2), lambda i, j: (i, 0, 0)),
                  pl.BlockSpec((2, None, n2, cb), lambda i, j: (0, i, 0, j)),
                  pl.BlockSpec((1, cb), lambda i, j: (0, j))],
        out_specs=pl.BlockSpec((2, None, n2, cb), lambda i, j: (0, i, 0, j)),
        out_shape=jax.ShapeDtypeStruct(a.shape, F32),
        compiler_params=_params("parallel", "parallel"),
        name="spectrum_mid",
    )(g, a, scale)


def _conv_mid_kernel(g_ref, gi_ref, a_ref, k_ref, o_ref):
    n2 = a_ref.shape[1]
    cb = a_ref.shape[2]
    a = a_ref[...].reshape(2 * n2, cb)
    x = jnp.dot(g_ref[...], a, preferred_element_type=F32)
    xr, xi = x[:n2], x[n2:]
    kr, ki = k_ref[0], k_ref[1]
    y = jnp.concatenate([xr * kr - xi * ki, xr * ki + xi * kr], axis=0).astype(BF16)
    z = jnp.dot(gi_ref[...], y, preferred_element_type=F32)
    o_ref[...] = z.reshape(o_ref.shape).astype(o_ref.dtype)


def conv_mid(g, gi, a, kf, cb=256):
    b, _, n1, n2, c = a.shape
    cb = min(cb, c)
    return pl.pallas_call(
        _conv_mid_kernel,
        grid=(n1, b, c // cb),
        in_specs=[pl.BlockSpec((None, 2 * n2, 2 * n2), lambda i, bb, j: (i, 0, 0)),
                  pl.BlockSpec((None, 2 * n2, 2 * n2), lambda i, bb, j: (i, 0, 0)),
                  pl.BlockSpec((None, 2, None, n2, cb), lambda i, bb, j: (bb, 0, i, 0, j)),
                  pl.BlockSpec((2, None, n2, cb), lambda i, bb, j: (0, i, 0, j))],
        out_specs=pl.BlockSpec((None, 2, None, n2, cb), lambda i, bb, j: (bb, 0, i, 0, j)),
        out_shape=jax.ShapeDtypeStruct(a.shape, BF16),
        compiler_params=_params("parallel", "parallel", "parallel"),
        name="conv_mid",
    )(g, gi, a, kf)


def _conv_out_kernel(f_ref, z_ref, w_ref, x0_ref, bias_ref, o_ref):
    y = jnp.dot(f_ref[...], z_ref[...], preferred_element_type=F32)
    o_ref[...] = x0_ref[...] * (y + w_ref[...] * bias_ref[...])


def conv_out(f, z, w, x0, bias_l, ln=4096):
    b, k, lanes = z.shape
    m = f.shape[0]
    ln = min(ln, lanes)
    assert lanes % ln == 0 and bias_l.shape == (1, ln)
    return pl.pallas_call(
        _conv_out_kernel,
        grid=(b, lanes // ln),
        in_specs=[pl.BlockSpec((m, k), lambda i, j: (0, 0)),
                  pl.BlockSpec((None, k, ln), lambda i, j: (i, 0, j)),
                  pl.BlockSpec((None, m, ln), lambda i, j: (i, 0, j)),
                  pl.BlockSpec((None, m, ln), lambda i, j: (i, 0, j)),
                  pl.BlockSpec((1, ln), lambda i, j: (0, 0))],
        out_specs=pl.BlockSpec((None, m, ln), lambda i, j: (i, 0, j)),
        out_shape=jax.ShapeDtypeStruct((b, m, lanes), F32),
        compiler_params=_params("parallel", "parallel"),
        name="conv_out",
    )(f, z, w, x0, bias_l)


def _hyena_pre_kernel(x0_ref, x1_ref, v_ref, w0_ref, w1_ref, wv_ref, b0_ref, b1_ref, bv_ref, x0o_ref, wo_ref):
    seq = x0_ref.shape[0]
    row = lax.broadcasted_iota(jnp.int32, x0_ref.shape, 0)
    first, last = row == 0, row == seq - 1

    def conv(u_ref, w_ref, b_ref):
        u = u_ref[...]
        prev = jnp.where(first, 0.0, pltpu.roll(u, 1, axis=0))
        nxt = jnp.where(last, 0.0, pltpu.roll(u, seq - 1, axis=0))
        return b_ref[...] + prev * w_ref[0:1, :] + u * w_ref[1:2, :] + nxt * w_ref[2:3, :]

    x0o_ref[...] = conv(x0_ref, w0_ref, b0_ref)
    wo_ref[...] = conv(v_ref, wv_ref, bv_ref) * conv(x1_ref, w1_ref, b1_ref)


def hyena_pre(hproj, conv_w, conv_b):
    b, seq, _ = hproj.shape
    c = HYENA_WIDTH
    cb = max(LANES, min(c, (1 << 20) // seq))
    nb = c // cb
    u_spec = lambda g: pl.BlockSpec((None, seq, cb), lambda i, j: (i, 0, g * nb + j))
    w_spec = lambda g: pl.BlockSpec((3, cb), lambda i, j: (0, g * nb + j))
    b_spec = lambda g: pl.BlockSpec((1, cb), lambda i, j: (0, g * nb + j))
    o_spec = pl.BlockSpec((None, seq, cb), lambda i, j: (i, 0, j))
    cw = conv_w.astype(F32)
    cbias = conv_b.reshape(1, 3 * c).astype(F32)
    return pl.pallas_call(
        _hyena_pre_kernel,
        grid=(b, nb),
        in_specs=[u_spec(0), u_spec(1), u_spec(2), w_spec(0), w_spec(1), w_spec(2),
                  b_spec(0), b_spec(1), b_spec(2)],
        out_specs=[o_spec, o_spec],
        out_shape=[jax.ShapeDtypeStruct((b, seq, c), F32)] * 2,
        compiler_params=_params("parallel", "parallel"),
        name="hyena_pre",
    )(hproj, hproj, hproj, cw, cw, cw, cbias, cbias, cbias)


def hyena_branch(hproj, conv_w, conv_b, fw1, fb1, fw2, fb2, fw3, fb3, fw4, freq, hyena_bias):
    b, seq, _ = hproj.shape
    c = HYENA_WIDTH
    n = 2 * seq
    n1, n2 = _fft_split(n)
    n1h = n1 // 2
    cos1, sin1 = _slow_dft_tables(n1)
    g = _fast_dft_tables(n1, n2)
    gi = jnp.swapaxes(g, 1, 2)

    h, l1 = filter_mlp(seq, fw1, fb1, fw2, fb2, fw3, fb3, fw4, freq)
    h = h.reshape(seq, 2, c)
    k_full = jnp.concatenate([h[:, 0], jnp.zeros((1, c), F32), h[1:, 1][::-1]], axis=0)
    l1c = l1[:, :c] + l1[:, c:]
    scale = 1.0 / ((l1c + EPS) * n)
    fa_full = jnp.concatenate([cos1, -sin1], axis=0)
    ka = left_matmul(fa_full, k_full.reshape(1, n1, n2 * c), F32, precise=True)
    kf = spectrum_mid(g, ka.reshape(2, n1, n2, c), scale)

    x0c, w = hyena_pre(hproj, conv_w, conv_b)
    fa = fa_full[:, :n1h].astype(BF16)
    a = left_matmul(fa, w.reshape(b, n1h, n2 * c), BF16, precise=False)
    z = conv_mid(g.astype(BF16), gi.astype(BF16), a.reshape(b, 2, n1, n2, c), kf)
    fc = jnp.concatenate([cos1[:n1h], -sin1[:n1h]], axis=1).astype(BF16)
    ln = min(4096, n2 * c)
    bias_l = jnp.tile(hyena_bias.reshape(1, c).astype(F32), (1, ln // c))
    y = conv_out(fc, z.reshape(b, 2 * n1, n2 * c), w.reshape(b, n1h, n2 * c), x0c.reshape(b, n1h, n2 * c),
                 bias_l, ln=ln)
    return y.reshape(b, seq, c)


ATT_SUB = 128


def _dilated_attn_kernel(q_ref, kp_ref, kc_ref, kn_ref, vp_ref, vc_ref, vn_ref, o_ref, l_ref, *,
                         n_cls, dilation, slopes):
    tq = q_ref.shape[0]
    q0 = pl.program_id(2) * tq
    sk = ATT_SUB + 2 * ATT_HALF
    scale = 1.0 / math.sqrt(HEAD_DIM)
    qi = lax.broadcasted_iota(jnp.int32, (ATT_SUB, sk), 0)
    kj = lax.broadcasted_iota(jnp.int32, (ATT_SUB, sk), 1) - ATT_HALF
    dist = jnp.abs(kj - qi)
    band = dist <= ATT_HALF
    adist = (dilation * dist).astype(F32)
    lane = lax.broadcasted_iota(jnp.int32, (sk, LANES), 1)
    low_k = lane < HEAD_DIM
    low_q = lax.broadcasted_iota(jnp.int32, (ATT_SUB, LANES), 1) < HEAD_DIM
    for sub in range(tq // ATT_SUB):
        r0 = sub * ATT_SUB
        kabs = q0 + r0 + kj
        mask = band & (kabs >= 0) & (kabs < n_cls)
        for pair in range(HEADS_PER_GROUP // 2):
            cs = slice(pair * LANES, (pair + 1) * LANES)
            q = q_ref[r0:r0 + ATT_SUB, cs]

            def keys(p_ref, c_ref, n_ref):
                parts = []
                if r0 == 0:
                    parts.append(p_ref[:, cs])
                    parts.append(c_ref[0:min(tq, ATT_SUB + ATT_HALF), cs])
                else:
                    parts.append(c_ref[r0 - ATT_HALF:min(tq, r0 + ATT_SUB + ATT_HALF), cs])
                if r0 + ATT_SUB + ATT_HALF > tq:
                    parts.append(n_ref[:, cs])
                return jnp.concatenate(parts, axis=0)

            k = keys(kp_ref, kc_ref, kn_ref)
            v = keys(vp_ref, vc_ref, vn_ref)
            acc = None
            ls, lses = [], []
            for e in range(2):
                sel = low_k if e == 0 else ~low_k
                ke = jnp.where(sel, k, jnp.zeros_like(k))
                ve = jnp.where(sel, v, jnp.zeros_like(v))
                s = lax.dot_general(q, ke, (((1,), (1,)), ((), ())), preferred_element_type=F32) * scale
                s = jnp.where(mask, s - slopes[2 * pair + e] * adist, NEG_INF)
                m = jnp.max(s, axis=-1, keepdims=True)
                p = jnp.exp(s - m)
                l = jnp.sum(p, axis=-1, keepdims=True)
                pv = jnp.dot(p.astype(BF16), ve, preferred_element_type=F32)
                acc = pv if acc is None else acc + pv
                ls.append(l)
                lses.append(m + jnp.log(l))
            inv = 1.0 / jnp.where(low_q, ls[0], ls[1])
            o_ref[r0:r0 + ATT_SUB, cs] = acc * inv
            l_ref[r0:r0 + ATT_SUB, cs] = jnp.where(low_q, lses[0], lses[1])


def dilated_attention(aproj, group, tq=256):
    b, seq, _ = aproj.shape
    window, dilation = ATT_GROUPS[group]
    assert window // (2 * dilation) == ATT_HALF
    n_cls = seq // dilation
    tq = min(tq, n_cls)
    assert n_cls % tq == 0 and tq % ATT_SUB == 0
    nblk = ATT_COLS // ATT_WIDTH
    hb = tq // ATT_HALF
    n_halo = n_cls // ATT_HALF
    slopes = tuple(2.0 ** (-8.0 * (group * HEADS_PER_GROUP + h + 1.0) / N_ATT_HEADS)
                   for h in range(HEADS_PER_GROUP))
    x = aproj.reshape(b, n_cls, dilation * ATT_COLS)
    col = lambda which: (lambda r: r * nblk + which * N_GROUPS + group)
    cur = lambda which: pl.BlockSpec((None, tq, ATT_WIDTH), lambda i, r, j: (i, j, col(which)(r)))
    prev = lambda which: pl.BlockSpec((None, ATT_HALF, ATT_WIDTH),
                                      lambda i, r, j: (i, jnp.maximum(j * hb - 1, 0), col(which)(r)))
    nxt = lambda which: pl.BlockSpec((None, ATT_HALF, ATT_WIDTH),
                                     lambda i, r, j: (i, jnp.minimum((j + 1) * hb, n_halo - 1), col(which)(r)))
    o_spec = pl.BlockSpec((None, tq, ATT_WIDTH), lambda i, r, j: (i, j, r))
    o, l = pl.pallas_call(
        functools.partial(_dilated_attn_kernel, n_cls=n_cls, dilation=dilation, slopes=slopes),
        grid=(b, dilation, n_cls // tq),
        in_specs=[cur(0), prev(1), cur(1), nxt(1), prev(2), cur(2), nxt(2)],
        out_specs=[o_spec, o_spec],
        out_shape=[jax.ShapeDtypeStruct((b, n_cls, dilation * ATT_WIDTH), F32)] * 2,
        compiler_params=_params("parallel", "parallel", "parallel"),
        name="dilated_attn",
    )(x, x, x, x, x, x, x)
    return o.reshape(b * seq, ATT_WIDTH), l.reshape(b * seq, ATT_WIDTH)


def _merge_kernel(x_ref, ga_ref, gb_ref, ya_ref, o0_ref, o1_ref, o2_ref, l0_ref, l1_ref, l2_ref,
                  wa_ref, wb_ref, wo_ref, g2_ref, x1_ref, xn_ref):
    l0, l1, l2 = l0_ref[...], l1_ref[...], l2_ref[...]
    m = jnp.maximum(jnp.maximum(l0, l1), l2)
    e0, e1, e2 = jnp.exp(l0 - m), jnp.exp(l1 - m), jnp.exp(l2 - m)
    yb = (e0 * o0_ref[...] + e1 * o1_ref[...] + e2 * o2_ref[...]) / (e0 + e1 + e2)
    pa = jnp.dot(ya_ref[...].astype(BF16), wa_ref[...], preferred_element_type=F32)
    pb = jnp.dot(yb.astype(BF16), wb_ref[...], preferred_element_type=F32)
    merged = jax.nn.sigmoid(ga_ref[...]) * pa + jax.nn.sigmoid(gb_ref[...]) * pb
    x1 = x_ref[...] + jnp.dot(merged.astype(BF16), wo_ref[...], preferred_element_type=F32)
    x1_ref[...] = x1
    ms = jnp.mean(x1 * x1, axis=-1, keepdims=True)
    xn_ref[...] = (x1 * lax.rsqrt(ms + EPS) * g2_ref[...]).astype(BF16)


def merge(x, glog, ya, outs, lses, wa, wb, wo, g2, tm=256):
    t, d = x.shape
    tm = min(tm, t)
    assert t % tm == 0
    row = lambda c, blk=0: pl.BlockSpec((tm, c), lambda i: (i, blk))
    const = lambda r, c: pl.BlockSpec((r, c), lambda i: (0, 0), pipeline_mode=pl.Buffered(1))
    aw = ATT_WIDTH
    return pl.pallas_call(
        _merge_kernel,
        grid=(t // tm,),
        in_specs=[row(d), row(d, 0), row(d, 1), row(HYENA_WIDTH), row(aw), row(aw), row(aw), row(aw), row(aw),
                  row(aw), const(HYENA_WIDTH, d), const(aw, d), const(d, d), const(1, d)],
        out_specs=[row(d), row(d)],
        out_shape=[jax.ShapeDtypeStruct((t, d), F32), jax.ShapeDtypeStruct((t, d), BF16)],
        compiler_params=_params("parallel"),
        name="merge",
    )(x, glog, glog, ya, *outs, *lses, wa, wb, wo, g2.reshape(1, d).astype(F32))


def _top_threshold(vals, k):
    top = None
    cur = vals
    m = None
    for it in range(k):
        m = jnp.max(cur, axis=0, keepdims=True)
        if it == 0:
            top = m
        if it + 1 < k:
            cur = jnp.where(cur == m, -jnp.inf, cur)
    return m, top


def _top_values(vals, k):
    rows = []
    cur = vals
    for it in range(k):
        m = jnp.max(cur, axis=0, keepdims=True)
        rows.append(m)
        if it + 1 < k:
            cur = jnp.where(cur == m, -jnp.inf, cur)
    return jnp.concatenate(rows, axis=0)


def _peer_scores_kernel(xn_ref, wq_ref, sk_ref, s_ref, e_ref, tau_ref):
    q = jnp.dot(xn_ref[...], wq_ref[...], preferred_element_type=F32).astype(BF16)
    for h in range(PEER_HEADS):
        tops = []
        for p in range(2):
            hp = 2 * h + p
            qs = q[:, hp * N_KEYS:(hp + 1) * N_KEYS]
            s = lax.dot_general(sk_ref[hp], qs, (((1,), (1,)), ((), ())), preferred_element_type=F32)
            s_ref[hp * N_KEYS:(hp + 1) * N_KEYS, :] = s
            tops.append(_top_values(s, PEER_TOPK))
        cand = jnp.concatenate([tops[0][r:r + 1, :] + tops[1] for r in range(PEER_TOPK)], axis=0)
        tau, best = _top_threshold(cand, PEER_TOPK)
        zsum = jnp.sum(jnp.where(cand >= tau, jnp.exp(cand - best), 0.0), axis=0, keepdims=True)
        s1 = s_ref[(2 * h) * N_KEYS:(2 * h + 1) * N_KEYS, :]
        s2 = s_ref[(2 * h + 1) * N_KEYS:(2 * h + 2) * N_KEYS, :]
        e_ref[(2 * h) * N_KEYS:(2 * h + 1) * N_KEYS, :] = jnp.exp(s1 - tops[0][0:1, :])
        e_ref[(2 * h + 1) * N_KEYS:(2 * h + 2) * N_KEYS, :] = jnp.exp(s2 - tops[1][0:1, :]) / zsum
        tau_ref[h:h + 1, :] = tau


def peer_scores(xn, wq, subkeys, tm=256):
    t, d = xn.shape
    tm = min(tm, t)
    assert t % tm == 0
    rows = 2 * PEER_HEADS * N_KEYS
    return pl.pallas_call(
        _peer_scores_kernel,
        grid=(t // tm,),
        in_specs=[pl.BlockSpec((tm, d), lambda i: (i, 0)),
                  pl.BlockSpec((d, rows), lambda i: (0, 0), pipeline_mode=pl.Buffered(1)),
                  pl.BlockSpec((2 * PEER_HEADS, N_KEYS, N_KEYS), lambda i: (0, 0, 0))],
        out_specs=[pl.BlockSpec((rows, tm), lambda i: (0, i)), pl.BlockSpec((rows, tm), lambda i: (0, i)),
                   pl.BlockSpec((PEER_HEADS, tm), lambda i: (0, i))],
        out_shape=[jax.ShapeDtypeStruct((rows, t), F32), jax.ShapeDtypeStruct((rows, t), F32),
                   jax.ShapeDtypeStruct((PEER_HEADS, t), F32)],
        compiler_params=_params("parallel"),
        name="peer_scores",
    )(xn, wq, subkeys)


def _gelu_tanh(x):
    return 0.5 * x * (1.0 + jnp.tanh(math.sqrt(2.0 / math.pi) * (x + 0.044715 * (x * x * x))))


def _peer_experts_kernel(xn_ref, s_ref, e_ref, tau_ref, u_ref, vt_ref, x1_ref, g_ref, y_ref, acc_ref, wa_ref):
    j = pl.program_id(1)
    eb = u_ref.shape[0]
    per = eb // N_KEYS

    @pl.when(j == 0)
    def _():
        acc_ref[...] = jnp.zeros_like(acc_ref)

    xn = xn_ref[...]
    for ii in range(per):
        key1 = j * per + ii
        a = lax.dot_general(u_ref[ii * N_KEYS:(ii + 1) * N_KEYS, :], xn, (((1,), (1,)), ((), ())),
                            preferred_element_type=F32)
        wsum = None
        for h in range(PEER_HEADS):
            r1 = (2 * h) * N_KEYS
            r2 = (2 * h + 1) * N_KEYS
            s1 = s_ref[pl.ds(r1 + key1, 1), :]
            e1 = e_ref[pl.ds(r1 + key1, 1), :]
            sel = (s1 + s_ref[r2:r2 + N_KEYS, :]) >= tau_ref[h:h + 1, :]
            term = jnp.where(sel, e1 * e_ref[r2:r2 + N_KEYS, :], 0.0)
            wsum = term if wsum is None else wsum + term
        wa_ref[ii * N_KEYS:(ii + 1) * N_KEYS, :] = (_gelu_tanh(a) * wsum).astype(BF16)
    acc_ref[...] += jnp.dot(vt_ref[...], wa_ref[...], preferred_element_type=F32)

    @pl.when(j == pl.num_programs(1) - 1)
    def _():
        x2 = x1_ref[...] + acc_ref[...].T
        ms = jnp.mean(x2 * x2, axis=-1, keepdims=True)
        y_ref[...] = x2 * lax.rsqrt(ms + EPS) * g_ref[...]


def peer_experts(xn, s_t, e_t, tau_t, u, vt, x1, final_g, tm=512, eb=512):
    t, d = xn.shape
    tm = min(tm, t)
    assert t % tm == 0 and N_EXPERTS % eb == 0 and eb % N_KEYS == 0
    rows = 2 * PEER_HEADS * N_KEYS
    return pl.pallas_call(
        _peer_experts_kernel,
        grid=(t // tm, N_EXPERTS // eb),
        in_specs=[pl.BlockSpec((tm, d), lambda i, j: (i, 0)),
                  pl.BlockSpec((rows, tm), lambda i, j: (0, i)),
                  pl.BlockSpec((rows, tm), lambda i, j: (0, i)),
                  pl.BlockSpec((PEER_HEADS, tm), lambda i, j: (0, i)),
                  pl.BlockSpec((eb, d), lambda i, j: (j, 0)),
                  pl.BlockSpec((d, eb), lambda i, j: (0, j)),
                  pl.BlockSpec((tm, d), lambda i, j: (i, 0)),
                  pl.BlockSpec((1, d), lambda i, j: (0, 0))],
        out_specs=pl.BlockSpec((tm, d), lambda i, j: (i, 0)),
        out_shape=jax.ShapeDtypeStruct((t, d), F32),
        scratch_shapes=[pltpu.VMEM((d, tm), F32), pltpu.VMEM((eb, tm), BF16)],
        compiler_params=_params("parallel", "arbitrary"),
        name="peer_experts",
    )(xn, s_t, e_t, tau_t, u, vt, x1, final_g.reshape(1, d).astype(F32))


def _encoder(x, p):
    b, seq, d = x.shape
    t = b * seq
    xt = x.reshape(t, d)
    hproj = norm_matmul(xt, p["norm1_g"], p["w_in"], 0, HYENA_COLS, F32)
    aproj = norm_matmul(xt, p["norm1_g"], p["w_in"], HYENA_COLS, ATT_COLS, BF16)
    glog = norm_matmul(xt, p["norm1_g"], p["w_in"], HYENA_COLS + ATT_COLS, GATE_COLS, F32)
    ya = hyena_branch(hproj.reshape(b, seq, HYENA_COLS), p["conv_w"], p["conv_b"], p["filt_w1"], p["filt_b1"],
                      p["filt_w2"], p["filt_b2"], p["filt_w3"], p["filt_b3"], p["filt_w4"], p["filt_freq"],
                      p["hyena_bias"])
    att = [dilated_attention(aproj.reshape(b, seq, ATT_COLS), g) for g in range(N_GROUPS)]
    x1, xn2 = merge(xt, glog, ya.reshape(t, HYENA_WIDTH), [o for o, _ in att], [l for _, l in att],
                    p["w_branch_a"], p["w_branch_b"], p["w_out"], p["norm2_g"])
    s_t, e_t, tau_t = peer_scores(xn2, p["peer_wq"], p["peer_subkeys"])
    y = peer_experts(xn2, s_t, e_t, tau_t, p["peer_u"], p["peer_vt"], x1, p["final_g"])
    return y.reshape(b, seq, d)


def kernel(x_prompt, x_sample, norm1_g, w_in, conv_w, conv_b, filt_w1, filt_b1, filt_w2, filt_b2, filt_w3,
           filt_b3, filt_w4, filt_freq, hyena_bias, w_branch_a, w_branch_b, w_out, norm2_g, peer_wq,
           peer_subkeys, peer_u, peer_v, final_g):
    assert norm1_g.shape[0] == 1, "single-layer encoder"
    p = dict(
        norm1_g=norm1_g[0].astype(F32), w_in=w_in[0].astype(BF16), conv_w=conv_w[0], conv_b=conv_b[0],
        filt_w1=filt_w1[0], filt_b1=filt_b1[0], filt_w2=filt_w2[0], filt_b2=filt_b2[0], filt_w3=filt_w3[0],
        filt_b3=filt_b3[0], filt_w4=filt_w4[0], filt_freq=filt_freq[0], hyena_bias=hyena_bias[0],
        w_branch_a=w_branch_a[0].astype(BF16), w_branch_b=w_branch_b[0].astype(BF16),
        w_out=w_out[0].astype(BF16), norm2_g=norm2_g[0], peer_wq=peer_wq[0].astype(BF16),
        peer_subkeys=peer_subkeys[0].reshape(2 * PEER_HEADS, N_KEYS, N_KEYS).astype(BF16),
        peer_u=peer_u[0].astype(BF16), peer_vt=peer_v[0].astype(BF16).T, final_g=final_g,
    )
    return (_encoder(x_prompt, p), _encoder(x_sample, p))
```

```python
import functools
import math

import jax
import jax.numpy as jnp
from jax import lax
from jax.experimental import pallas as pl
from jax.experimental.pallas import tpu as pltpu

F32 = jnp.float32
BF16 = jnp.bfloat16

D_MODEL = 2048
HYENA_WIDTH = 1024
FILTER_EMB = 33
FILTER_EMB_PAD = 128
FILTER_HIDDEN = 64
DECAY_TARGET = 1e-2
FAST_DECAY_PCT = 0.3
SLOW_DECAY_PCT = 1.5
ATT_GROUPS = ((128, 1), (512, 4), (2048, 16))
N_GROUPS = 3
HEADS_PER_GROUP = 8
HEAD_DIM = 64
N_ATT_HEADS = N_GROUPS * HEADS_PER_GROUP
ATT_WIDTH = HEADS_PER_GROUP * HEAD_DIM
HYENA_COLS = 3 * HYENA_WIDTH
ATT_COLS = 3 * N_GROUPS * ATT_WIDTH
GATE_COLS = 2 * D_MODEL
PEER_HEADS = 8
N_KEYS = 128
N_EXPERTS = N_KEYS * N_KEYS
PEER_TOPK = 16
EPS = 1e-6
NEG_INF = -1e30
ATT_HALF = 64
LANES = 128
VMEM_LIMIT = 56 * 1024 * 1024
HIGHEST = lax.Precision.HIGHEST


def _params(*sem, flags=None):
    return pltpu.CompilerParams(dimension_semantics=sem, vmem_limit_bytes=VMEM_LIMIT, flags=flags)


def _norm_matmul_kernel(x_ref, g_ref, w_ref, o_ref, xn_ref):
    @pl.when(pl.program_id(1) == 0)
    def _():
        x = x_ref[...]
        ms = jnp.mean(x * x, axis=-1, keepdims=True)
        xn_ref[...] = (x * lax.rsqrt(ms + EPS) * g_ref[...]).astype(BF16)

    o_ref[...] = jnp.dot(xn_ref[...], w_ref[...], preferred_element_type=F32).astype(o_ref.dtype)


def norm_matmul(x, g, w, col0, ncols, out_dtype, tm=1024, tn=512):
    t, d = x.shape
    tm = min(tm, t)
    assert t % tm == 0 and ncols % tn == 0 and col0 % tn == 0
    cb0 = col0 // tn
    return pl.pallas_call(
        _norm_matmul_kernel,
        grid=(t // tm, ncols // tn),
        in_specs=[
            pl.BlockSpec((tm, d), lambda i, j: (i, 0)),
            pl.BlockSpec((1, d), lambda i, j: (0, 0)),
            pl.BlockSpec((d, tn), lambda i, j: (0, cb0 + j)),
        ],
        out_specs=pl.BlockSpec((tm, tn), lambda i, j: (i, j)),
        out_shape=jax.ShapeDtypeStruct((t, ncols), out_dtype),
        scratch_shapes=[pltpu.VMEM((tm, d), BF16)],
        compiler_params=_params("parallel", "arbitrary"),
        name="norm_matmul",
    )(x, g.reshape(1, d), w)


def _filter_mlp_kernel(z_ref, t_ref, w1_ref, b1_ref, w2_ref, b2_ref, w3_ref, b3_ref, w4_ref, fr_ref,
                       dl_ref, h_ref, l1_ref):
    fr = fr_ref[...]
    h = jnp.sin(fr * (jnp.dot(z_ref[...], w1_ref[...], precision=HIGHEST, preferred_element_type=F32)
                      + b1_ref[...]))
    h = jnp.sin(fr * (jnp.dot(h, w2_ref[...], precision=HIGHEST, preferred_element_type=F32) + b2_ref[...]))
    h = jnp.sin(fr * (jnp.dot(h, w3_ref[...], precision=HIGHEST, preferred_element_type=F32) + b3_ref[...]))
    h = jnp.dot(h, w4_ref[...], precision=HIGHEST, preferred_element_type=F32)
    h = h * jnp.exp(-t_ref[...] * dl_ref[...])
    h_ref[...] = h

    @pl.when(pl.program_id(0) == 0)
    def _():
        l1_ref[...] = jnp.zeros_like(l1_ref)

    l1_ref[...] += jnp.sum(jnp.abs(h), axis=0, keepdims=True)


def filter_mlp(seq, w1, b1, w2, b2, w3, b3, w4, freq, tl=512):
    t = jnp.linspace(0.0, 1.0, seq, dtype=F32)[:, None]
    bands = (FILTER_EMB - 1) // 2
    ang = 2.0 * math.pi * jnp.arange(seq, dtype=F32)[:, None] / seq
    fb = jnp.linspace(1e-4, bands - 1, bands, dtype=F32)[None, :]
    z = jnp.concatenate([t, jnp.cos(fb * ang), -jnp.sin(fb * ang)], axis=-1)
    z = jnp.pad(z, ((0, 0), (0, FILTER_EMB_PAD - FILTER_EMB)))
    w1p = jnp.pad(w1.astype(F32), ((0, FILTER_EMB_PAD - FILTER_EMB), (0, 0)))
    min_decay = math.log(DECAY_TARGET) / SLOW_DECAY_PCT
    max_decay = math.log(DECAY_TARGET) / FAST_DECAY_PCT
    deltas = jnp.abs(jnp.linspace(min_decay, max_decay, HYENA_WIDTH, dtype=F32))
    deltas2 = jnp.concatenate([deltas, deltas])[None, :]
    tl = min(tl, seq)
    assert seq % tl == 0
    c2 = 2 * HYENA_WIDTH
    hid = FILTER_HIDDEN
    full = lambda r, c: pl.BlockSpec((r, c), lambda i: (0, 0))
    return pl.pallas_call(
        _filter_mlp_kernel,
        grid=(seq // tl,),
        in_specs=[
            pl.BlockSpec((tl, FILTER_EMB_PAD), lambda i: (i, 0)),
            pl.BlockSpec((tl, 1), lambda i: (i, 0)),
            full(FILTER_EMB_PAD, hid), full(1, hid), full(hid, hid), full(1, hid), full(hid, hid),
            full(1, hid), full(hid, c2), full(1, hid), full(1, c2),
        ],
        out_specs=[pl.BlockSpec((tl, c2), lambda i: (i, 0)), pl.BlockSpec((1, c2), lambda i: (0, 0))],
        out_shape=[jax.ShapeDtypeStruct((seq, c2), F32), jax.ShapeDtypeStruct((1, c2), F32)],
        compiler_params=_params("arbitrary"),
        name="filter_mlp",
    )(z, t, w1p, b1.reshape(1, hid).astype(F32), w2.astype(F32), b2.reshape(1, hid).astype(F32),
      w3.astype(F32), b3.reshape(1, hid).astype(F32), w4.astype(F32), freq.reshape(1, hid).astype(F32), deltas2)


def _fft_split(n):
    lg = int(math.log2(n))
    assert 1 << lg == n
    n1 = 1 << ((lg + 1) // 2)
    return n1, n // n1


def _slow_dft_tables(n1):
    a = jnp.arange(n1, dtype=jnp.int32)
    ang = ((a[:, None] * a[None, :]) % n1).astype(F32) * (2.0 * math.pi / n1)
    return jnp.cos(ang), jnp.sin(ang)


def _fast_dft_tables(n1, n2):
    n = n1 * n2
    k1 = jnp.arange(n1, dtype=jnp.int32)[:, None, None]
    k2 = jnp.arange(n2, dtype=jnp.int32)[None, :, None]
    f = jnp.arange(n2, dtype=jnp.int32)[None, None, :]
    ang = ((f * (k1 + n1 * k2)) % n).astype(F32) * (2.0 * math.pi / n)
    c, s = jnp.cos(ang), jnp.sin(ang)
    top = jnp.concatenate([c, s], axis=2)
    bot = jnp.concatenate([-s, c], axis=2)
    return jnp.concatenate([top, bot], axis=1)


def _left_matmul_kernel(f_ref, x_ref, o_ref, *, precise):
    if precise:
        o = jnp.dot(f_ref[...], x_ref[...], precision=HIGHEST, preferred_element_type=F32)
    else:
        o = jnp.dot(f_ref[...], x_ref[...].astype(BF16), preferred_element_type=F32)
    o_ref[...] = o.astype(o_ref.dtype)


def left_matmul(f, x, out_dtype, precise, ln=4096):
    b, k, lanes = x.shape
    m = f.shape[0]
    ln = min(ln, lanes)
    assert lanes % ln == 0 and f.shape[1] == k
    return pl.pallas_call(
        functools.partial(_left_matmul_kernel, precise=precise),
        grid=(b, lanes // ln),
        in_specs=[pl.BlockSpec((m, k), lambda i, j: (0, 0)),
                  pl.BlockSpec((None, k, ln), lambda i, j: (i, 0, j))],
        out_specs=pl.BlockSpec((None, m, ln), lambda i, j: (i, 0, j)),
        out_shape=jax.ShapeDtypeStruct((b, m, lanes), out_dtype),
        compiler_params=_params("parallel", "parallel"),
        name="left_matmul",
    )(f, x)


def _spectrum_mid_kernel(g_ref, a_ref, sc_ref, o_ref):
    n2 = a_ref.shape[1]
    a = a_ref[...].reshape(2 * n2, a_ref.shape[2])
    x = jnp.dot(g_ref[...], a, precision=HIGHEST, preferred_element_type=F32) * sc_ref[...]
    o_ref[...] = x.reshape(o_ref.shape)


def spectrum_mid(g, a, scale, cb=512):
    _, n1, n2, c = a.shape
    cb = min(cb, c)
    return pl.pallas_call(
        _spectrum_mid_kernel,
        grid=(n1, c // cb),
        in_specs=[pl.BlockSpec((None, 2 * n2, 2 * n2), lambda i, j: (i, 0, 0)),
                  pl.BlockSpec((2, None, n2, cb), lambda i, j: (0, i, 0, j)),
                  pl.BlockSpec((1, cb), lambda i, j: (0, j))],
        out_specs=pl.BlockSpec((2, None, n2, cb), lambda i, j: (0, i, 0, j)),
        out_shape=jax.ShapeDtypeStruct(a.shape, F32),
        compiler_params=_params("parallel", "parallel"),
        name="spectrum_mid",
    )(g, a, scale)


def _conv_mid_kernel(g_ref, gi_ref, a_ref, k_ref, o_ref):
    bblk, _, n2, cb = a_ref.shape
    kr, ki = k_ref[0], k_ref[1]
    for bi in range(bblk):
        a = a_ref[bi].reshape(2 * n2, cb)
        x = jnp.dot(g_ref[...], a, preferred_element_type=F32)
        xr, xi = x[:n2], x[n2:]
        y = jnp.concatenate([xr * kr - xi * ki, xr * ki + xi * kr], axis=0).astype(BF16)
        z = jnp.dot(gi_ref[...], y, preferred_element_type=F32)
        o_ref[bi] = z.reshape(2, n2, cb).astype(o_ref.dtype)


def conv_mid(g, gi, a, kf, block_bytes=2 << 20):
    b, _, n1, n2, c = a.shape
    bblk = max(1, min(b, block_bytes // (2 * n2 * c * 2)))
    assert b % bblk == 0
    return pl.pallas_call(
        _conv_mid_kernel,
        grid=(n1, b // bblk),
        in_specs=[pl.BlockSpec((None, 2 * n2, 2 * n2), lambda i, bb: (i, 0, 0)),
                  pl.BlockSpec((None, 2 * n2, 2 * n2), lambda i, bb: (i, 0, 0)),
                  pl.BlockSpec((bblk, 2, None, n2, c), lambda i, bb: (bb, 0, i, 0, 0)),
                  pl.BlockSpec((2, None, n2, c), lambda i, bb: (0, i, 0, 0))],
        out_specs=pl.BlockSpec((bblk, 2, None, n2, c), lambda i, bb: (bb, 0, i, 0, 0)),
        out_shape=jax.ShapeDtypeStruct(a.shape, BF16),
        compiler_params=_params("parallel", "parallel"),
        name="conv_mid",
    )(g, gi, a, kf)


def _conv_out_kernel(f_ref, z_ref, w_ref, x0_ref, bias_ref, o_ref):
    y = jnp.dot(f_ref[...], z_ref[...], preferred_element_type=F32)
    o_ref[...] = x0_ref[...] * (y + w_ref[...] * bias_ref[...])


def conv_out(f, z, w, x0, bias_l, ln=4096):
    b, k, lanes = z.shape
    m = f.shape[0]
    ln = min(ln, lanes)
    assert lanes % ln == 0 and bias_l.shape == (1, ln)
    return pl.pallas_call(
        _conv_out_kernel,
        grid=(b, lanes // ln),
        in_specs=[pl.BlockSpec((m, k), lambda i, j: (0, 0)),
                  pl.BlockSpec((None, k, ln), lambda i, j: (i, 0, j)),
                  pl.BlockSpec((None, m, ln), lambda i, j: (i, 0, j)),
                  pl.BlockSpec((None, m, ln), lambda i, j: (i, 0, j)),
                  pl.BlockSpec((1, ln), lambda i, j: (0, 0))],
        out_specs=pl.BlockSpec((None, m, ln), lambda i, j: (i, 0, j)),
        out_shape=jax.ShapeDtypeStruct((b, m, lanes), F32),
        compiler_params=_params("parallel", "parallel"),
        name="conv_out",
    )(f, z, w, x0, bias_l)


def _hyena_pre_kernel(x0_ref, x1_ref, v_ref, w0_ref, w1_ref, wv_ref, b0_ref, b1_ref, bv_ref, x0o_ref, wo_ref):
    seq = x0_ref.shape[0]
    row = lax.broadcasted_iota(jnp.int32, x0_ref.shape, 0)
    first, last = row == 0, row == seq - 1

    def conv(u_ref, w_ref, b_ref):
        u = u_ref[...]
        prev = jnp.where(first, 0.0, pltpu.roll(u, 1, axis=0))
        nxt = jnp.where(last, 0.0, pltpu.roll(u, seq - 1, axis=0))
        return b_ref[...] + prev * w_ref[0:1, :] + u * w_ref[1:2, :] + nxt * w_ref[2:3, :]

    x0o_ref[...] = conv(x0_ref, w0_ref, b0_ref)
    wo_ref[...] = conv(v_ref, wv_ref, bv_ref) * conv(x1_ref, w1_ref, b1_ref)


def hyena_pre(hproj, conv_w, conv_b):
    b, seq, _ = hproj.shape
    c = HYENA_WIDTH
    cb = max(LANES, min(c, (1 << 20) // seq))
    nb = c // cb
    u_spec = lambda g: pl.BlockSpec((None, seq, cb), lambda i, j: (i, 0, g * nb + j))
    w_spec = lambda g: pl.BlockSpec((3, cb), lambda i, j: (0, g * nb + j))
    b_spec = lambda g: pl.BlockSpec((1, cb), lambda i, j: (0, g * nb + j))
    o_spec = pl.BlockSpec((None, seq, cb), lambda i, j: (i, 0, j))
    cw = conv_w.astype(F32)
    cbias = conv_b.reshape(1, 3 * c).astype(F32)
    return pl.pallas_call(
        _hyena_pre_kernel,
        grid=(b, nb),
        in_specs=[u_spec(0), u_spec(1), u_spec(2), w_spec(0), w_spec(1), w_spec(2),
                  b_spec(0), b_spec(1), b_spec(2)],
        out_specs=[o_spec, o_spec],
        out_shape=[jax.ShapeDtypeStruct((b, seq, c), F32)] * 2,
        compiler_params=_params("parallel", "parallel"),
        name="hyena_pre",
    )(hproj, hproj, hproj, cw, cw, cw, cbias, cbias, cbias)


def hyena_branch(hproj, conv_w, conv_b, fw1, fb1, fw2, fb2, fw3, fb3, fw4, freq, hyena_bias):
    b, seq, _ = hproj.shape
    c = HYENA_WIDTH
    n = 2 * seq
    n1, n2 = _fft_split(n)
    n1h = n1 // 2
    cos1, sin1 = _slow_dft_tables(n1)
    g = _fast_dft_tables(n1, n2)
    gi = jnp.swapaxes(g, 1, 2)

    h, l1 = filter_mlp(seq, fw1, fb1, fw2, fb2, fw3, fb3, fw4, freq)
    h = h.reshape(seq, 2, c)
    k_full = jnp.concatenate([h[:, 0], jnp.zeros((1, c), F32), h[1:, 1][::-1]], axis=0)
    l1c = l1[:, :c] + l1[:, c:]
    scale = 1.0 / ((l1c + EPS) * n)
    fa_full = jnp.concatenate([cos1, -sin1], axis=0)
    ka = left_matmul(fa_full, k_full.reshape(1, n1, n2 * c), F32, precise=True)
    kf = spectrum_mid(g, ka.reshape(2, n1, n2, c), scale)

    x0c, w = hyena_pre(hproj, conv_w, conv_b)
    fa = fa_full[:, :n1h].astype(BF16)
    a = left_matmul(fa, w.reshape(b, n1h, n2 * c), BF16, precise=False)
    z = conv_mid(g.astype(BF16), gi.astype(BF16), a.reshape(b, 2, n1, n2, c), kf)
    fc = jnp.concatenate([cos1[:n1h], -sin1[:n1h]], axis=1).astype(BF16)
    ln = min(4096, n2 * c)
    bias_l = jnp.tile(hyena_bias.reshape(1, c).astype(F32), (1, ln // c))
    y = conv_out(fc, z.reshape(b, 2 * n1, n2 * c), w.reshape(b, n1h, n2 * c), x0c.reshape(b, n1h, n2 * c),
                 bias_l, ln=ln)
    return y.reshape(b, seq, c)


ATT_SUB = 128


def _dilated_attn_kernel(q_ref, kp_ref, kc_ref, kn_ref, vp_ref, vc_ref, vn_ref, o_ref, l_ref, *,
                         n_cls, dilation, slopes):
    tq = q_ref.shape[0]
    q0 = pl.program_id(2) * tq
    sk = ATT_SUB + 2 * ATT_HALF
    scale = 1.0 / math.sqrt(HEAD_DIM)
    qi = lax.broadcasted_iota(jnp.int32, (ATT_SUB, sk), 0)
    kj = lax.broadcasted_iota(jnp.int32, (ATT_SUB, sk), 1) - ATT_HALF
    dist = jnp.abs(kj - qi)
    band = dist <= ATT_HALF
    adist = (dilation * dist).astype(F32)
    lane = lax.broadcasted_iota(jnp.int32, (sk, LANES), 1)
    low_k = lane < HEAD_DIM
    low_q = lax.broadcasted_iota(jnp.int32, (ATT_SUB, LANES), 1) < HEAD_DIM
    for sub in range(tq // ATT_SUB):
        r0 = sub * ATT_SUB
        kabs = q0 + r0 + kj
        mask = band & (kabs >= 0) & (kabs < n_cls)
        for pair in range(HEADS_PER_GROUP // 2):
            cs = slice(pair * LANES, (pair + 1) * LANES)
            q = q_ref[r0:r0 + ATT_SUB, cs]

            def keys(p_ref, c_ref, n_ref):
                parts = []
                if r0 == 0:
                    parts.append(p_ref[:, cs])
                    parts.append(c_ref[0:min(tq, ATT_SUB + ATT_HALF), cs])
                else:
                    parts.append(c_ref[r0 - ATT_HALF:min(tq, r0 + ATT_SUB + ATT_HALF), cs])
                if r0 + ATT_SUB + ATT_HALF > tq:
                    parts.append(n_ref[:, cs])
                return jnp.concatenate(parts, axis=0)

            k = keys(kp_ref, kc_ref, kn_ref)
            v = keys(vp_ref, vc_ref, vn_ref)
            acc = None
            ls, lses = [], []
            for e in range(2):
                sel = low_k if e == 0 else ~low_k
                ke = jnp.where(sel, k, jnp.zeros_like(k))
                ve = jnp.where(sel, v, jnp.zeros_like(v))
                s = lax.dot_general(q, ke, (((1,), (1,)), ((), ())), preferred_element_type=F32) * scale
                s = jnp.where(mask, s - slopes[2 * pair + e] * adist, NEG_INF)
                m = jnp.max(s, axis=-1, keepdims=True)
                p = jnp.exp(s - m)
                l = jnp.sum(p, axis=-1, keepdims=True)
                pv = jnp.dot(p.astype(BF16), ve, preferred_element_type=F32)
                acc = pv if acc is None else acc + pv
                ls.append(l)
                lses.append(m + jnp.log(l))
            inv = 1.0 / jnp.where(low_q, ls[0], ls[1])
            o_ref[r0:r0 + ATT_SUB, cs] = acc * inv
            l_ref[r0:r0 + ATT_SUB, cs] = jnp.where(low_q, lses[0], lses[1])


def dilated_attention(aproj, group, tq=256):
    b, seq, _ = aproj.shape
    window, dilation = ATT_GROUPS[group]
    assert window // (2 * dilation) == ATT_HALF
    n_cls = seq // dilation
    tq = min(tq, n_cls)
    assert n_cls % tq == 0 and tq % ATT_SUB == 0
    nblk = ATT_COLS // ATT_WIDTH
    hb = tq // ATT_HALF
    n_halo = n_cls // ATT_HALF
    slopes = tuple(2.0 ** (-8.0 * (group * HEADS_PER_GROUP + h + 1.0) / N_ATT_HEADS)
                   for h in range(HEADS_PER_GROUP))
    x = aproj.reshape(b, n_cls, dilation * ATT_COLS)
    col = lambda which: (lambda r: r * nblk + which * N_GROUPS + group)
    cur = lambda which: pl.BlockSpec((None, tq, ATT_WIDTH), lambda i, r, j: (i, j, col(which)(r)))
    prev = lambda which: pl.BlockSpec((None, ATT_HALF, ATT_WIDTH),
                                      lambda i, r, j: (i, jnp.maximum(j * hb - 1, 0), col(which)(r)))
    nxt = lambda which: pl.BlockSpec((None, ATT_HALF, ATT_WIDTH),
                                     lambda i, r, j: (i, jnp.minimum((j + 1) * hb, n_halo - 1), col(which)(r)))
    o_spec = pl.BlockSpec((None, tq, ATT_WIDTH), lambda i, r, j: (i, j, r))
    o, l = pl.pallas_call(
        functools.partial(_dilated_attn_kernel, n_cls=n_cls, dilation=dilation, slopes=slopes),
        grid=(b, dilation, n_cls // tq),
        in_specs=[cur(0), prev(1), cur(1), nxt(1), prev(2), cur(2), nxt(2)],
        out_specs=[o_spec, o_spec],
        out_shape=[jax.ShapeDtypeStruct((b, n_cls, dilation * ATT_WIDTH), F32)] * 2,
        compiler_params=_params("parallel", "parallel", "parallel"),
        name="dilated_attn",
    )(x, x, x, x, x, x, x)
    return o.reshape(b * seq, ATT_WIDTH), l.reshape(b * seq, ATT_WIDTH)


def _merge_kernel(x_ref, ga_ref, gb_ref, ya_ref, o0_ref, o1_ref, o2_ref, l0_ref, l1_ref, l2_ref,
                  wa_ref, wb_ref, wo_ref, g2_ref, x1_ref, xn_ref):
    l0, l1, l2 = l0_ref[...], l1_ref[...], l2_ref[...]
    m = jnp.maximum(jnp.maximum(l0, l1), l2)
    e0, e1, e2 = jnp.exp(l0 - m), jnp.exp(l1 - m), jnp.exp(l2 - m)
    yb = (e0 * o0_ref[...] + e1 * o1_ref[...] + e2 * o2_ref[...]) / (e0 + e1 + e2)
    pa = jnp.dot(ya_ref[...].astype(BF16), wa_ref[...], preferred_element_type=F32)
    pb = jnp.dot(yb.astype(BF16), wb_ref[...], preferred_element_type=F32)
    merged = jax.nn.sigmoid(ga_ref[...]) * pa + jax.nn.sigmoid(gb_ref[...]) * pb
    x1 = x_ref[...] + jnp.dot(merged.astype(BF16), wo_ref[...], preferred_element_type=F32)
    x1_ref[...] = x1
    ms = jnp.mean(x1 * x1, axis=-1, keepdims=True)
    xn_ref[...] = (x1 * lax.rsqrt(ms + EPS) * g2_ref[...]).astype(BF16)


def merge(x, glog, ya, outs, lses, wa, wb, wo, g2, tm=256):
    t, d = x.shape
    tm = min(tm, t)
    assert t % tm == 0
    row = lambda c, blk=0: pl.BlockSpec((tm, c), lambda i: (i, blk))
    const = lambda r, c: pl.BlockSpec((r, c), lambda i: (0, 0), pipeline_mode=pl.Buffered(1))
    aw = ATT_WIDTH
    return pl.pallas_call(
        _merge_kernel,
        grid=(t // tm,),
        in_specs=[row(d), row(d, 0), row(d, 1), row(HYENA_WIDTH), row(aw), row(aw), row(aw), row(aw), row(aw),
                  row(aw), const(HYENA_WIDTH, d), const(aw, d), const(d, d), const(1, d)],
        out_specs=[row(d), row(d)],
        out_shape=[jax.ShapeDtypeStruct((t, d), F32), jax.ShapeDtypeStruct((t, d), BF16)],
        compiler_params=_params("parallel"),
        name="merge",
    )(x, glog, glog, ya, *outs, *lses, wa, wb, wo, g2.reshape(1, d).astype(F32))


def _extract_top(cur, k, out_ref, row0):
    for it in range(k):
        m = jnp.max(cur, axis=0, keepdims=True)
        out_ref[row0 + it:row0 + it + 1, :] = m
        if it + 1 < k:
            cur = jnp.where(cur == m, -jnp.inf, cur)


def _peer_scores_kernel(xn_ref, wq_ref, sk_ref, g1_ref, g2_ref, top_ref, cand_ref, s_ref):
    k = PEER_TOPK
    nh = PEER_HEADS * N_KEYS
    q = jnp.dot(xn_ref[...], wq_ref[...], preferred_element_type=F32).astype(BF16)
    for tc in range(xn_ref.shape[0] // LANES):
        cols = slice(tc * LANES, (tc + 1) * LANES)
        for h in range(PEER_HEADS):
            rows = slice(h * N_KEYS, (h + 1) * N_KEYS)
            rows_e = slice(nh + h * N_KEYS, nh + (h + 1) * N_KEYS)
            for p in range(2):
                hp = 2 * h + p
                qs = q[cols, hp * N_KEYS:(hp + 1) * N_KEYS]
                s = lax.dot_general(sk_ref[hp], qs, (((1,), (1,)), ((), ())), preferred_element_type=F32)
                s_ref[p] = s
                _extract_top(s, k, top_ref, k * p)
            v2 = top_ref[k:2 * k, :]
            for r in range(k):
                cand_ref[r * k:(r + 1) * k, :] = top_ref[r:r + 1, :] + v2
            cur = cand_ref[...]
            tau = None
            for it in range(k):
                tau = jnp.max(cur, axis=0, keepdims=True)
                if it + 1 < k:
                    cur = jnp.where(cur == tau, -jnp.inf, cur)
            best = top_ref[0:1, :] + top_ref[k:k + 1, :]
            cand = cand_ref[...]
            zsum = jnp.sum(jnp.where(cand >= tau, jnp.exp(cand - best), 0.0), axis=0, keepdims=True)
            s1, s2 = s_ref[0], s_ref[1]
            theta = jnp.full(s1.shape, jnp.inf, F32)
            for r in range(k):
                c = cand_ref[r * k:(r + 1) * k, :]
                theta_r = jnp.min(jnp.where(c >= tau, v2, jnp.inf), axis=0, keepdims=True)
                theta = jnp.where(s1 == top_ref[r:r + 1, :], theta_r, theta)
            g1_ref[rows, cols] = theta
            g1_ref[rows_e, cols] = jnp.exp(s1 - top_ref[0:1, :])
            g2_ref[tc, rows, :] = s2
            g2_ref[tc, rows_e, :] = jnp.exp(s2 - top_ref[k:k + 1, :]) / zsum


def peer_scores(xn, wq, subkeys, tm=256):
    t, d = xn.shape
    tm = min(tm, t)
    assert t % tm == 0 and tm % LANES == 0
    rows = 2 * PEER_HEADS * N_KEYS
    return pl.pallas_call(
        _peer_scores_kernel,
        grid=(t // tm,),
        in_specs=[pl.BlockSpec((tm, d), lambda i: (i, 0)),
                  pl.BlockSpec((d, rows), lambda i: (0, 0), pipeline_mode=pl.Buffered(1)),
                  pl.BlockSpec((2 * PEER_HEADS, N_KEYS, N_KEYS), lambda i: (0, 0, 0))],
        out_specs=[pl.BlockSpec((rows, tm), lambda i: (0, i)),
                   pl.BlockSpec((tm // LANES, rows, LANES), lambda i: (i, 0, 0))],
        out_shape=[jax.ShapeDtypeStruct((rows, t), F32), jax.ShapeDtypeStruct((t // LANES, rows, LANES), F32)],
        scratch_shapes=[pltpu.VMEM((2 * PEER_TOPK, LANES), F32), pltpu.VMEM((PEER_TOPK * PEER_TOPK, LANES), F32),
                        pltpu.VMEM((2, N_KEYS, LANES), F32)],
        compiler_params=_params("parallel"),
        name="peer_scores",
    )(xn, wq, subkeys)


GELU_C0 = math.sqrt(2.0 / math.pi)
GELU_C1 = GELU_C0 * 0.044715


def _gelu_tanh(x):
    return x * (0.5 + 0.5 * jnp.tanh(x * (GELU_C0 + GELU_C1 * (x * x))))


def _peer_experts_kernel(xn_ref, g1_ref, g2_ref, u_ref, vt_ref, x1_ref, g_ref, y_ref, acc_ref, a_ref, wa_ref):
    j = pl.program_id(1)
    nblk = pl.num_programs(1) - 2
    _, nchunk, eb, _ = a_ref.shape
    tm = nchunk * LANES
    a_even, a_odd = a_ref.at[0], a_ref.at[1]
    wa_even, wa_odd = wa_ref.at[0], wa_ref.at[1]
    nh = PEER_HEADS * N_KEYS
    per = eb // N_KEYS
    half = tm // 2

    def activations(a_ref, c):
        cols = slice(c * half, (c + 1) * half)
        a = lax.dot_general(u_ref[...], xn_ref[cols, :], (((1,), (1,)), ((), ())), preferred_element_type=F32)
        for k in range(half // LANES):
            a_ref[c * (half // LANES) + k] = a[:, k * LANES:(k + 1) * LANES]

    def gates(a_ref, wa_ref, block, ii):
        key1 = block * per + ii
        rows = slice(ii * N_KEYS, (ii + 1) * N_KEYS)
        theta_rows = [g1_ref[pl.ds(h * N_KEYS + key1, 1), :] for h in range(PEER_HEADS)]
        e1_rows = [g1_ref[pl.ds(nh + h * N_KEYS + key1, 1), :] for h in range(PEER_HEADS)]
        for tc in range(nchunk):
            cols = slice(tc * LANES, (tc + 1) * LANES)
            wsum = None
            for h in range(PEER_HEADS):
                s2 = g2_ref[tc, h * N_KEYS:(h + 1) * N_KEYS, :]
                e2 = g2_ref[tc, nh + h * N_KEYS:nh + (h + 1) * N_KEYS, :]
                term = e1_rows[h][:, cols] * jnp.where(s2 >= theta_rows[h][:, cols], e2, 0.0)
                wsum = term if wsum is None else wsum + term
            wa_ref[rows, cols] = (_gelu_tanh(a_ref[tc, rows, :]) * wsum).astype(BF16)

    def outputs(wa_ref, c):
        cols = slice(c * half, (c + 1) * half)
        acc_ref[:, cols] += jnp.dot(vt_ref[...], wa_ref[:, cols], preferred_element_type=F32)

    def step(act, gate, out):
        vec = [functools.partial(gates, *gate, j - 1, ii) for ii in range(per)] if gate else []
        mxu = [functools.partial(activations, act, c) for c in range(2)] if act is not None else []
        mxu += [functools.partial(outputs, out, c) for c in range(2)] if out is not None else []
        n = max(len(vec), len(mxu))
        for k in range(n):
            for stage in (vec, mxu):
                for piece in stage[k * len(stage) // n:(k + 1) * len(stage) // n]:
                    piece()

    steady = (j > 1) & (j < nblk)

    @pl.when(j == 0)
    def _():
        acc_ref[...] = jnp.zeros_like(acc_ref)
        step(a_even, None, None)

    @pl.when(j == 1)
    def _():
        step(a_odd, (a_even, wa_even), None)

    @pl.when(steady & (j % 2 == 0))
    def _():
        step(a_even, (a_odd, wa_odd), wa_even)

    @pl.when(steady & (j % 2 == 1))
    def _():
        step(a_odd, (a_even, wa_even), wa_odd)

    @pl.when(j == nblk)
    def _():
        step(None, (a_odd, wa_odd), wa_even)

    @pl.when(j == nblk + 1)
    def _():
        step(None, None, wa_odd)
        x2 = x1_ref[...] + acc_ref[...].T
        ms = jnp.mean(x2 * x2, axis=-1, keepdims=True)
        y_ref[...] = x2 * lax.rsqrt(ms + EPS) * g_ref[...]


def peer_experts(xn, g1, g2, u, vt, x1, final_g, tm=512, eb=512):
    t, d = xn.shape
    tm = min(tm, t)
    assert t % tm == 0 and N_EXPERTS % (2 * eb) == 0 and eb % N_KEYS == 0 and tm % (2 * LANES) == 0
    rows = 2 * PEER_HEADS * N_KEYS
    nblk = N_EXPERTS // eb
    return pl.pallas_call(
        _peer_experts_kernel,
        grid=(t // tm, nblk + 2),
        in_specs=[pl.BlockSpec((tm, d), lambda i, j: (i, 0)),
                  pl.BlockSpec((rows, tm), lambda i, j: (0, i)),
                  pl.BlockSpec((tm // LANES, rows, LANES), lambda i, j: (i, 0, 0)),
                  pl.BlockSpec((eb, d), lambda i, j: (jnp.minimum(j, nblk - 1), 0)),
                  pl.BlockSpec((d, eb), lambda i, j: (0, jnp.clip(j - 2, 0, nblk - 1))),
                  pl.BlockSpec((tm, d), lambda i, j: (i, 0), pipeline_mode=pl.Buffered(1)),
                  pl.BlockSpec((1, d), lambda i, j: (0, 0))],
        out_specs=pl.BlockSpec((tm, d), lambda i, j: (i, 0)),
        out_shape=jax.ShapeDtypeStruct((t, d), F32),
        scratch_shapes=[pltpu.VMEM((d, tm), F32), pltpu.VMEM((2, tm // LANES, eb, LANES), F32),
                        pltpu.VMEM((2, eb, tm), BF16)],
        compiler_params=_params("parallel", "arbitrary"),
        name="peer_experts",
    )(xn, g1, g2, u, vt, x1, final_g.reshape(1, d).astype(F32))


def _encoder(x, p):
    b, seq, d = x.shape
    t = b * seq
    xt = x.reshape(t, d)
    hproj = norm_matmul(xt, p["norm1_g"], p["w_in"], 0, HYENA_COLS, F32)
    aproj = norm_matmul(xt, p["norm1_g"], p["w_in"], HYENA_COLS, ATT_COLS, BF16)
    glog = norm_matmul(xt, p["norm1_g"], p["w_in"], HYENA_COLS + ATT_COLS, GATE_COLS, F32)
    ya = hyena_branch(hproj.reshape(b, seq, HYENA_COLS), p["conv_w"], p["conv_b"], p["filt_w1"], p["filt_b1"],
                      p["filt_w2"], p["filt_b2"], p["filt_w3"], p["filt_b3"], p["filt_w4"], p["filt_freq"],
                      p["hyena_bias"])
    att = [dilated_attention(aproj.reshape(b, seq, ATT_COLS), g) for g in range(N_GROUPS)]
    x1, xn2 = merge(xt, glog, ya.reshape(t, HYENA_WIDTH), [o for o, _ in att], [l for _, l in att],
                    p["w_branch_a"], p["w_branch_b"], p["w_out"], p["norm2_g"])
    g1, g2 = peer_scores(xn2, p["peer_wq"], p["peer_subkeys"])
    y = peer_experts(xn2, g1, g2, p["peer_u"], p["peer_vt"], x1, p["final_g"])
    return y.reshape(b, seq, d)


def kernel(x_prompt, x_sample, norm1_g, w_in, conv_w, conv_b, filt_w1, filt_b1, filt_w2, filt_b2, filt_w3,
           filt_b3, filt_w4, filt_freq, hyena_bias, w_branch_a, w_branch_b, w_out, norm2_g, peer_wq,
           peer_subkeys, peer_u, peer_v, final_g):
    assert norm1_g.shape[0] == 1, "single-layer encoder"
    p = dict(
        norm1_g=norm1_g[0].astype(F32), w_in=w_in[0].astype(BF16), conv_w=conv_w[0], conv_b=conv_b[0],
        filt_w1=filt_w1[0], filt_b1=filt_b1[0], filt_w2=filt_w2[0], filt_b2=filt_b2[0], filt_w3=filt_w3[0],
        filt_b3=filt_b3[0], filt_w4=filt_w4[0], filt_freq=filt_freq[0], hyena_bias=hyena_bias[0],
        w_branch_a=w_branch_a[0].astype(BF16), w_branch_b=w_branch_b[0].astype(BF16),
        w_out=w_out[0].astype(BF16), norm2_g=norm2_g[0], peer_wq=peer_wq[0].astype(BF16),
        peer_subkeys=peer_subkeys[0].reshape(2 * PEER_HEADS, N_KEYS, N_KEYS).astype(BF16),
        peer_u=peer_u[0].astype(BF16), peer_vt=peer_v[0].astype(BF16).T, final_g=final_g,
    )
    return (_encoder(x_prompt, p), _encoder(x_sample, p))
```

```python
import functools
import math

import jax
import jax.numpy as jnp
from jax import lax
from jax.experimental import pallas as pl
from jax.experimental.pallas import tpu as pltpu

F32 = jnp.float32
BF16 = jnp.bfloat16

D_MODEL = 2048
HYENA_WIDTH = 1024
FILTER_EMB = 33
FILTER_EMB_PAD = 128
FILTER_HIDDEN = 64
DECAY_TARGET = 1e-2
FAST_DECAY_PCT = 0.3
SLOW_DECAY_PCT = 1.5
ATT_GROUPS = ((128, 1), (512, 4), (2048, 16))
N_GROUPS = 3
HEADS_PER_GROUP = 8
HEAD_DIM = 64
N_ATT_HEADS = N_GROUPS * HEADS_PER_GROUP
ATT_WIDTH = HEADS_PER_GROUP * HEAD_DIM
HYENA_COLS = 3 * HYENA_WIDTH
ATT_COLS = 3 * N_GROUPS * ATT_WIDTH
GATE_COLS = 2 * D_MODEL
PEER_HEADS = 8
N_KEYS = 128
N_EXPERTS = N_KEYS * N_KEYS
PEER_TOPK = 16
EPS = 1e-6
NEG_INF = -1e30
ATT_HALF = 64
LANES = 128
VMEM_LIMIT = 56 * 1024 * 1024
HIGHEST = lax.Precision.HIGHEST


def _params(*sem, flags=None):
    return pltpu.CompilerParams(dimension_semantics=sem, vmem_limit_bytes=VMEM_LIMIT, flags=flags)


def _norm_matmul_kernel(x_ref, g_ref, w_ref, o_ref, xn_ref):
    @pl.when(pl.program_id(1) == 0)
    def _():
        x = x_ref[...]
        ms = jnp.mean(x * x, axis=-1, keepdims=True)
        xn_ref[...] = (x * lax.rsqrt(ms + EPS) * g_ref[...]).astype(BF16)

    o_ref[...] = jnp.dot(xn_ref[...], w_ref[...], preferred_element_type=F32).astype(o_ref.dtype)


def norm_matmul(x, g, w, col0, ncols, out_dtype, tm=1024, tn=512):
    t, d = x.shape
    tm = min(tm, t)
    assert t % tm == 0 and ncols % tn == 0 and col0 % tn == 0
    cb0 = col0 // tn
    return pl.pallas_call(
        _norm_matmul_kernel,
        grid=(t // tm, ncols // tn),
        in_specs=[
            pl.BlockSpec((tm, d), lambda i, j: (i, 0)),
            pl.BlockSpec((1, d), lambda i, j: (0, 0)),
            pl.BlockSpec((d, tn), lambda i, j: (0, cb0 + j)),
        ],
        out_specs=pl.BlockSpec((tm, tn), lambda i, j: (i, j)),
        out_shape=jax.ShapeDtypeStruct((t, ncols), out_dtype),
        scratch_shapes=[pltpu.VMEM((tm, d), BF16)],
        compiler_params=_params("parallel", "arbitrary"),
        name="norm_matmul",
    )(x, g.reshape(1, d), w)


def _norm_matmul_classes_kernel(x_ref, g_ref, w_ref, o_ref, xn_ref, acc_ref, *, dilation):
    which = pl.program_id(1)

    @pl.when(which == 0)
    def _():
        x = x_ref[...]
        ms = jnp.mean(x * x, axis=-1, keepdims=True)
        xn_ref[...] = (x * lax.rsqrt(ms + EPS) * g_ref[...]).astype(BF16)

    acc = jnp.dot(xn_ref[...], w_ref[...], preferred_element_type=F32)
    nlane, tm, _ = acc_ref.shape
    for k in range(nlane):
        acc_ref[k] = acc[:, k * LANES:(k + 1) * LANES]
    rows = tm // dilation
    for part in range(3):
        @pl.when(which == part)
        def _(part=part):
            for r in range(dilation):
                for k in range(nlane):
                    piece = acc_ref[k] if dilation == 1 else acc_ref[k, pl.ds(r, rows, stride=dilation), :]
                    c0 = (3 * r + part) * nlane * LANES + k * LANES
                    o_ref[:, c0:c0 + LANES] = piece.astype(o_ref.dtype)


def norm_matmul_classes(x, g, w, group, tm=1024):
    t, d = x.shape
    dilation = ATT_GROUPS[group][1]
    tm = min(tm, t)
    assert t % tm == 0 and tm % (8 * dilation) == 0
    cb0 = HYENA_COLS // ATT_WIDTH + group
    return pl.pallas_call(
        functools.partial(_norm_matmul_classes_kernel, dilation=dilation),
        grid=(t // tm, 3),
        in_specs=[
            pl.BlockSpec((tm, d), lambda i, j: (i, 0)),
            pl.BlockSpec((1, d), lambda i, j: (0, 0)),
            pl.BlockSpec((d, ATT_WIDTH), lambda i, j: (0, cb0 + N_GROUPS * j)),
        ],
        out_specs=pl.BlockSpec((tm // dilation, dilation * 3 * ATT_WIDTH), lambda i, j: (i, 0)),
        out_shape=jax.ShapeDtypeStruct((t // dilation, dilation * 3 * ATT_WIDTH), BF16),
        scratch_shapes=[pltpu.VMEM((tm, d), BF16), pltpu.VMEM((ATT_WIDTH // LANES, tm, LANES), F32)],
        compiler_params=_params("parallel", "arbitrary"),
        name="norm_matmul_classes",
    )(x, g.reshape(1, d), w)


def _filter_mlp_kernel(z_ref, t_ref, w1_ref, b1_ref, w2_ref, b2_ref, w3_ref, b3_ref, w4_ref, fr_ref,
                       dl_ref, h_ref, l1_ref):
    fr = fr_ref[...]
    h = jnp.sin(fr * (jnp.dot(z_ref[...], w1_ref[...], precision=HIGHEST, preferred_element_type=F32)
                      + b1_ref[...]))
    h = jnp.sin(fr * (jnp.dot(h, w2_ref[...], precision=HIGHEST, preferred_element_type=F32) + b2_ref[...]))
    h = jnp.sin(fr * (jnp.dot(h, w3_ref[...], precision=HIGHEST, preferred_element_type=F32) + b3_ref[...]))
    h = jnp.dot(h, w4_ref[...], precision=HIGHEST, preferred_element_type=F32)
    h = h * jnp.exp(-t_ref[...] * dl_ref[...])
    h_ref[...] = h

    @pl.when(pl.program_id(0) == 0)
    def _():
        l1_ref[...] = jnp.zeros_like(l1_ref)

    l1_ref[...] += jnp.sum(jnp.abs(h), axis=0, keepdims=True)


def filter_mlp(seq, w1, b1, w2, b2, w3, b3, w4, freq, tl=512):
    t = jnp.linspace(0.0, 1.0, seq, dtype=F32)[:, None]
    bands = (FILTER_EMB - 1) // 2
    ang = 2.0 * math.pi * jnp.arange(seq, dtype=F32)[:, None] / seq
    fb = jnp.linspace(1e-4, bands - 1, bands, dtype=F32)[None, :]
    z = jnp.concatenate([t, jnp.cos(fb * ang), -jnp.sin(fb * ang)], axis=-1)
    z = jnp.pad(z, ((0, 0), (0, FILTER_EMB_PAD - FILTER_EMB)))
    w1p = jnp.pad(w1.astype(F32), ((0, FILTER_EMB_PAD - FILTER_EMB), (0, 0)))
    min_decay = math.log(DECAY_TARGET) / SLOW_DECAY_PCT
    max_decay = math.log(DECAY_TARGET) / FAST_DECAY_PCT
    deltas = jnp.abs(jnp.linspace(min_decay, max_decay, HYENA_WIDTH, dtype=F32))
    deltas2 = jnp.concatenate([deltas, deltas])[None, :]
    tl = min(tl, seq)
    assert seq % tl == 0
    c2 = 2 * HYENA_WIDTH
    hid = FILTER_HIDDEN
    full = lambda r, c: pl.BlockSpec((r, c), lambda i: (0, 0))
    return pl.pallas_call(
        _filter_mlp_kernel,
        grid=(seq // tl,),
        in_specs=[
            pl.BlockSpec((tl, FILTER_EMB_PAD), lambda i: (i, 0)),
            pl.BlockSpec((tl, 1), lambda i: (i, 0)),
            full(FILTER_EMB_PAD, hid), full(1, hid), full(hid, hid), full(1, hid), full(hid, hid),
            full(1, hid), full(hid, c2), full(1, hid), full(1, c2),
        ],
        out_specs=[pl.BlockSpec((tl, c2), lambda i: (i, 0)), pl.BlockSpec((1, c2), lambda i: (0, 0))],
        out_shape=[jax.ShapeDtypeStruct((seq, c2), F32), jax.ShapeDtypeStruct((1, c2), F32)],
        compiler_params=_params("arbitrary"),
        name="filter_mlp",
    )(z, t, w1p, b1.reshape(1, hid).astype(F32), w2.astype(F32), b2.reshape(1, hid).astype(F32),
      w3.astype(F32), b3.reshape(1, hid).astype(F32), w4.astype(F32), freq.reshape(1, hid).astype(F32), deltas2)


def _fft_split(n):
    lg = int(math.log2(n))
    assert 1 << lg == n
    n1 = 1 << ((lg + 1) // 2)
    return n1, n // n1


def _slow_dft_tables(n1):
    a = jnp.arange(n1, dtype=jnp.int32)
    ang = ((a[:, None] * a[None, :]) % n1).astype(F32) * (2.0 * math.pi / n1)
    return jnp.cos(ang), jnp.sin(ang)


def _fast_dft_tables(n1, n2):
    n = n1 * n2
    k1 = jnp.arange(n1, dtype=jnp.int32)[:, None, None]
    k2 = jnp.arange(n2, dtype=jnp.int32)[None, :, None]
    f = jnp.arange(n2, dtype=jnp.int32)[None, None, :]
    ang = ((f * (k1 + n1 * k2)) % n).astype(F32) * (2.0 * math.pi / n)
    c, s = jnp.cos(ang), jnp.sin(ang)
    top = jnp.concatenate([c, s], axis=2)
    bot = jnp.concatenate([-s, c], axis=2)
    return jnp.concatenate([top, bot], axis=1)


def _left_matmul_kernel(f_ref, x_ref, o_ref, *, precise):
    if precise:
        o = jnp.dot(f_ref[...], x_ref[...], precision=HIGHEST, preferred_element_type=F32)
    else:
        o = jnp.dot(f_ref[...], x_ref[...].astype(BF16), preferred_element_type=F32)
    o_ref[...] = o.astype(o_ref.dtype)


def left_matmul(f, x, out_dtype, precise, ln=4096):
    b, k, lanes = x.shape
    m = f.shape[0]
    ln = min(ln, lanes)
    assert lanes % ln == 0 and f.shape[1] == k
    return pl.pallas_call(
        functools.partial(_left_matmul_kernel, precise=precise),
        grid=(b, lanes // ln),
        in_specs=[pl.BlockSpec((m, k), lambda i, j: (0, 0)),
                  pl.BlockSpec((None, k, ln), lambda i, j: (i, 0, j))],
        out_specs=pl.BlockSpec((None, m, ln), lambda i, j: (i, 0, j)),
        out_shape=jax.ShapeDtypeStruct((b, m, lanes), out_dtype),
        compiler_params=_params("parallel", "parallel"),
        name="left_matmul",
    )(f, x)


def _spectrum_mid_kernel(g_ref, a_ref, sc_ref, o_ref):
    n2 = a_ref.shape[1]
    a = a_ref[...].reshape(2 * n2, a_ref.shape[2])
    x = jnp.dot(g_ref[...], a, precision=HIGHEST, preferred_element_type=F32) * sc_ref[...]
    o_ref[...] = x.reshape(o_ref.shape)


def spectrum_mid(g, a, scale, cb=512):
    _, n1, n2, c = a.shape
    cb = min(cb, c)
    return pl.pallas_call(
        _spectrum_mid_kernel,
        grid=(n1, c // cb),
        in_specs=[pl.BlockSpec((None, 2 * n2, 2 * n2), lambda i, j: (i, 0, 0)),
                  pl.BlockSpec((2, None, n2, cb), lambda i, j: (0, i, 0, j)),
                  pl.BlockSpec((1, cb), lambda i, j: (0, j))],
        out_specs=pl.BlockSpec((2, None, n2, cb), lambda i, j: (0, i, 0, j)),
        out_shape=jax.ShapeDtypeStruct(a.shape, F32),
        compiler_params=_params("parallel", "parallel"),
        name="spectrum_mid",
    )(g, a, scale)


def _conv_mid_kernel(g_ref, gi_ref, a_ref, k_ref, o_ref):
    bblk, _, n2, cb = a_ref.shape
    kr, ki = k_ref[0], k_ref[1]
    for bi in range(bblk):
        a = a_ref[bi].reshape(2 * n2, cb)
        x = jnp.dot(g_ref[...], a, preferred_element_type=F32)
        xr, xi = x[:n2], x[n2:]
        y = jnp.concatenate([xr * kr - xi * ki, xr * ki + xi * kr], axis=0).astype(BF16)
        z = jnp.dot(gi_ref[...], y, preferred_element_type=F32)
        o_ref[bi] = z.reshape(2, n2, cb).astype(o_ref.dtype)


def conv_mid(g, gi, a, kf, block_bytes=2 << 20):
    b, _, n1, n2, c = a.shape
    bblk = max(1, min(b, block_bytes // (2 * n2 * c * 2)))
    assert b % bblk == 0
    return pl.pallas_call(
        _conv_mid_kernel,
        grid=(n1, b // bblk),
        in_specs=[pl.BlockSpec((None, 2 * n2, 2 * n2), lambda i, bb: (i, 0, 0)),
                  pl.BlockSpec((None, 2 * n2, 2 * n2), lambda i, bb: (i, 0, 0)),
                  pl.BlockSpec((bblk, 2, None, n2, c), lambda i, bb: (bb, 0, i, 0, 0)),
                  pl.BlockSpec((2, None, n2, c), lambda i, bb: (0, i, 0, 0))],
        out_specs=pl.BlockSpec((bblk, 2, None, n2, c), lambda i, bb: (bb, 0, i, 0, 0)),
        out_shape=jax.ShapeDtypeStruct(a.shape, BF16),
        compiler_params=_params("parallel", "parallel"),
        name="conv_mid",
    )(g, gi, a, kf)


def _conv_out_kernel(f_ref, z_ref, w_ref, x0_ref, bias_ref, o_ref):
    y = jnp.dot(f_ref[...], z_ref[...], preferred_element_type=F32)
    o_ref[...] = x0_ref[...] * (y + w_ref[...] * bias_ref[...])


def conv_out(f, z, w, x0, bias_l, ln=4096):
    b, k, lanes = z.shape
    m = f.shape[0]
    ln = min(ln, lanes)
    assert lanes % ln == 0 and bias_l.shape == (1, ln)
    return pl.pallas_call(
        _conv_out_kernel,
        grid=(b, lanes // ln),
        in_specs=[pl.BlockSpec((m, k), lambda i, j: (0, 0)),
                  pl.BlockSpec((None, k, ln), lambda i, j: (i, 0, j)),
                  pl.BlockSpec((None, m, ln), lambda i, j: (i, 0, j)),
                  pl.BlockSpec((None, m, ln), lambda i, j: (i, 0, j)),
                  pl.BlockSpec((1, ln), lambda i, j: (0, 0))],
        out_specs=pl.BlockSpec((None, m, ln), lambda i, j: (i, 0, j)),
        out_shape=jax.ShapeDtypeStruct((b, m, lanes), F32),
        compiler_params=_params("parallel", "parallel"),
        name="conv_out",
    )(f, z, w, x0, bias_l)


def _hyena_pre_kernel(x0_ref, x1_ref, v_ref, w0_ref, w1_ref, wv_ref, b0_ref, b1_ref, bv_ref, x0o_ref, wo_ref):
    seq = x0_ref.shape[0]
    row = lax.broadcasted_iota(jnp.int32, x0_ref.shape, 0)
    first, last = row == 0, row == seq - 1

    def conv(u_ref, w_ref, b_ref):
        u = u_ref[...]
        prev = jnp.where(first, 0.0, pltpu.roll(u, 1, axis=0))
        nxt = jnp.where(last, 0.0, pltpu.roll(u, seq - 1, axis=0))
        return b_ref[...] + prev * w_ref[0:1, :] + u * w_ref[1:2, :] + nxt * w_ref[2:3, :]

    x0o_ref[...] = conv(x0_ref, w0_ref, b0_ref)
    wo_ref[...] = conv(v_ref, wv_ref, bv_ref) * conv(x1_ref, w1_ref, b1_ref)


def hyena_pre(hproj, conv_w, conv_b):
    b, seq, _ = hproj.shape
    c = HYENA_WIDTH
    cb = max(LANES, min(c, (1 << 20) // seq))
    nb = c // cb
    u_spec = lambda g: pl.BlockSpec((None, seq, cb), lambda i, j: (i, 0, g * nb + j))
    w_spec = lambda g: pl.BlockSpec((3, cb), lambda i, j: (0, g * nb + j))
    b_spec = lambda g: pl.BlockSpec((1, cb), lambda i, j: (0, g * nb + j))
    o_spec = pl.BlockSpec((None, seq, cb), lambda i, j: (i, 0, j))
    cw = conv_w.astype(F32)
    cbias = conv_b.reshape(1, 3 * c).astype(F32)
    return pl.pallas_call(
        _hyena_pre_kernel,
        grid=(b, nb),
        in_specs=[u_spec(0), u_spec(1), u_spec(2), w_spec(0), w_spec(1), w_spec(2),
                  b_spec(0), b_spec(1), b_spec(2)],
        out_specs=[o_spec, o_spec],
        out_shape=[jax.ShapeDtypeStruct((b, seq, c), F32)] * 2,
        compiler_params=_params("parallel", "parallel"),
        name="hyena_pre",
    )(hproj, hproj, hproj, cw, cw, cw, cbias, cbias, cbias)


def hyena_branch(hproj, conv_w, conv_b, fw1, fb1, fw2, fb2, fw3, fb3, fw4, freq, hyena_bias):
    b, seq, _ = hproj.shape
    c = HYENA_WIDTH
    n = 2 * seq
    n1, n2 = _fft_split(n)
    n1h = n1 // 2
    cos1, sin1 = _slow_dft_tables(n1)
    g = _fast_dft_tables(n1, n2)
    gi = jnp.swapaxes(g, 1, 2)

    h, l1 = filter_mlp(seq, fw1, fb1, fw2, fb2, fw3, fb3, fw4, freq)
    h = h.reshape(seq, 2, c)
    k_full = jnp.concatenate([h[:, 0], jnp.zeros((1, c), F32), h[1:, 1][::-1]], axis=0)
    l1c = l1[:, :c] + l1[:, c:]
    scale = 1.0 / ((l1c + EPS) * n)
    fa_full = jnp.concatenate([cos1, -sin1], axis=0)
    ka = left_matmul(fa_full, k_full.reshape(1, n1, n2 * c), F32, precise=True)
    kf = spectrum_mid(g, ka.reshape(2, n1, n2, c), scale)

    x0c, w = hyena_pre(hproj, conv_w, conv_b)
    fa = fa_full[:, :n1h].astype(BF16)
    a = left_matmul(fa, w.reshape(b, n1h, n2 * c), BF16, precise=False)
    z = conv_mid(g.astype(BF16), gi.astype(BF16), a.reshape(b, 2, n1, n2, c), kf)
    fc = jnp.concatenate([cos1[:n1h], -sin1[:n1h]], axis=1).astype(BF16)
    ln = min(4096, n2 * c)
    bias_l = jnp.tile(hyena_bias.reshape(1, c).astype(F32), (1, ln // c))
    y = conv_out(fc, z.reshape(b, 2 * n1, n2 * c), w.reshape(b, n1h, n2 * c), x0c.reshape(b, n1h, n2 * c),
                 bias_l, ln=ln)
    return y.reshape(b, seq, c)


ATT_SUB = 128


def _dilated_attn_kernel(q_ref, kp_ref, kc_ref, kn_ref, vp_ref, vc_ref, vn_ref, o_ref, l_ref, *,
                         n_cls, dilation, slopes):
    tq = q_ref.shape[0]
    q0 = pl.program_id(1) * tq
    res = pl.program_id(2)
    sk = ATT_SUB + 2 * ATT_HALF
    scale = 1.0 / math.sqrt(HEAD_DIM)
    qi = lax.broadcasted_iota(jnp.int32, (ATT_SUB, sk), 0)
    kj = lax.broadcasted_iota(jnp.int32, (ATT_SUB, sk), 1) - ATT_HALF
    dist = jnp.abs(kj - qi)
    band = dist <= ATT_HALF
    adist = (dilation * dist).astype(F32)
    lane = lax.broadcasted_iota(jnp.int32, (sk, LANES), 1)
    low_k = lane < HEAD_DIM
    low_q = lax.broadcasted_iota(jnp.int32, (ATT_SUB, LANES), 1) < HEAD_DIM
    for sub in range(tq // ATT_SUB):
        r0 = sub * ATT_SUB
        kabs = q0 + r0 + kj
        mask = band & (kabs >= 0) & (kabs < n_cls)
        for pair in range(HEADS_PER_GROUP // 2):
            cs = slice(pair * LANES, (pair + 1) * LANES)
            q = q_ref[r0:r0 + ATT_SUB, cs]

            def keys(p_ref, c_ref, n_ref):
                parts = []
                if r0 == 0:
                    parts.append(p_ref[:, cs])
                    parts.append(c_ref[0:min(tq, ATT_SUB + ATT_HALF), cs])
                else:
                    parts.append(c_ref[r0 - ATT_HALF:min(tq, r0 + ATT_SUB + ATT_HALF), cs])
                if r0 + ATT_SUB + ATT_HALF > tq:
                    parts.append(n_ref[:, cs])
                return jnp.concatenate(parts, axis=0)

            k = keys(kp_ref, kc_ref, kn_ref)
            v = keys(vp_ref, vc_ref, vn_ref)
            acc = None
            ls, lses = [], []
            for e in range(2):
                sel = low_k if e == 0 else ~low_k
                ke = jnp.where(sel, k, jnp.zeros_like(k))
                ve = jnp.where(sel, v, jnp.zeros_like(v))
                s = lax.dot_general(q, ke, (((1,), (1,)), ((), ())), preferred_element_type=F32) * scale
                s = jnp.where(mask, s - slopes[2 * pair + e] * adist, NEG_INF)
                m = jnp.max(s, axis=-1, keepdims=True)
                p = jnp.exp(s - m)
                l = jnp.sum(p, axis=-1, keepdims=True)
                pv = jnp.dot(p.astype(BF16), ve, preferred_element_type=F32)
                acc = pv if acc is None else acc + pv
                ls.append(l)
                lses.append(m + jnp.log(l))
            inv = 1.0 / jnp.where(low_q, ls[0], ls[1])
            if dilation == 1:
                tok = slice(r0, r0 + ATT_SUB)
            else:
                tok = pl.ds(r0 * dilation + res, ATT_SUB, stride=dilation)
            o_ref[pair, tok, :] = acc * inv
            l_ref[pair, tok, :] = jnp.where(low_q, lses[0], lses[1])


def dilated_attention(qkv, batch, group, out_block_bytes=4 << 20):
    window, dilation = ATT_GROUPS[group]
    assert window // (2 * dilation) == ATT_HALF
    n_cls = qkv.shape[0] // batch
    seq = n_cls * dilation
    tq = min(256, n_cls, max(ATT_SUB, out_block_bytes // (dilation * ATT_WIDTH * 4)))
    assert n_cls % tq == 0 and tq % ATT_SUB == 0
    hb = tq // ATT_HALF
    n_halo = n_cls // ATT_HALF
    slopes = tuple(2.0 ** (-8.0 * (group * HEADS_PER_GROUP + h + 1.0) / N_ATT_HEADS)
                   for h in range(HEADS_PER_GROUP))
    x = qkv.reshape(batch, n_cls, dilation * 3 * ATT_WIDTH)
    cur = lambda which: pl.BlockSpec((None, tq, ATT_WIDTH), lambda i, j, r: (i, j, 3 * r + which))
    prev = lambda which: pl.BlockSpec((None, ATT_HALF, ATT_WIDTH),
                                      lambda i, j, r: (i, jnp.maximum(j * hb - 1, 0), 3 * r + which))
    nxt = lambda which: pl.BlockSpec((None, ATT_HALF, ATT_WIDTH),
                                     lambda i, j, r: (i, jnp.minimum((j + 1) * hb, n_halo - 1), 3 * r + which))
    npair = ATT_WIDTH // LANES
    o_spec = pl.BlockSpec((None, npair, tq * dilation, LANES), lambda i, j, r: (i, 0, j, 0))
    return pl.pallas_call(
        functools.partial(_dilated_attn_kernel, n_cls=n_cls, dilation=dilation, slopes=slopes),
        grid=(batch, n_cls // tq, dilation),
        in_specs=[cur(0), prev(1), cur(1), nxt(1), prev(2), cur(2), nxt(2)],
        out_specs=[o_spec, o_spec],
        out_shape=[jax.ShapeDtypeStruct((batch, npair, seq, LANES), F32)] * 2,
        compiler_params=_params("parallel", "parallel", "arbitrary"),
        name="dilated_attn",
    )(x, x, x, x, x, x, x)


def _merge_kernel(x_ref, ga_ref, gb_ref, ya_ref, o0_ref, o1_ref, o2_ref, l0_ref, l1_ref, l2_ref,
                  wa_ref, wb_ref, wo_ref, g2_ref, x1_ref, xn_ref):
    parts = []
    for pair in range(o0_ref.shape[0]):
        l0, l1, l2 = l0_ref[pair], l1_ref[pair], l2_ref[pair]
        m = jnp.maximum(jnp.maximum(l0, l1), l2)
        e0, e1, e2 = jnp.exp(l0 - m), jnp.exp(l1 - m), jnp.exp(l2 - m)
        yb = (e0 * o0_ref[pair] + e1 * o1_ref[pair] + e2 * o2_ref[pair]) / (e0 + e1 + e2)
        parts.append(yb.astype(BF16))
    pa = jnp.dot(ya_ref[...].astype(BF16), wa_ref[...], preferred_element_type=F32)
    pb = jnp.dot(jnp.concatenate(parts, axis=1), wb_ref[...], preferred_element_type=F32)
    merged = jax.nn.sigmoid(ga_ref[...]) * pa + jax.nn.sigmoid(gb_ref[...]) * pb
    x1 = x_ref[...] + jnp.dot(merged.astype(BF16), wo_ref[...], preferred_element_type=F32)
    x1_ref[...] = x1
    ms = jnp.mean(x1 * x1, axis=-1, keepdims=True)
    xn_ref[...] = (x1 * lax.rsqrt(ms + EPS) * g2_ref[...]).astype(BF16)


def merge(x, glog, ya, outs, lses, wa, wb, wo, g2, tm=256):
    t, d = x.shape
    _, npair, seq, _ = outs[0].shape
    tm = min(tm, seq)
    assert seq % tm == 0
    tiles = seq // tm
    row = lambda c, blk=0: pl.BlockSpec((tm, c), lambda i: (i, blk))
    const = lambda r, c: pl.BlockSpec((r, c), lambda i: (0, 0), pipeline_mode=pl.Buffered(1))
    att = pl.BlockSpec((None, npair, tm, LANES), lambda i: (i // tiles, 0, i % tiles, 0))
    aw = ATT_WIDTH
    return pl.pallas_call(
        _merge_kernel,
        grid=(t // tm,),
        in_specs=[row(d), row(d, 0), row(d, 1), row(HYENA_WIDTH), att, att, att, att, att, att,
                  const(HYENA_WIDTH, d), const(aw, d), const(d, d), const(1, d)],
        out_specs=[row(d), row(d)],
        out_shape=[jax.ShapeDtypeStruct((t, d), F32), jax.ShapeDtypeStruct((t, d), BF16)],
        compiler_params=_params("parallel"),
        name="merge",
    )(x, glog, glog, ya, *outs, *lses, wa, wb, wo, g2.reshape(1, d).astype(F32))


def _extract_top(cur, k, out_ref, row0):
    for it in range(k):
        m = jnp.max(cur, axis=0, keepdims=True)
        out_ref[row0 + it:row0 + it + 1, :] = m
        if it + 1 < k:
            cur = jnp.where(cur == m, -jnp.inf, cur)


def _peer_scores_kernel(xn_ref, wq_ref, sk_ref, g1_ref, g2_ref, top_ref, cand_ref, s_ref):
    k = PEER_TOPK
    nh = PEER_HEADS * N_KEYS
    q = jnp.dot(xn_ref[...], wq_ref[...], preferred_element_type=F32).astype(BF16)
    for tc in range(xn_ref.shape[0] // LANES):
        cols = slice(tc * LANES, (tc + 1) * LANES)
        for h in range(PEER_HEADS):
            rows = slice(h * N_KEYS, (h + 1) * N_KEYS)
            rows_e = slice(nh + h * N_KEYS, nh + (h + 1) * N_KEYS)
            for p in range(2):
                hp = 2 * h + p
                qs = q[cols, hp * N_KEYS:(hp + 1) * N_KEYS]
                s = lax.dot_general(sk_ref[hp], qs, (((1,), (1,)), ((), ())), preferred_element_type=F32)
                s_ref[p] = s
                _extract_top(s, k, top_ref, k * p)
            v2 = top_ref[k:2 * k, :]
            for r in range(k):
                cand_ref[r * k:(r + 1) * k, :] = top_ref[r:r + 1, :] + v2
            cur = cand_ref[...]
            tau = None
            for it in range(k):
                tau = jnp.max(cur, axis=0, keepdims=True)
                if it + 1 < k:
                    cur = jnp.where(cur == tau, -jnp.inf, cur)
            best = top_ref[0:1, :] + top_ref[k:k + 1, :]
            cand = cand_ref[...]
            zsum = jnp.sum(jnp.where(cand >= tau, jnp.exp(cand - best), 0.0), axis=0, keepdims=True)
            s1, s2 = s_ref[0], s_ref[1]
            inv_z = 1.0 / zsum
            e2_top = jnp.exp(v2 - top_ref[k:k + 1, :]) * inv_z
            theta = jnp.full(s1.shape, jnp.inf, F32)
            for r in range(k):
                c = cand_ref[r * k:(r + 1) * k, :]
                theta_r = jnp.min(jnp.where(c >= tau, e2_top, jnp.inf), axis=0, keepdims=True)
                theta = jnp.where(s1 == top_ref[r:r + 1, :], theta_r, theta)
            g1_ref[rows, cols] = theta
            g1_ref[rows_e, cols] = jnp.exp(s1 - top_ref[0:1, :])
            g2_ref[tc, rows, :] = jnp.exp(s2 - top_ref[k:k + 1, :]) * inv_z


def peer_scores(xn, wq, subkeys, tm=256):
    t, d = xn.shape
    tm = min(tm, t)
    assert t % tm == 0 and tm % LANES == 0
    rows = 2 * PEER_HEADS * N_KEYS
    return pl.pallas_call(
        _peer_scores_kernel,
        grid=(t // tm,),
        in_specs=[pl.BlockSpec((tm, d), lambda i: (i, 0)),
                  pl.BlockSpec((d, rows), lambda i: (0, 0), pipeline_mode=pl.Buffered(1)),
                  pl.BlockSpec((2 * PEER_HEADS, N_KEYS, N_KEYS), lambda i: (0, 0, 0))],
        out_specs=[pl.BlockSpec((rows, tm), lambda i: (0, i)),
                   pl.BlockSpec((tm // LANES, rows // 2, LANES), lambda i: (i, 0, 0))],
        out_shape=[jax.ShapeDtypeStruct((rows, t), F32),
                   jax.ShapeDtypeStruct((t // LANES, rows // 2, LANES), F32)],
        scratch_shapes=[pltpu.VMEM((2 * PEER_TOPK, LANES), F32), pltpu.VMEM((PEER_TOPK * PEER_TOPK, LANES), F32),
                        pltpu.VMEM((2, N_KEYS, LANES), F32)],
        compiler_params=_params("parallel"),
        name="peer_scores",
    )(xn, wq, subkeys)


GELU_C0 = math.sqrt(2.0 / math.pi)
GELU_C1 = GELU_C0 * 0.044715


def _gelu_tanh(x):
    return x * (0.5 + 0.5 * jnp.tanh(x * (GELU_C0 + GELU_C1 * (x * x))))


def _peer_experts_kernel(xn_ref, g1_ref, g2_ref, u_ref, vt_ref, x1_ref, g_ref, y_ref, acc_ref, a_ref, wa_ref):
    j = pl.program_id(1)
    nblk = pl.num_programs(1) - 2
    _, nchunk, eb, _ = a_ref.shape
    tm = nchunk * LANES
    a_even, a_odd = a_ref.at[0], a_ref.at[1]
    wa_even, wa_odd = wa_ref.at[0], wa_ref.at[1]
    nh = PEER_HEADS * N_KEYS
    per = eb // N_KEYS
    half = tm // 2

    mrows = 256
    arows = 256

    def activations(a_ref, m, c):
        rows = slice(m * arows, (m + 1) * arows)
        cols = slice(c * half, (c + 1) * half)
        a = lax.dot_general(u_ref[rows, :], xn_ref[cols, :], (((1,), (1,)), ((), ())),
                            preferred_element_type=F32)
        for k in range(half // LANES):
            a_ref[c * (half // LANES) + k, rows, :] = a[:, k * LANES:(k + 1) * LANES]

    def gate_rows(block, ii):
        key1 = block * per + ii
        return ([g1_ref[pl.ds(h * N_KEYS + key1, 1), :] for h in range(PEER_HEADS)],
                [g1_ref[pl.ds(nh + h * N_KEYS + key1, 1), :] for h in range(PEER_HEADS)])

    def gates(a_ref, wa_ref, theta_rows, e1_rows, ii, tc):
        rows = slice(ii * N_KEYS, (ii + 1) * N_KEYS)
        cols = slice(tc * LANES, (tc + 1) * LANES)
        wsum = None
        for h in range(PEER_HEADS):
            e2 = g2_ref[tc, h * N_KEYS:(h + 1) * N_KEYS, :]
            term = e1_rows[h][:, cols] * jnp.where(e2 >= theta_rows[h][:, cols], e2, 0.0)
            wsum = term if wsum is None else wsum + term
        wa_ref[rows, cols] = (_gelu_tanh(a_ref[tc, rows, :]) * wsum).astype(BF16)

    def outputs(wa_ref, r, c):
        rows = slice(r * mrows, (r + 1) * mrows)
        cols = slice(c * half, (c + 1) * half)
        acc_ref[rows, cols] += jnp.dot(vt_ref[rows, :], wa_ref[:, cols], preferred_element_type=F32)

    def step(act, gate, out):
        vec, mxu_a, mxu_o = [], [], []
        rows_cache = {}

        def gate_piece(ii, tc):
            if ii not in rows_cache:
                rows_cache[ii] = gate_rows(j - 1, ii)
            gates(*gate, *rows_cache[ii], ii, tc)

        if gate:
            vec = [functools.partial(gate_piece, ii, tc) for ii in range(per) for tc in range(nchunk)]
        if act is not None:
            mxu_a = [functools.partial(activations, act, m, c) for m in range(eb // arows) for c in range(2)]
        if out is not None:
            mxu_o = [functools.partial(outputs, out, r, c) for c in range(2) for r in range(acc_ref.shape[0] // mrows)]
        n = max(len(vec), len(mxu_a), len(mxu_o))
        for k in range(n):
            for stage in (vec, mxu_a, mxu_o):
                for piece in stage[k * len(stage) // n:(k + 1) * len(stage) // n]:
                    piece()

    steady = (j > 1) & (j < nblk)

    @pl.when(j == 0)
    def _():
        acc_ref[...] = jnp.zeros_like(acc_ref)
        step(a_even, None, None)

    @pl.when(j == 1)
    def _():
        step(a_odd, (a_even, wa_even), None)

    @pl.when(steady & (j % 2 == 0))
    def _():
        step(a_even, (a_odd, wa_odd), wa_even)

    @pl.when(steady & (j % 2 == 1))
    def _():
        step(a_odd, (a_even, wa_even), wa_odd)

    @pl.when(j == nblk)
    def _():
        step(None, (a_odd, wa_odd), wa_even)

    @pl.when(j == nblk + 1)
    def _():
        step(None, None, wa_odd)
        x2 = x1_ref[...] + acc_ref[...].T
        ms = jnp.mean(x2 * x2, axis=-1, keepdims=True)
        y_ref[...] = x2 * lax.rsqrt(ms + EPS) * g_ref[...]


def peer_experts(xn, g1, g2, u, vt, x1, final_g, tm=512, eb=512):
    t, d = xn.shape
    tm = min(tm, t)
    assert t % tm == 0 and N_EXPERTS % (2 * eb) == 0 and eb % N_KEYS == 0 and tm % (2 * LANES) == 0
    rows = 2 * PEER_HEADS * N_KEYS
    nblk = N_EXPERTS // eb
    return pl.pallas_call(
        _peer_experts_kernel,
        grid=(t // tm, nblk + 2),
        in_specs=[pl.BlockSpec((tm, d), lambda i, j: (i, 0)),
                  pl.BlockSpec((rows, tm), lambda i, j: (0, i)),
                  pl.BlockSpec((tm // LANES, rows // 2, LANES), lambda i, j: (i, 0, 0)),
                  pl.BlockSpec((eb, d), lambda i, j: (jnp.minimum(j, nblk - 1), 0)),
                  pl.BlockSpec((d, eb), lambda i, j: (0, jnp.clip(j - 2, 0, nblk - 1))),
                  pl.BlockSpec((tm, d), lambda i, j: (i, 0), pipeline_mode=pl.Buffered(1)),
                  pl.BlockSpec((1, d), lambda i, j: (0, 0))],
        out_specs=pl.BlockSpec((tm, d), lambda i, j: (i, 0)),
        out_shape=jax.ShapeDtypeStruct((t, d), F32),
        scratch_shapes=[pltpu.VMEM((d, tm), F32), pltpu.VMEM((2, tm // LANES, eb, LANES), F32),
                        pltpu.VMEM((2, eb, tm), BF16)],
        compiler_params=_params("parallel", "arbitrary"),
        name="peer_experts",
    )(xn, g1, g2, u, vt, x1, final_g.reshape(1, d).astype(F32))


def _encoder(x, p):
    b, seq, d = x.shape
    t = b * seq
    xt = x.reshape(t, d)
    hproj = norm_matmul(xt, p["norm1_g"], p["w_in"], 0, HYENA_COLS, F32)
    glog = norm_matmul(xt, p["norm1_g"], p["w_in"], HYENA_COLS + ATT_COLS, GATE_COLS, F32)
    ya = hyena_branch(hproj.reshape(b, seq, HYENA_COLS), p["conv_w"], p["conv_b"], p["filt_w1"], p["filt_b1"],
                      p["filt_w2"], p["filt_b2"], p["filt_w3"], p["filt_b3"], p["filt_w4"], p["filt_freq"],
                      p["hyena_bias"])
    att = [dilated_attention(norm_matmul_classes(xt, p["norm1_g"], p["w_in"], g), b, g) for g in range(N_GROUPS)]
    x1, xn2 = merge(xt, glog, ya.reshape(t, HYENA_WIDTH), [o for o, _ in att], [l for _, l in att],
                    p["w_branch_a"], p["w_branch_b"], p["w_out"], p["norm2_g"])
    g1, g2 = peer_scores(xn2, p["peer_wq"], p["peer_subkeys"])
    y = peer_experts(xn2, g1, g2, p["peer_u"], p["peer_vt"], x1, p["final_g"])
    return y.reshape(b, seq, d)


def kernel(x_prompt, x_sample, norm1_g, w_in, conv_w, conv_b, filt_w1, filt_b1, filt_w2, filt_b2, filt_w3,
           filt_b3, filt_w4, filt_freq, hyena_bias, w_branch_a, w_branch_b, w_out, norm2_g, peer_wq,
           peer_subkeys, peer_u, peer_v, final_g):
    assert norm1_g.shape[0] == 1, "single-layer encoder"
    p = dict(
        norm1_g=norm1_g[0].astype(F32), w_in=w_in[0].astype(BF16), conv_w=conv_w[0], conv_b=conv_b[0],
        filt_w1=filt_w1[0], filt_b1=filt_b1[0], filt_w2=filt_w2[0], filt_b2=filt_b2[0], filt_w3=filt_w3[0],
        filt_b3=filt_b3[0], filt_w4=filt_w4[0], filt_freq=filt_freq[0], hyena_bias=hyena_bias[0],
        w_branch_a=w_branch_a[0].astype(BF16), w_branch_b=w_branch_b[0].astype(BF16),
        w_out=w_out[0].astype(BF16), norm2_g=norm2_g[0], peer_wq=peer_wq[0].astype(BF16),
        peer_subkeys=peer_subkeys[0].reshape(2 * PEER_HEADS, N_KEYS, N_KEYS).astype(BF16),
        peer_u=peer_u[0].astype(BF16), peer_vt=peer_v[0].astype(BF16).T, final_g=final_g,
    )
    return (_encoder(x_prompt, p), _encoder(x_sample, p))
```

```python
import functools
import math

import jax
import jax.numpy as jnp
from jax import lax
from jax.experimental import pallas as pl
from jax.experimental.pallas import tpu as pltpu

F32 = jnp.float32
BF16 = jnp.bfloat16

D_MODEL = 2048
HYENA_WIDTH = 1024
FILTER_EMB = 33
FILTER_EMB_PAD = 128
FILTER_HIDDEN = 64
DECAY_TARGET = 1e-2
FAST_DECAY_PCT = 0.3
SLOW_DECAY_PCT = 1.5
ATT_GROUPS = ((128, 1), (512, 4), (2048, 16))
N_GROUPS = 3
HEADS_PER_GROUP = 8
HEAD_DIM = 64
N_ATT_HEADS = N_GROUPS * HEADS_PER_GROUP
ATT_WIDTH = HEADS_PER_GROUP * HEAD_DIM
HYENA_COLS = 3 * HYENA_WIDTH
ATT_COLS = 3 * N_GROUPS * ATT_WIDTH
GATE_COLS = 2 * D_MODEL
PEER_HEADS = 8
N_KEYS = 128
N_EXPERTS = N_KEYS * N_KEYS
PEER_TOPK = 16
PEER_EB = 512
EPS = 1e-6
NEG_INF = -1e30
ATT_HALF = 64
LANES = 128
VMEM_LIMIT = 56 * 1024 * 1024
HIGHEST = lax.Precision.HIGHEST


def _params(*sem, flags=None):
    return pltpu.CompilerParams(dimension_semantics=sem, vmem_limit_bytes=VMEM_LIMIT, flags=flags)


def _norm_matmul_kernel(x_ref, g_ref, w_ref, o_ref, xn_ref):
    @pl.when(pl.program_id(1) == 0)
    def _():
        x = x_ref[...]
        ms = jnp.mean(x * x, axis=-1, keepdims=True)
        xn_ref[...] = (x * lax.rsqrt(ms + EPS) * g_ref[...]).astype(BF16)

    o_ref[...] = jnp.dot(xn_ref[...], w_ref[...], preferred_element_type=F32).astype(o_ref.dtype)


def norm_matmul(x, g, w, col0, ncols, out_dtype, tm=1024, tn=512):
    t, d = x.shape
    tm = min(tm, t)
    assert t % tm == 0 and ncols % tn == 0 and col0 % tn == 0
    cb0 = col0 // tn
    return pl.pallas_call(
        _norm_matmul_kernel,
        grid=(t // tm, ncols // tn),
        in_specs=[
            pl.BlockSpec((tm, d), lambda i, j: (i, 0)),
            pl.BlockSpec((1, d), lambda i, j: (0, 0)),
            pl.BlockSpec((d, tn), lambda i, j: (0, cb0 + j)),
        ],
        out_specs=pl.BlockSpec((tm, tn), lambda i, j: (i, j)),
        out_shape=jax.ShapeDtypeStruct((t, ncols), out_dtype),
        scratch_shapes=[pltpu.VMEM((tm, d), BF16)],
        compiler_params=_params("parallel", "arbitrary"),
        name="norm_matmul",
    )(x, g.reshape(1, d), w)


def _norm_matmul_classes_kernel(x_ref, g_ref, w_ref, o_ref, xn_ref, acc_ref, *, dilation):
    which = pl.program_id(1)

    @pl.when(which == 0)
    def _():
        x = x_ref[...]
        ms = jnp.mean(x * x, axis=-1, keepdims=True)
        xn_ref[...] = (x * lax.rsqrt(ms + EPS) * g_ref[...]).astype(BF16)

    acc = jnp.dot(xn_ref[...], w_ref[...], preferred_element_type=F32)
    nlane, tm, _ = acc_ref.shape
    for k in range(nlane):
        acc_ref[k] = acc[:, k * LANES:(k + 1) * LANES]
    rows = tm // dilation
    for part in range(3):
        @pl.when(which == part)
        def _(part=part):
            for r in range(dilation):
                for k in range(nlane):
                    piece = acc_ref[k] if dilation == 1 else acc_ref[k, pl.ds(r, rows, stride=dilation), :]
                    c0 = (3 * r + part) * nlane * LANES + k * LANES
                    o_ref[:, c0:c0 + LANES] = piece.astype(o_ref.dtype)


def norm_matmul_classes(x, g, w, group, tm=1024):
    t, d = x.shape
    dilation = ATT_GROUPS[group][1]
    tm = min(tm, t)
    assert t % tm == 0 and tm % (8 * dilation) == 0
    cb0 = HYENA_COLS // ATT_WIDTH + group
    return pl.pallas_call(
        functools.partial(_norm_matmul_classes_kernel, dilation=dilation),
        grid=(t // tm, 3),
        in_specs=[
            pl.BlockSpec((tm, d), lambda i, j: (i, 0)),
            pl.BlockSpec((1, d), lambda i, j: (0, 0)),
            pl.BlockSpec((d, ATT_WIDTH), lambda i, j: (0, cb0 + N_GROUPS * j)),
        ],
        out_specs=pl.BlockSpec((tm // dilation, dilation * 3 * ATT_WIDTH), lambda i, j: (i, 0)),
        out_shape=jax.ShapeDtypeStruct((t // dilation, dilation * 3 * ATT_WIDTH), BF16),
        scratch_shapes=[pltpu.VMEM((tm, d), BF16), pltpu.VMEM((ATT_WIDTH // LANES, tm, LANES), F32)],
        compiler_params=_params("parallel", "arbitrary"),
        name="norm_matmul_classes",
    )(x, g.reshape(1, d), w)


def _filter_mlp_kernel(z_ref, t_ref, w1_ref, b1_ref, w2_ref, b2_ref, w3_ref, b3_ref, w4_ref, fr_ref,
                       dl_ref, h_ref, l1_ref):
    fr = fr_ref[...]
    h = jnp.sin(fr * (jnp.dot(z_ref[...], w1_ref[...], precision=HIGHEST, preferred_element_type=F32)
                      + b1_ref[...]))
    h = jnp.sin(fr * (jnp.dot(h, w2_ref[...], precision=HIGHEST, preferred_element_type=F32) + b2_ref[...]))
    h = jnp.sin(fr * (jnp.dot(h, w3_ref[...], precision=HIGHEST, preferred_element_type=F32) + b3_ref[...]))
    h = jnp.dot(h, w4_ref[...], precision=HIGHEST, preferred_element_type=F32)
    h = h * jnp.exp(-t_ref[...] * dl_ref[...])
    h_ref[...] = h

    @pl.when(pl.program_id(0) == 0)
    def _():
        l1_ref[...] = jnp.zeros_like(l1_ref)

    l1_ref[...] += jnp.sum(jnp.abs(h), axis=0, keepdims=True)


def filter_mlp(seq, w1, b1, w2, b2, w3, b3, w4, freq, tl=512):
    t = jnp.linspace(0.0, 1.0, seq, dtype=F32)[:, None]
    bands = (FILTER_EMB - 1) // 2
    ang = 2.0 * math.pi * jnp.arange(seq, dtype=F32)[:, None] / seq
    fb = jnp.linspace(1e-4, bands - 1, bands, dtype=F32)[None, :]
    z = jnp.concatenate([t, jnp.cos(fb * ang), -jnp.sin(fb * ang)], axis=-1)
    z = jnp.pad(z, ((0, 0), (0, FILTER_EMB_PAD - FILTER_EMB)))
    w1p = jnp.pad(w1.astype(F32), ((0, FILTER_EMB_PAD - FILTER_EMB), (0, 0)))
    min_decay = math.log(DECAY_TARGET) / SLOW_DECAY_PCT
    max_decay = math.log(DECAY_TARGET) / FAST_DECAY_PCT
    deltas = jnp.abs(jnp.linspace(min_decay, max_decay, HYENA_WIDTH, dtype=F32))
    deltas2 = jnp.concatenate([deltas, deltas])[None, :]
    tl = min(tl, seq)
    assert seq % tl == 0
    c2 = 2 * HYENA_WIDTH
    hid = FILTER_HIDDEN
    full = lambda r, c: pl.BlockSpec((r, c), lambda i: (0, 0))
    return pl.pallas_call(
        _filter_mlp_kernel,
        grid=(seq // tl,),
        in_specs=[
            pl.BlockSpec((tl, FILTER_EMB_PAD), lambda i: (i, 0)),
            pl.BlockSpec((tl, 1), lambda i: (i, 0)),
            full(FILTER_EMB_PAD, hid), full(1, hid), full(hid, hid), full(1, hid), full(hid, hid),
            full(1, hid), full(hid, c2), full(1, hid), full(1, c2),
        ],
        out_specs=[pl.BlockSpec((tl, c2), lambda i: (i, 0)), pl.BlockSpec((1, c2), lambda i: (0, 0))],
        out_shape=[jax.ShapeDtypeStruct((seq, c2), F32), jax.ShapeDtypeStruct((1, c2), F32)],
        compiler_params=_params("arbitrary"),
        name="filter_mlp",
    )(z, t, w1p, b1.reshape(1, hid).astype(F32), w2.astype(F32), b2.reshape(1, hid).astype(F32),
      w3.astype(F32), b3.reshape(1, hid).astype(F32), w4.astype(F32), freq.reshape(1, hid).astype(F32), deltas2)


def _fft_split(n):
    lg = int(math.log2(n))
    assert 1 << lg == n
    n1 = 1 << ((lg + 1) // 2)
    return n1, n // n1


def _slow_dft_tables(n1):
    a = jnp.arange(n1, dtype=jnp.int32)
    ang = ((a[:, None] * a[None, :]) % n1).astype(F32) * (2.0 * math.pi / n1)
    return jnp.cos(ang), jnp.sin(ang)


def _fast_dft_tables(n1, n2):
    n = n1 * n2
    k1 = jnp.arange(n1, dtype=jnp.int32)[:, None, None]
    k2 = jnp.arange(n2, dtype=jnp.int32)[None, :, None]
    f = jnp.arange(n2, dtype=jnp.int32)[None, None, :]
    ang = ((f * (k1 + n1 * k2)) % n).astype(F32) * (2.0 * math.pi / n)
    c, s = jnp.cos(ang), jnp.sin(ang)
    top = jnp.concatenate([c, s], axis=2)
    bot = jnp.concatenate([-s, c], axis=2)
    return jnp.concatenate([top, bot], axis=1)


def _left_matmul_kernel(f_ref, x_ref, o_ref, *, precise):
    if precise:
        o = jnp.dot(f_ref[...], x_ref[...], precision=HIGHEST, preferred_element_type=F32)
    else:
        o = jnp.dot(f_ref[...], x_ref[...].astype(BF16), preferred_element_type=F32)
    o_ref[...] = o.astype(o_ref.dtype)


def left_matmul(f, x, out_dtype, precise, ln=4096):
    b, k, lanes = x.shape
    m = f.shape[0]
    ln = min(ln, lanes)
    assert lanes % ln == 0 and f.shape[1] == k
    return pl.pallas_call(
        functools.partial(_left_matmul_kernel, precise=precise),
        grid=(b, lanes // ln),
        in_specs=[pl.BlockSpec((m, k), lambda i, j: (0, 0)),
                  pl.BlockSpec((None, k, ln), lambda i, j: (i, 0, j))],
        out_specs=pl.BlockSpec((None, m, ln), lambda i, j: (i, 0, j)),
        out_shape=jax.ShapeDtypeStruct((b, m, lanes), out_dtype),
        compiler_params=_params("parallel", "parallel"),
        name="left_matmul",
    )(f, x)


def _spectrum_mid_kernel(g_ref, a_ref, sc_ref, o_ref):
    n2 = a_ref.shape[1]
    a = a_ref[...].reshape(2 * n2, a_ref.shape[2])
    x = jnp.dot(g_ref[...], a, precision=HIGHEST, preferred_element_type=F32) * sc_ref[...]
    o_ref[...] = x.reshape(o_ref.shape)


def spectrum_mid(g, a, scale, cb=512):
    _, n1, n2, c = a.shape
    cb = min(cb, c)
    return pl.pallas_call(
        _spectrum_mid_kernel,
        grid=(n1, c // cb),
        in_specs=[pl.BlockSpec((None, 2 * n2, 2 * n2), lambda i, j: (i, 0, 0)),
                  pl.BlockSpec((2, None, n2, cb), lambda i, j: (0, i, 0, j)),
                  pl.BlockSpec((1, cb), lambda i, j: (0, j))],
        out_specs=pl.BlockSpec((2, None, n2, cb), lambda i, j: (0, i, 0, j)),
        out_shape=jax.ShapeDtypeStruct(a.shape, F32),
        compiler_params=_params("parallel", "parallel"),
        name="spectrum_mid",
    )(g, a, scale)


def _conv_mid_kernel(g_ref, gi_ref, a_ref, k_ref, o_ref):
    bblk, _, n2, cb = a_ref.shape
    kr, ki = k_ref[0], k_ref[1]
    for bi in range(bblk):
        a = a_ref[bi].reshape(2 * n2, cb)
        x = jnp.dot(g_ref[...], a, preferred_element_type=F32)
        xr, xi = x[:n2], x[n2:]
        y = jnp.concatenate([xr * kr - xi * ki, xr * ki + xi * kr], axis=0).astype(BF16)
        z = jnp.dot(gi_ref[...], y, preferred_element_type=F32)
        o_ref[bi] = z.reshape(2, n2, cb).astype(o_ref.dtype)


def conv_mid(g, gi, a, kf, block_bytes=2 << 20):
    b, _, n1, n2, c = a.shape
    bblk = max(1, min(b, block_bytes // (2 * n2 * c * 2)))
    assert b % bblk == 0
    return pl.pallas_call(
        _conv_mid_kernel,
        grid=(n1, b // bblk),
        in_specs=[pl.BlockSpec((None, 2 * n2, 2 * n2), lambda i, bb: (i, 0, 0)),
                  pl.BlockSpec((None, 2 * n2, 2 * n2), lambda i, bb: (i, 0, 0)),
                  pl.BlockSpec((bblk, 2, None, n2, c), lambda i, bb: (bb, 0, i, 0, 0)),
                  pl.BlockSpec((2, None, n2, c), lambda i, bb: (0, i, 0, 0))],
        out_specs=pl.BlockSpec((bblk, 2, None, n2, c), lambda i, bb: (bb, 0, i, 0, 0)),
        out_shape=jax.ShapeDtypeStruct(a.shape, BF16),
        compiler_params=_params("parallel", "parallel"),
        name="conv_mid",
    )(g, gi, a, kf)


def _conv_out_kernel(f_ref, z_ref, w_ref, x0_ref, bias_ref, o_ref):
    y = jnp.dot(f_ref[...], z_ref[...], preferred_element_type=F32)
    o_ref[...] = x0_ref[...] * (y + w_ref[...] * bias_ref[...])


def conv_out(f, z, w, x0, bias_l, ln=4096):
    b, k, lanes = z.shape
    m = f.shape[0]
    ln = min(ln, lanes)
    assert lanes % ln == 0 and bias_l.shape == (1, ln)
    return pl.pallas_call(
        _conv_out_kernel,
        grid=(b, lanes // ln),
        in_specs=[pl.BlockSpec((m, k), lambda i, j: (0, 0)),
                  pl.BlockSpec((None, k, ln), lambda i, j: (i, 0, j)),
                  pl.BlockSpec((None, m, ln), lambda i, j: (i, 0, j)),
                  pl.BlockSpec((None, m, ln), lambda i, j: (i, 0, j)),
                  pl.BlockSpec((1, ln), lambda i, j: (0, 0))],
        out_specs=pl.BlockSpec((None, m, ln), lambda i, j: (i, 0, j)),
        out_shape=jax.ShapeDtypeStruct((b, m, lanes), F32),
        compiler_params=_params("parallel", "parallel"),
        name="conv_out",
    )(f, z, w, x0, bias_l)


def _hyena_pre_kernel(x0_ref, x1_ref, v_ref, w0_ref, w1_ref, wv_ref, b0_ref, b1_ref, bv_ref, x0o_ref, wo_ref):
    seq = x0_ref.shape[0]
    row = lax.broadcasted_iota(jnp.int32, x0_ref.shape, 0)
    first, last = row == 0, row == seq - 1

    def conv(u_ref, w_ref, b_ref):
        u = u_ref[...]
        prev = jnp.where(first, 0.0, pltpu.roll(u, 1, axis=0))
        nxt = jnp.where(last, 0.0, pltpu.roll(u, seq - 1, axis=0))
        return b_ref[...] + prev * w_ref[0:1, :] + u * w_ref[1:2, :] + nxt * w_ref[2:3, :]

    x0o_ref[...] = conv(x0_ref, w0_ref, b0_ref)
    wo_ref[...] = conv(v_ref, wv_ref, bv_ref) * conv(x1_ref, w1_ref, b1_ref)


def hyena_pre(hproj, conv_w, conv_b):
    b, seq, _ = hproj.shape
    c = HYENA_WIDTH
    cb = max(LANES, min(c, (1 << 20) // seq))
    nb = c // cb
    u_spec = lambda g: pl.BlockSpec((None, seq, cb), lambda i, j: (i, 0, g * nb + j))
    w_spec = lambda g: pl.BlockSpec((3, cb), lambda i, j: (0, g * nb + j))
    b_spec = lambda g: pl.BlockSpec((1, cb), lambda i, j: (0, g * nb + j))
    o_spec = pl.BlockSpec((None, seq, cb), lambda i, j: (i, 0, j))
    cw = conv_w.astype(F32)
    cbias = conv_b.reshape(1, 3 * c).astype(F32)
    return pl.pallas_call(
        _hyena_pre_kernel,
        grid=(b, nb),
        in_specs=[u_spec(0), u_spec(1), u_spec(2), w_spec(0), w_spec(1), w_spec(2),
                  b_spec(0), b_spec(1), b_spec(2)],
        out_specs=[o_spec, o_spec],
        out_shape=[jax.ShapeDtypeStruct((b, seq, c), F32)] * 2,
        compiler_params=_params("parallel", "parallel"),
        name="hyena_pre",
    )(hproj, hproj, hproj, cw, cw, cw, cbias, cbias, cbias)


def hyena_branch(hproj, conv_w, conv_b, fw1, fb1, fw2, fb2, fw3, fb3, fw4, freq, hyena_bias):
    b, seq, _ = hproj.shape
    c = HYENA_WIDTH
    n = 2 * seq
    n1, n2 = _fft_split(n)
    n1h = n1 // 2
    cos1, sin1 = _slow_dft_tables(n1)
    g = _fast_dft_tables(n1, n2)
    gi = jnp.swapaxes(g, 1, 2)

    h, l1 = filter_mlp(seq, fw1, fb1, fw2, fb2, fw3, fb3, fw4, freq)
    h = h.reshape(seq, 2, c)
    k_full = jnp.concatenate([h[:, 0], jnp.zeros((1, c), F32), h[1:, 1][::-1]], axis=0)
    l1c = l1[:, :c] + l1[:, c:]
    scale = 1.0 / ((l1c + EPS) * n)
    fa_full = jnp.concatenate([cos1, -sin1], axis=0)
    ka = left_matmul(fa_full, k_full.reshape(1, n1, n2 * c), F32, precise=True)
    kf = spectrum_mid(g, ka.reshape(2, n1, n2, c), scale)

    x0c, w = hyena_pre(hproj, conv_w, conv_b)
    fa = fa_full[:, :n1h].astype(BF16)
    a = left_matmul(fa, w.reshape(b, n1h, n2 * c), BF16, precise=False)
    z = conv_mid(g.astype(BF16), gi.astype(BF16), a.reshape(b, 2, n1, n2, c), kf)
    fc = jnp.concatenate([cos1[:n1h], -sin1[:n1h]], axis=1).astype(BF16)
    ln = min(4096, n2 * c)
    bias_l = jnp.tile(hyena_bias.reshape(1, c).astype(F32), (1, ln // c))
    y = conv_out(fc, z.reshape(b, 2 * n1, n2 * c), w.reshape(b, n1h, n2 * c), x0c.reshape(b, n1h, n2 * c),
                 bias_l, ln=ln)
    return y.reshape(b, seq, c)


ATT_SUB = 128


def _dilated_attn_kernel(q_ref, kp_ref, kc_ref, kn_ref, vp_ref, vc_ref, vn_ref, o_ref, l_ref, *,
                         n_cls, dilation, slopes):
    tq = q_ref.shape[0]
    q0 = pl.program_id(1) * tq
    res = pl.program_id(2)
    sk = ATT_SUB + 2 * ATT_HALF
    scale = 1.0 / math.sqrt(HEAD_DIM)
    qi = lax.broadcasted_iota(jnp.int32, (ATT_SUB, sk), 0)
    kj = lax.broadcasted_iota(jnp.int32, (ATT_SUB, sk), 1) - ATT_HALF
    dist = jnp.abs(kj - qi)
    band = dist <= ATT_HALF
    adist = (dilation * dist).astype(F32)
    lane = lax.broadcasted_iota(jnp.int32, (sk, LANES), 1)
    low_k = lane < HEAD_DIM
    low_q = lax.broadcasted_iota(jnp.int32, (ATT_SUB, LANES), 1) < HEAD_DIM
    for sub in range(tq // ATT_SUB):
        r0 = sub * ATT_SUB
        kabs = q0 + r0 + kj
        mask = band & (kabs >= 0) & (kabs < n_cls)
        for pair in range(HEADS_PER_GROUP // 2):
            cs = slice(pair * LANES, (pair + 1) * LANES)
            q = q_ref[r0:r0 + ATT_SUB, cs]

            def keys(p_ref, c_ref, n_ref):
                parts = []
                if r0 == 0:
                    parts.append(p_ref[:, cs])
                    parts.append(c_ref[0:min(tq, ATT_SUB + ATT_HALF), cs])
                else:
                    parts.append(c_ref[r0 - ATT_HALF:min(tq, r0 + ATT_SUB + ATT_HALF), cs])
                if r0 + ATT_SUB + ATT_HALF > tq:
                    parts.append(n_ref[:, cs])
                return jnp.concatenate(parts, axis=0)

            k = keys(kp_ref, kc_ref, kn_ref)
            v = keys(vp_ref, vc_ref, vn_ref)
            acc = None
            ls, lses = [], []
            for e in range(2):
                sel = low_k if e == 0 else ~low_k
                ke = jnp.where(sel, k, jnp.zeros_like(k))
                ve = jnp.where(sel, v, jnp.zeros_like(v))
                s = lax.dot_general(q, ke, (((1,), (1,)), ((), ())), preferred_element_type=F32) * scale
                s = jnp.where(mask, s - slopes[2 * pair + e] * adist, NEG_INF)
                m = jnp.max(s, axis=-1, keepdims=True)
                p = jnp.exp(s - m)
                l = jnp.sum(p, axis=-1, keepdims=True)
                pv = jnp.dot(p.astype(BF16), ve, preferred_element_type=F32)
                acc = pv if acc is None else acc + pv
                ls.append(l)
                lses.append(m + jnp.log(l))
            inv = 1.0 / jnp.where(low_q, ls[0], ls[1])
            if dilation == 1:
                tok = slice(r0, r0 + ATT_SUB)
            else:
                tok = pl.ds(r0 * dilation + res, ATT_SUB, stride=dilation)
            o_ref[pair, tok, :] = acc * inv
            l_ref[pair, tok, :] = jnp.where(low_q, lses[0], lses[1])


def dilated_attention(qkv, batch, group, out_block_bytes=4 << 20):
    window, dilation = ATT_GROUPS[group]
    assert window // (2 * dilation) == ATT_HALF
    n_cls = qkv.shape[0] // batch
    seq = n_cls * dilation
    tq = min(256, n_cls, max(ATT_SUB, out_block_bytes // (dilation * ATT_WIDTH * 4)))
    assert n_cls % tq == 0 and tq % ATT_SUB == 0
    hb = tq // ATT_HALF
    n_halo = n_cls // ATT_HALF
    slopes = tuple(2.0 ** (-8.0 * (group * HEADS_PER_GROUP + h + 1.0) / N_ATT_HEADS)
                   for h in range(HEADS_PER_GROUP))
    x = qkv.reshape(batch, n_cls, dilation * 3 * ATT_WIDTH)
    cur = lambda which: pl.BlockSpec((None, tq, ATT_WIDTH), lambda i, j, r: (i, j, 3 * r + which))
    prev = lambda which: pl.BlockSpec((None, ATT_HALF, ATT_WIDTH),
                                      lambda i, j, r: (i, jnp.maximum(j * hb - 1, 0), 3 * r + which))
    nxt = lambda which: pl.BlockSpec((None, ATT_HALF, ATT_WIDTH),
                                     lambda i, j, r: (i, jnp.minimum((j + 1) * hb, n_halo - 1), 3 * r + which))
    npair = ATT_WIDTH // LANES
    o_spec = pl.BlockSpec((None, npair, tq * dilation, LANES), lambda i, j, r: (i, 0, j, 0))
    return pl.pallas_call(
        functools.partial(_dilated_attn_kernel, n_cls=n_cls, dilation=dilation, slopes=slopes),
        grid=(batch, n_cls // tq, dilation),
        in_specs=[cur(0), prev(1), cur(1), nxt(1), prev(2), cur(2), nxt(2)],
        out_specs=[o_spec, o_spec],
        out_shape=[jax.ShapeDtypeStruct((batch, npair, seq, LANES), F32)] * 2,
        compiler_params=_params("parallel", "parallel", "arbitrary"),
        name="dilated_attn",
    )(x, x, x, x, x, x, x)


def _merge_kernel(x_ref, ga_ref, gb_ref, ya_ref, o0_ref, o1_ref, o2_ref, l0_ref, l1_ref, l2_ref,
                  wa_ref, wb_ref, wo_ref, g2_ref, x1_ref, xn_ref):
    parts = []
    for pair in range(o0_ref.shape[0]):
        l0, l1, l2 = l0_ref[pair], l1_ref[pair], l2_ref[pair]
        m = jnp.maximum(jnp.maximum(l0, l1), l2)
        e0, e1, e2 = jnp.exp(l0 - m), jnp.exp(l1 - m), jnp.exp(l2 - m)
        yb = (e0 * o0_ref[pair] + e1 * o1_ref[pair] + e2 * o2_ref[pair]) / (e0 + e1 + e2)
        parts.append(yb.astype(BF16))
    pa = jnp.dot(ya_ref[...].astype(BF16), wa_ref[...], preferred_element_type=F32)
    pb = jnp.dot(jnp.concatenate(parts, axis=1), wb_ref[...], preferred_element_type=F32)
    merged = jax.nn.sigmoid(ga_ref[...]) * pa + jax.nn.sigmoid(gb_ref[...]) * pb
    x1 = x_ref[...] + jnp.dot(merged.astype(BF16), wo_ref[...], preferred_element_type=F32)
    x1_ref[...] = x1
    ms = jnp.mean(x1 * x1, axis=-1, keepdims=True)
    xn_ref[...] = (x1 * lax.rsqrt(ms + EPS) * g2_ref[...]).astype(BF16)


def merge(x, glog, ya, outs, lses, wa, wb, wo, g2, tm=256):
    t, d = x.shape
    _, npair, seq, _ = outs[0].shape
    tm = min(tm, seq)
    assert seq % tm == 0
    tiles = seq // tm
    row = lambda c, blk=0: pl.BlockSpec((tm, c), lambda i: (i, blk))
    const = lambda r, c: pl.BlockSpec((r, c), lambda i: (0, 0), pipeline_mode=pl.Buffered(1))
    att = pl.BlockSpec((None, npair, tm, LANES), lambda i: (i // tiles, 0, i % tiles, 0))
    aw = ATT_WIDTH
    return pl.pallas_call(
        _merge_kernel,
        grid=(t // tm,),
        in_specs=[row(d), row(d, 0), row(d, 1), row(HYENA_WIDTH), att, att, att, att, att, att,
                  const(HYENA_WIDTH, d), const(aw, d), const(d, d), const(1, d)],
        out_specs=[row(d), row(d)],
        out_shape=[jax.ShapeDtypeStruct((t, d), F32), jax.ShapeDtypeStruct((t, d), BF16)],
        compiler_params=_params("parallel"),
        name="merge",
    )(x, glog, glog, ya, *outs, *lses, wa, wb, wo, g2.reshape(1, d).astype(F32))


def _descending_max(curs, k, emit):
    curs = list(curs)
    for it in range(k):
        for c, cur in enumerate(curs):
            m = jnp.max(cur, axis=0, keepdims=True)
            emit(c, it, m)
            if it + 1 < k:
                curs[c] = jnp.where(cur == m, -jnp.inf, cur)


PEER_LOCKSTEP_HEADS = 2


def _peer_scores_kernel(xn_ref, wq_ref, sk_ref, g1_ref, g2_ref, top_ref, s_ref):
    k = PEER_TOPK
    sub = 8
    nh = PEER_HEADS * N_KEYS
    q = jnp.dot(xn_ref[...], wq_ref[...], preferred_element_type=F32).astype(BF16)
    low_rank = lax.broadcasted_iota(jnp.int32, (sub, LANES), 0) < 4
    for tc in range(xn_ref.shape[0] // LANES):
        cols = slice(tc * LANES, (tc + 1) * LANES)
        for h0 in range(0, PEER_HEADS, PEER_LOCKSTEP_HEADS):
            heads = range(h0, h0 + PEER_LOCKSTEP_HEADS)
            chains = [(hh, p) for hh in range(PEER_LOCKSTEP_HEADS) for p in range(2)]
            for c, (hh, p) in enumerate(chains):
                hp = 2 * (h0 + hh) + p
                qs = q[cols, hp * N_KEYS:(hp + 1) * N_KEYS]
                s_ref[c] = lax.dot_general(sk_ref[hp], qs, (((1,), (1,)), ((), ())), preferred_element_type=F32)

            def put_top(c, it, m):
                top_ref[c, it:it + 1, :] = m

            _descending_max([s_ref[c] for c in range(len(chains))], k, put_top)

            cands = []
            for hh in range(PEER_LOCKSTEP_HEADS):
                v1, v2 = top_ref.at[2 * hh], top_ref.at[2 * hh + 1]
                tiles = [v1[r:r + 1, :] + v2[0:sub, :] for r in range(4)]
                tiles += [v1[0:sub, :] + v2[r:r + 1, :] for r in range(4)]
                tiles += [v1[sub:k, :] + v2[0:1, :], v1[0:1, :] + v2[sub:k, :]]
                cands.append(jnp.concatenate(tiles, axis=0))
            repeated = [None] * 4 + [low_rank] * 4 + [None] * 2
            taus = [None] * PEER_LOCKSTEP_HEADS

            def put_tau(c, it, m):
                taus[c] = m

            _descending_max(cands, k, put_tau)
            for hh, h in enumerate(heads):
                rows = slice(h * N_KEYS, (h + 1) * N_KEYS)
                rows_e = slice(nh + h * N_KEYS, nh + (h + 1) * N_KEYS)
                v1, v2 = top_ref.at[2 * hh], top_ref.at[2 * hh + 1]
                tau = taus[hh]
                best = v1[0:1, :] + v2[0:1, :]
                zsum = None
                for ti, dup in enumerate(repeated):
                    tile = cands[hh][ti * sub:(ti + 1) * sub, :]
                    keep = tile >= tau
                    if dup is not None:
                        keep = keep & ~dup
                    part = jnp.sum(jnp.where(keep, jnp.exp(tile - best), 0.0), axis=0, keepdims=True)
                    zsum = part if zsum is None else zsum + part
                inv_z = 1.0 / zsum
                v2_all = v2[...]
                e2_top = jnp.exp(v2_all - v2[0:1, :]) * inv_z
                s1 = s_ref[2 * hh]
                theta = jnp.full(s1.shape, jnp.inf, F32)
                for r in range(k):
                    c = v1[r:r + 1, :] + v2_all
                    theta_r = jnp.min(jnp.where(c >= tau, e2_top, jnp.inf), axis=0, keepdims=True)
                    theta = jnp.where(s1 == v1[r:r + 1, :], theta_r, theta)
                g1_ref[rows, cols] = theta
                g1_ref[rows_e, cols] = jnp.exp(s1 - v1[0:1, :])
                g2_ref[tc, rows, :] = jnp.exp(s_ref[2 * hh + 1] - v2[0:1, :]) * inv_z


def peer_scores(xn, wq, subkeys, tm=256):
    t, d = xn.shape
    tm = min(tm, t)
    assert t % tm == 0 and tm % LANES == 0
    rows = 2 * PEER_HEADS * N_KEYS
    return pl.pallas_call(
        _peer_scores_kernel,
        grid=(t // tm,),
        in_specs=[pl.BlockSpec((tm, d), lambda i: (i, 0)),
                  pl.BlockSpec((d, rows), lambda i: (0, 0), pipeline_mode=pl.Buffered(1)),
                  pl.BlockSpec((2 * PEER_HEADS, N_KEYS, N_KEYS), lambda i: (0, 0, 0))],
        out_specs=[pl.BlockSpec((rows, tm), lambda i: (0, i)),
                   pl.BlockSpec((tm // LANES, rows // 2, LANES), lambda i: (i, 0, 0))],
        out_shape=[jax.ShapeDtypeStruct((rows, t), F32),
                   jax.ShapeDtypeStruct((t // LANES, rows // 2, LANES), F32)],
        scratch_shapes=[pltpu.VMEM((2 * PEER_LOCKSTEP_HEADS, PEER_TOPK, LANES), F32),
                        pltpu.VMEM((2 * PEER_LOCKSTEP_HEADS, N_KEYS, LANES), F32)],
        compiler_params=_params("parallel"),
        name="peer_scores",
    )(xn, wq, subkeys)


GELU_C0 = math.sqrt(2.0 / math.pi)
GELU_C1 = GELU_C0 * 0.044715


def _gelu_tanh(x):
    return x * (0.5 + 0.5 * jnp.tanh(x * (GELU_C0 + GELU_C1 * (x * x))))


def _peer_experts_kernel(xn_ref, g1_ref, g2_ref, u_ref, vt_ref, x1_ref, g_ref, y_ref, acc_ref, a_ref, wa_ref):
    j = pl.program_id(1)
    nblk = pl.num_programs(1) - 2
    _, nchunk, eb, _ = a_ref.shape
    tm = nchunk * LANES
    a_even, a_odd = a_ref.at[0], a_ref.at[1]
    wa_even, wa_odd = wa_ref.at[0], wa_ref.at[1]
    nh = PEER_HEADS * N_KEYS
    per = eb // N_KEYS
    half = tm // 2

    mrows = 256
    arows = 256

    def activations(a_ref, m, c):
        rows = slice(m * arows, (m + 1) * arows)
        cols = slice(c * half, (c + 1) * half)
        a = lax.dot_general(u_ref[rows, :], xn_ref[cols, :], (((1,), (1,)), ((), ())),
                            preferred_element_type=F32)
        for k in range(half // LANES):
            a_ref[c * (half // LANES) + k, rows, :] = a[:, k * LANES:(k + 1) * LANES]

    def gate_rows(block, ii):
        key1 = block * per + ii
        return ([g1_ref[pl.ds(h * N_KEYS + key1, 1), :] for h in range(PEER_HEADS)],
                [g1_ref[pl.ds(nh + h * N_KEYS + key1, 1), :] for h in range(PEER_HEADS)])

    def gates(a_ref, wa_ref, theta_rows, e1_rows, ii, tc):
        rows = slice(ii * N_KEYS, (ii + 1) * N_KEYS)
        cols = slice(tc * LANES, (tc + 1) * LANES)
        wsum = None
        for h in range(PEER_HEADS):
            e2 = g2_ref[tc, h * N_KEYS:(h + 1) * N_KEYS, :]
            term = e1_rows[h][:, cols] * jnp.where(e2 >= theta_rows[h][:, cols], e2, 0.0)
            wsum = term if wsum is None else wsum + term
        wa_ref[rows, cols] = (_gelu_tanh(a_ref[tc, rows, :]) * wsum).astype(BF16)

    def outputs(wa_ref, r, c):
        rows = slice(r * mrows, (r + 1) * mrows)
        cols = slice(c * half, (c + 1) * half)
        acc_ref[rows, cols] += jnp.dot(vt_ref[rows, :], wa_ref[:, cols], preferred_element_type=F32)

    def step(act, gate, out):
        vec, mxu_a, mxu_o = [], [], []
        rows_cache = {}

        def gate_piece(ii, tc):
            if ii not in rows_cache:
                rows_cache[ii] = gate_rows(j - 1, ii)
            gates(*gate, *rows_cache[ii], ii, tc)

        if gate:
            vec = [functools.partial(gate_piece, ii, tc) for ii in range(per) for tc in range(nchunk)]
        if act is not None:
            mxu_a = [functools.partial(activations, act, m, c) for m in range(eb // arows) for c in range(2)]
        if out is not None:
            mxu_o = [functools.partial(outputs, out, r, c) for c in range(2) for r in range(acc_ref.shape[0] // mrows)]
        n = max(len(vec), len(mxu_a), len(mxu_o))
        for k in range(n):
            for stage in (vec, mxu_a, mxu_o):
                for piece in stage[k * len(stage) // n:(k + 1) * len(stage) // n]:
                    piece()

    steady = (j > 1) & (j < nblk)

    @pl.when(j == 0)
    def _():
        acc_ref[...] = jnp.zeros_like(acc_ref)
        step(a_even, None, None)

    @pl.when(j == 1)
    def _():
        step(a_odd, (a_even, wa_even), None)

    @pl.when(steady & (j % 2 == 0))
    def _():
        step(a_even, (a_odd, wa_odd), wa_even)

    @pl.when(steady & (j % 2 == 1))
    def _():
        step(a_odd, (a_even, wa_even), wa_odd)

    @pl.when(j == nblk)
    def _():
        step(None, (a_odd, wa_odd), wa_even)

    @pl.when(j == nblk + 1)
    def _():
        step(None, None, wa_odd)
        x2 = x1_ref[...] + acc_ref[...].T
        ms = jnp.mean(x2 * x2, axis=-1, keepdims=True)
        y_ref[...] = x2 * lax.rsqrt(ms + EPS) * g_ref[...]


def peer_experts(xn, g1, g2, u, vt, x1, final_g, tm=512):
    t, d = xn.shape
    eb = vt.shape[2]
    tm = min(tm, t)
    assert t % tm == 0 and N_EXPERTS % (2 * eb) == 0 and eb % N_KEYS == 0 and tm % (2 * LANES) == 0
    rows = 2 * PEER_HEADS * N_KEYS
    nblk = N_EXPERTS // eb
    return pl.pallas_call(
        _peer_experts_kernel,
        grid=(t // tm, nblk + 2),
        in_specs=[pl.BlockSpec((tm, d), lambda i, j: (i, 0)),
                  pl.BlockSpec((rows, tm), lambda i, j: (0, i)),
                  pl.BlockSpec((tm // LANES, rows // 2, LANES), lambda i, j: (i, 0, 0)),
                  pl.BlockSpec((eb, d), lambda i, j: (jnp.minimum(j, nblk - 1), 0)),
                  pl.BlockSpec((None, d, eb), lambda i, j: (jnp.clip(j - 2, 0, nblk - 1), 0, 0)),
                  pl.BlockSpec((tm, d), lambda i, j: (i, 0), pipeline_mode=pl.Buffered(1)),
                  pl.BlockSpec((1, d), lambda i, j: (0, 0))],
        out_specs=pl.BlockSpec((tm, d), lambda i, j: (i, 0)),
        out_shape=jax.ShapeDtypeStruct((t, d), F32),
        scratch_shapes=[pltpu.VMEM((d, tm), F32), pltpu.VMEM((2, tm // LANES, eb, LANES), F32),
                        pltpu.VMEM((2, eb, tm), BF16)],
        compiler_params=_params("parallel", "arbitrary"),
        name="peer_experts",
    )(xn, g1, g2, u, vt, x1, final_g.reshape(1, d).astype(F32))


def _encoder(x, p):
    b, seq, d = x.shape
    t = b * seq
    xt = x.reshape(t, d)
    hproj = norm_matmul(xt, p["norm1_g"], p["w_in"], 0, HYENA_COLS, F32)
    glog = norm_matmul(xt, p["norm1_g"], p["w_in"], HYENA_COLS + ATT_COLS, GATE_COLS, F32)
    ya = hyena_branch(hproj.reshape(b, seq, HYENA_COLS), p["conv_w"], p["conv_b"], p["filt_w1"], p["filt_b1"],
                      p["filt_w2"], p["filt_b2"], p["filt_w3"], p["filt_b3"], p["filt_w4"], p["filt_freq"],
                      p["hyena_bias"])
    att = [dilated_attention(norm_matmul_classes(xt, p["norm1_g"], p["w_in"], g), b, g) for g in range(N_GROUPS)]
    x1, xn2 = merge(xt, glog, ya.reshape(t, HYENA_WIDTH), [o for o, _ in att], [l for _, l in att],
                    p["w_branch_a"], p["w_branch_b"], p["w_out"], p["norm2_g"])
    g1, g2 = peer_scores(xn2, p["peer_wq"], p["peer_subkeys"])
    y = peer_experts(xn2, g1, g2, p["peer_u"], p["peer_vt"], x1, p["final_g"])
    return y.reshape(b, seq, d)


def kernel(x_prompt, x_sample, norm1_g, w_in, conv_w, conv_b, filt_w1, filt_b1, filt_w2, filt_b2, filt_w3,
           filt_b3, filt_w4, filt_freq, hyena_bias, w_branch_a, w_branch_b, w_out, norm2_g, peer_wq,
           peer_subkeys, peer_u, peer_v, final_g):
    assert norm1_g.shape[0] == 1, "single-layer encoder"
    p = dict(
        norm1_g=norm1_g[0].astype(F32), w_in=w_in[0].astype(BF16), conv_w=conv_w[0], conv_b=conv_b[0],
        filt_w1=filt_w1[0], filt_b1=filt_b1[0], filt_w2=filt_w2[0], filt_b2=filt_b2[0], filt_w3=filt_w3[0],
        filt_b3=filt_b3[0], filt_w4=filt_w4[0], filt_freq=filt_freq[0], hyena_bias=hyena_bias[0],
        w_branch_a=w_branch_a[0].astype(BF16), w_branch_b=w_branch_b[0].astype(BF16),
        w_out=w_out[0].astype(BF16), norm2_g=norm2_g[0], peer_wq=peer_wq[0].astype(BF16),
        peer_subkeys=peer_subkeys[0].reshape(2 * PEER_HEADS, N_KEYS, N_KEYS).astype(BF16),
        peer_u=peer_u[0].astype(BF16),
        peer_vt=peer_v[0].astype(BF16).reshape(N_EXPERTS // PEER_EB, PEER_EB, D_MODEL).transpose(0, 2, 1),
        final_g=final_g,
    )
    return (_encoder(x_prompt, p), _encoder(x_sample, p))
```

```python
import functools
import math

import jax
import jax.numpy as jnp
from jax import lax
from jax.experimental import pallas as pl
from jax.experimental.pallas import tpu as pltpu

F32 = jnp.float32
BF16 = jnp.bfloat16

D_MODEL = 2048
HYENA_WIDTH = 1024
FILTER_EMB = 33
FILTER_EMB_PAD = 128
FILTER_HIDDEN = 64
DECAY_TARGET = 1e-2
FAST_DECAY_PCT = 0.3
SLOW_DECAY_PCT = 1.5
ATT_GROUPS = ((128, 1), (512, 4), (2048, 16))
N_GROUPS = 3
HEADS_PER_GROUP = 8
HEAD_DIM = 64
N_ATT_HEADS = N_GROUPS * HEADS_PER_GROUP
ATT_WIDTH = HEADS_PER_GROUP * HEAD_DIM
HYENA_COLS = 3 * HYENA_WIDTH
ATT_COLS = 3 * N_GROUPS * ATT_WIDTH
GATE_COLS = 2 * D_MODEL
PEER_HEADS = 8
N_KEYS = 128
N_EXPERTS = N_KEYS * N_KEYS
PEER_TOPK = 16
PEER_EB = 512
EPS = 1e-6
NEG_INF = -1e30
ATT_HALF = 64
LANES = 128
VMEM_LIMIT = 56 * 1024 * 1024
HIGHEST = lax.Precision.HIGHEST


def _params(*sem, flags=None):
    return pltpu.CompilerParams(dimension_semantics=sem, vmem_limit_bytes=VMEM_LIMIT, flags=flags)


def _rms_norm_kernel(x_ref, g_ref, o_ref):
    x = x_ref[...]
    ms = jnp.mean(x * x, axis=-1, keepdims=True)
    o_ref[...] = (x * lax.rsqrt(ms + EPS) * g_ref[...]).astype(o_ref.dtype)


def rms_norm_bf16(x, g, tm=512):
    t, d = x.shape
    tm = min(tm, t)
    assert t % tm == 0
    return pl.pallas_call(
        _rms_norm_kernel,
        grid=(t // tm,),
        in_specs=[pl.BlockSpec((tm, d), lambda i: (i, 0)), pl.BlockSpec((1, d), lambda i: (0, 0))],
        out_specs=pl.BlockSpec((tm, d), lambda i: (i, 0)),
        out_shape=jax.ShapeDtypeStruct((t, d), BF16),
        compiler_params=_params("parallel"),
        name="rms_norm",
    )(x, g.reshape(1, d))


def _project_kernel(x_ref, w_ref, o_ref):
    o_ref[...] = jnp.dot(x_ref[...], w_ref[...], preferred_element_type=F32).astype(o_ref.dtype)


def project(xn, w, col0, ncols, out_dtype, tm=2048, tn=512):
    t, d = xn.shape
    tm = min(tm, t)
    assert t % tm == 0 and ncols % tn == 0 and col0 % tn == 0
    cb0 = col0 // tn
    return pl.pallas_call(
        _project_kernel,
        grid=(t // tm, ncols // tn),
        in_specs=[
            pl.BlockSpec((tm, d), lambda i, j: (i, 0)),
            pl.BlockSpec((d, tn), lambda i, j: (0, cb0 + j)),
        ],
        out_specs=pl.BlockSpec((tm, tn), lambda i, j: (i, j)),
        out_shape=jax.ShapeDtypeStruct((t, ncols), out_dtype),
        compiler_params=_params("parallel", "arbitrary"),
        name="project",
    )(xn, w)


def _project_classes_kernel(x_ref, w_ref, o_ref, acc_ref, *, dilation):
    which = pl.program_id(1)
    acc = jnp.dot(x_ref[...], w_ref[...], preferred_element_type=F32)
    nlane, tm, _ = acc_ref.shape
    for k in range(nlane):
        acc_ref[k] = acc[:, k * LANES:(k + 1) * LANES]
    rows = tm // dilation
    for part in range(3):
        @pl.when(which == part)
        def _(part=part):
            for r in range(dilation):
                for k in range(nlane):
                    piece = acc_ref[k] if dilation == 1 else acc_ref[k, pl.ds(r, rows, stride=dilation), :]
                    c0 = (3 * r + part) * nlane * LANES + k * LANES
                    o_ref[:, c0:c0 + LANES] = piece.astype(o_ref.dtype)


def project_classes(xn, w, group, tm=2048):
    t, d = xn.shape
    dilation = ATT_GROUPS[group][1]
    tm = min(tm, t)
    assert t % tm == 0 and tm % (8 * dilation) == 0
    cb0 = HYENA_COLS // ATT_WIDTH + group
    return pl.pallas_call(
        functools.partial(_project_classes_kernel, dilation=dilation),
        grid=(t // tm, 3),
        in_specs=[
            pl.BlockSpec((tm, d), lambda i, j: (i, 0)),
            pl.BlockSpec((d, ATT_WIDTH), lambda i, j: (0, cb0 + N_GROUPS * j)),
        ],
        out_specs=pl.BlockSpec((tm // dilation, dilation * 3 * ATT_WIDTH), lambda i, j: (i, 0)),
        out_shape=jax.ShapeDtypeStruct((t // dilation, dilation * 3 * ATT_WIDTH), BF16),
        scratch_shapes=[pltpu.VMEM((ATT_WIDTH // LANES, tm, LANES), F32)],
        compiler_params=_params("parallel", "arbitrary"),
        name="project_classes",
    )(xn, w)


def _filter_mlp_kernel(z_ref, t_ref, keep_ref, w1_ref, b1_ref, w2_ref, b2_ref, w3_ref, b3_ref, w4_ref, fr_ref,
                       dl_ref, h_ref, l1_ref):
    fr = fr_ref[...]
    h = jnp.sin(fr * (jnp.dot(z_ref[...], w1_ref[...], precision=HIGHEST, preferred_element_type=F32)
                      + b1_ref[...]))
    h = jnp.sin(fr * (jnp.dot(h, w2_ref[...], precision=HIGHEST, preferred_element_type=F32) + b2_ref[...]))
    h = jnp.sin(fr * (jnp.dot(h, w3_ref[...], precision=HIGHEST, preferred_element_type=F32) + b3_ref[...]))
    h = jnp.dot(h, w4_ref[...], precision=HIGHEST, preferred_element_type=F32)
    h = h * jnp.exp(-t_ref[...] * dl_ref[...])
    h_ref[...] = h * keep_ref[...]

    @pl.when(pl.program_id(0) == 0)
    def _():
        l1_ref[...] = jnp.zeros_like(l1_ref)

    l1_ref[...] += jnp.sum(jnp.abs(h), axis=0, keepdims=True)


def filter_mlp(seq, w1, b1, w2, b2, w3, b3, w4, freq, tl=512):
    t = jnp.linspace(0.0, 1.0, seq, dtype=F32)[:, None]
    bands = (FILTER_EMB - 1) // 2
    ang = 2.0 * math.pi * jnp.arange(seq, dtype=F32)[:, None] / seq
    fb = jnp.linspace(1e-4, bands - 1, bands, dtype=F32)[None, :]
    z = jnp.concatenate([t, jnp.cos(fb * ang), -jnp.sin(fb * ang)], axis=-1)
    z = jnp.pad(z, ((0, 0), (0, FILTER_EMB_PAD - FILTER_EMB)))
    pos = jnp.concatenate([jnp.arange(seq), jnp.zeros((1,), jnp.int32), jnp.arange(seq - 1, 0, -1)])
    keep = (jnp.arange(2 * seq) != seq).astype(F32)[:, None]
    w1p = jnp.pad(w1.astype(F32), ((0, FILTER_EMB_PAD - FILTER_EMB), (0, 0)))
    min_decay = math.log(DECAY_TARGET) / SLOW_DECAY_PCT
    max_decay = math.log(DECAY_TARGET) / FAST_DECAY_PCT
    deltas = jnp.abs(jnp.linspace(min_decay, max_decay, HYENA_WIDTH, dtype=F32))[None, :]
    tl = min(tl, seq)
    assert seq % tl == 0
    c = HYENA_WIDTH
    hid = FILTER_HIDDEN
    per_dir = seq // tl
    full = lambda r, cc: pl.BlockSpec((r, cc), lambda i: (0, 0))
    return pl.pallas_call(
        _filter_mlp_kernel,
        grid=(2 * per_dir,),
        in_specs=[
            pl.BlockSpec((tl, FILTER_EMB_PAD), lambda i: (i, 0)),
            pl.BlockSpec((tl, 1), lambda i: (i, 0)),
            pl.BlockSpec((tl, 1), lambda i: (i, 0)),
            full(FILTER_EMB_PAD, hid), full(1, hid), full(hid, hid), full(1, hid), full(hid, hid),
            full(1, hid), pl.BlockSpec((hid, c), lambda i: (0, i // per_dir)), full(1, hid), full(1, c),
        ],
        out_specs=[pl.BlockSpec((tl, c), lambda i: (i, 0)), pl.BlockSpec((1, c), lambda i: (0, 0))],
        out_shape=[jax.ShapeDtypeStruct((2 * seq, c), F32), jax.ShapeDtypeStruct((1, c), F32)],
        compiler_params=_params("arbitrary"),
        name="filter_mlp",
    )(z[pos], t[pos], keep, w1p, b1.reshape(1, hid).astype(F32), w2.astype(F32), b2.reshape(1, hid).astype(F32),
      w3.astype(F32), b3.reshape(1, hid).astype(F32), w4.astype(F32), freq.reshape(1, hid).astype(F32), deltas)


def _fft_split(n):
    lg = int(math.log2(n))
    assert 1 << lg == n
    n1 = 1 << ((lg + 1) // 2)
    return n1, n // n1


def _slow_dft_tables(n1):
    a = jnp.arange(n1, dtype=jnp.int32)
    ang = ((a[:, None] * a[None, :]) % n1).astype(F32) * (2.0 * math.pi / n1)
    return jnp.cos(ang), jnp.sin(ang)


def _fast_dft_tables(n1, n2):
    n = n1 * n2
    k1 = jnp.arange(n1, dtype=jnp.int32)[:, None, None]
    k2 = jnp.arange(n2, dtype=jnp.int32)[None, :, None]
    f = jnp.arange(n2, dtype=jnp.int32)[None, None, :]
    ang = ((f * (k1 + n1 * k2)) % n).astype(F32) * (2.0 * math.pi / n)
    c, s = jnp.cos(ang), jnp.sin(ang)
    top = jnp.concatenate([c, s], axis=2)
    bot = jnp.concatenate([-s, c], axis=2)
    return jnp.concatenate([top, bot], axis=1)


def _left_matmul_kernel(f_ref, x_ref, o_ref, *, precise):
    if precise:
        o = jnp.dot(f_ref[...], x_ref[...], precision=HIGHEST, preferred_element_type=F32)
    else:
        o = jnp.dot(f_ref[...], x_ref[...].astype(BF16), preferred_element_type=F32)
    o_ref[...] = o.astype(o_ref.dtype)


def left_matmul(f, x, out_dtype, precise, ln=4096):
    b, k, lanes = x.shape
    m = f.shape[0]
    ln = min(ln, lanes)
    assert lanes % ln == 0 and f.shape[1] == k
    return pl.pallas_call(
        functools.partial(_left_matmul_kernel, precise=precise),
        grid=(b, lanes // ln),
        in_specs=[pl.BlockSpec((m, k), lambda i, j: (0, 0)),
                  pl.BlockSpec((None, k, ln), lambda i, j: (i, 0, j))],
        out_specs=pl.BlockSpec((None, m, ln), lambda i, j: (i, 0, j)),
        out_shape=jax.ShapeDtypeStruct((b, m, lanes), out_dtype),
        compiler_params=_params("parallel", "parallel"),
        name="left_matmul",
    )(f, x)


def _spectrum_mid_kernel(g_ref, a_ref, sc_ref, o_ref):
    n2 = a_ref.shape[1]
    a = a_ref[...].reshape(2 * n2, a_ref.shape[2])
    x = jnp.dot(g_ref[...], a, precision=HIGHEST, preferred_element_type=F32) * sc_ref[...]
    o_ref[...] = x.reshape(o_ref.shape)


def spectrum_mid(g, a, scale, cb=512):
    _, n1, n2, c = a.shape
    cb = min(cb, c)
    return pl.pallas_call(
        _spectrum_mid_kernel,
        grid=(n1, c // cb),
        in_specs=[pl.BlockSpec((None, 2 * n2, 2 * n2), lambda i, j: (i, 0, 0)),
                  pl.BlockSpec((2, None, n2, cb), lambda i, j: (0, i, 0, j)),
                  pl.BlockSpec((1, cb), lambda i, j: (0, j))],
        out_specs=pl.BlockSpec((2, None, n2, cb), lambda i, j: (0, i, 0, j)),
        out_shape=jax.ShapeDtypeStruct(a.shape, F32),
        compiler_params=_params("parallel", "parallel"),
        name="spectrum_mid",
    )(g, a, scale)


def _conv_mid_kernel(g_ref, gi_ref, a_ref, k_ref, o_ref):
    bblk, _, n2, cb = a_ref.shape
    kr, ki = k_ref[0], k_ref[1]
    for bi in range(bblk):
        a = a_ref[bi].reshape(2 * n2, cb)
        x = jnp.dot(g_ref[...], a, preferred_element_type=F32)
        xr, xi = x[:n2], x[n2:]
        y = jnp.concatenate([xr * kr - xi * ki, xr * ki + xi * kr], axis=0).astype(BF16)
        z = jnp.dot(gi_ref[...], y, preferred_element_type=F32)
        o_ref[bi] = z.reshape(2, n2, cb).astype(o_ref.dtype)


def conv_mid(g, gi, a, kf, block_bytes=2 << 20):
    b, _, n1, n2, c = a.shape
    bblk = max(1, min(b, block_bytes // (2 * n2 * c * 2)))
    assert b % bblk == 0
    return pl.pallas_call(
        _conv_mid_kernel,
        grid=(n1, b // bblk),
        in_specs=[pl.BlockSpec((None, 2 * n2, 2 * n2), lambda i, bb: (i, 0, 0)),
                  pl.BlockSpec((None, 2 * n2, 2 * n2), lambda i, bb: (i, 0, 0)),
                  pl.BlockSpec((bblk, 2, None, n2, c), lambda i, bb: (bb, 0, i, 0, 0)),
                  pl.BlockSpec((2, None, n2, c), lambda i, bb: (0, i, 0, 0))],
        out_specs=pl.BlockSpec((bblk, 2, None, n2, c), lambda i, bb: (bb, 0, i, 0, 0)),
        out_shape=jax.ShapeDtypeStruct(a.shape, BF16),
        compiler_params=_params("parallel", "parallel"),
        name="conv_mid",
    )(g, gi, a, kf)


def _conv_out_kernel(f_ref, z_ref, w_ref, x0_ref, bias_ref, o_ref):
    y = jnp.dot(f_ref[...], z_ref[...], preferred_element_type=F32)
    o_ref[...] = x0_ref[...] * (y + w_ref[...] * bias_ref[...])


def conv_out(f, z, w, x0, bias_l, ln=4096):
    b, k, lanes = z.shape
    m = f.shape[0]
    ln = min(ln, lanes)
    assert lanes % ln == 0 and bias_l.shape == (1, ln)
    return pl.pallas_call(
        _conv_out_kernel,
        grid=(b, lanes // ln),
        in_specs=[pl.BlockSpec((m, k), lambda i, j: (0, 0)),
                  pl.BlockSpec((None, k, ln), lambda i, j: (i, 0, j)),
                  pl.BlockSpec((None, m, ln), lambda i, j: (i, 0, j)),
                  pl.BlockSpec((None, m, ln), lambda i, j: (i, 0, j)),
                  pl.BlockSpec((1, ln), lambda i, j: (0, 0))],
        out_specs=pl.BlockSpec((None, m, ln), lambda i, j: (i, 0, j)),
        out_shape=jax.ShapeDtypeStruct((b, m, lanes), F32),
        compiler_params=_params("parallel", "parallel"),
        name="conv_out",
    )(f, z, w, x0, bias_l)


def _hyena_pre_kernel(x0_ref, x1_ref, v_ref, w0_ref, w1_ref, wv_ref, b0_ref, b1_ref, bv_ref, x0o_ref, wo_ref):
    seq = x0_ref.shape[0]
    row = lax.broadcasted_iota(jnp.int32, x0_ref.shape, 0)
    first, last = row == 0, row == seq - 1

    def conv(u_ref, w_ref, b_ref):
        u = u_ref[...].astype(F32)
        prev = jnp.where(first, 0.0, pltpu.roll(u, 1, axis=0))
        nxt = jnp.where(last, 0.0, pltpu.roll(u, seq - 1, axis=0))
        return b_ref[...] + prev * w_ref[0:1, :] + u * w_ref[1:2, :] + nxt * w_ref[2:3, :]

    x0o_ref[...] = conv(x0_ref, w0_ref, b0_ref)
    wo_ref[...] = conv(v_ref, wv_ref, bv_ref) * conv(x1_ref, w1_ref, b1_ref)


def hyena_pre(hproj, conv_w, conv_b):
    b, seq, _ = hproj.shape
    c = HYENA_WIDTH
    cb = max(LANES, min(c, (1 << 20) // seq))
    nb = c // cb
    u_spec = lambda g: pl.BlockSpec((None, seq, cb), lambda i, j: (i, 0, g * nb + j))
    w_spec = lambda g: pl.BlockSpec((3, cb), lambda i, j: (0, g * nb + j))
    b_spec = lambda g: pl.BlockSpec((1, cb), lambda i, j: (0, g * nb + j))
    o_spec = pl.BlockSpec((None, seq, cb), lambda i, j: (i, 0, j))
    cw = conv_w.astype(F32)
    cbias = conv_b.reshape(1, 3 * c).astype(F32)
    return pl.pallas_call(
        _hyena_pre_kernel,
        grid=(b, nb),
        in_specs=[u_spec(0), u_spec(1), u_spec(2), w_spec(0), w_spec(1), w_spec(2),
                  b_spec(0), b_spec(1), b_spec(2)],
        out_specs=[o_spec, o_spec],
        out_shape=[jax.ShapeDtypeStruct((b, seq, c), F32)] * 2,
        compiler_params=_params("parallel", "parallel"),
        name="hyena_pre",
    )(hproj, hproj, hproj, cw, cw, cw, cbias, cbias, cbias)


def hyena_branch(hproj, conv_w, conv_b, fw1, fb1, fw2, fb2, fw3, fb3, fw4, freq, hyena_bias):
    b, seq, _ = hproj.shape
    c = HYENA_WIDTH
    n = 2 * seq
    n1, n2 = _fft_split(n)
    n1h = n1 // 2
    cos1, sin1 = _slow_dft_tables(n1)
    g = _fast_dft_tables(n1, n2)
    gi = jnp.swapaxes(g, 1, 2)

    k_full, l1 = filter_mlp(seq, fw1, fb1, fw2, fb2, fw3, fb3, fw4, freq)
    scale = 1.0 / ((l1 + EPS) * n)
    fa_full = jnp.concatenate([cos1, -sin1], axis=0)
    ka = left_matmul(fa_full, k_full.reshape(1, n1, n2 * c), F32, precise=True)
    kf = spectrum_mid(g, ka.reshape(2, n1, n2, c), scale)

    x0c, w = hyena_pre(hproj, conv_w, conv_b)
    fa = fa_full[:, :n1h].astype(BF16)
    a = left_matmul(fa, w.reshape(b, n1h, n2 * c), BF16, precise=False)
    z = conv_mid(g.astype(BF16), gi.astype(BF16), a.reshape(b, 2, n1, n2, c), kf)
    fc = jnp.concatenate([cos1[:n1h], -sin1[:n1h]], axis=1).astype(BF16)
    ln = min(4096, n2 * c)
    bias_l = jnp.tile(hyena_bias.reshape(1, c).astype(F32), (1, ln // c))
    y = conv_out(fc, z.reshape(b, 2 * n1, n2 * c), w.reshape(b, n1h, n2 * c), x0c.reshape(b, n1h, n2 * c),
                 bias_l, ln=ln)
    return y.reshape(b, seq, c)


ATT_SUB = 128


def _dilated_attn_kernel(q_ref, kp_ref, kc_ref, kn_ref, vp_ref, vc_ref, vn_ref, o_ref, l_ref, *,
                         n_cls, dilation, slopes):
    tq = q_ref.shape[0]
    q0 = pl.program_id(1) * tq
    res = pl.program_id(2)
    sk = ATT_SUB + 2 * ATT_HALF
    scale = 1.0 / math.sqrt(HEAD_DIM)
    qi = lax.broadcasted_iota(jnp.int32, (ATT_SUB, sk), 0)
    kj = lax.broadcasted_iota(jnp.int32, (ATT_SUB, sk), 1) - ATT_HALF
    dist = jnp.abs(kj - qi)
    band = dist <= ATT_HALF
    adist = (dilation * dist).astype(F32)
    lane = lax.broadcasted_iota(jnp.int32, (sk, LANES), 1)
    low_k = lane < HEAD_DIM
    low_q = lax.broadcasted_iota(jnp.int32, (ATT_SUB, LANES), 1) < HEAD_DIM
    for sub in range(tq // ATT_SUB):
        r0 = sub * ATT_SUB
        kabs = q0 + r0 + kj
        mask = band & (kabs >= 0) & (kabs < n_cls)
        for pair in range(HEADS_PER_GROUP // 2):
            cs = slice(pair * LANES, (pair + 1) * LANES)
            q = q_ref[r0:r0 + ATT_SUB, cs]

            def keys(p_ref, c_ref, n_ref):
                parts = []
                if r0 == 0:
                    parts.append(p_ref[:, cs])
                    parts.append(c_ref[0:min(tq, ATT_SUB + ATT_HALF), cs])
                else:
                    parts.append(c_ref[r0 - ATT_HALF:min(tq, r0 + ATT_SUB + ATT_HALF), cs])
                if r0 + ATT_SUB + ATT_HALF > tq:
                    parts.append(n_ref[:, cs])
                return jnp.concatenate(parts, axis=0)

            k = keys(kp_ref, kc_ref, kn_ref)
            v = keys(vp_ref, vc_ref, vn_ref)
            acc = None
            ls, lses = [], []
            for e in range(2):
                sel = low_k if e == 0 else ~low_k
                ke = jnp.where(sel, k, jnp.zeros_like(k))
                ve = jnp.where(sel, v, jnp.zeros_like(v))
                s = lax.dot_general(q, ke, (((1,), (1,)), ((), ())), preferred_element_type=F32) * scale
                s = jnp.where(mask, s - slopes[2 * pair + e] * adist, NEG_INF)
                m = jnp.max(s, axis=-1, keepdims=True)
                p = jnp.exp(s - m)
                l = jnp.sum(p, axis=-1, keepdims=True)
                pv = jnp.dot(p.astype(BF16), ve, preferred_element_type=F32)
                acc = pv if acc is None else acc + pv
                ls.append(l)
                lses.append(m + jnp.log(l))
            inv = 1.0 / jnp.where(low_q, ls[0], ls[1])
            if dilation == 1:
                tok = slice(r0, r0 + ATT_SUB)
            else:
                tok = pl.ds(r0 * dilation + res, ATT_SUB, stride=dilation)
            o_ref[pair, tok, :] = acc * inv
            l_ref[pair, tok, :] = jnp.where(low_q, lses[0], lses[1])


def dilated_attention(qkv, batch, group, out_block_bytes=4 << 20):
    window, dilation = ATT_GROUPS[group]
    assert window // (2 * dilation) == ATT_HALF
    n_cls = qkv.shape[0] // batch
    seq = n_cls * dilation
    tq = min(256, n_cls, max(ATT_SUB, out_block_bytes // (dilation * ATT_WIDTH * 4)))
    assert n_cls % tq == 0 and tq % ATT_SUB == 0
    hb = tq // ATT_HALF
    n_halo = n_cls // ATT_HALF
    slopes = tuple(2.0 ** (-8.0 * (group * HEADS_PER_GROUP + h + 1.0) / N_ATT_HEADS)
                   for h in range(HEADS_PER_GROUP))
    x = qkv.reshape(batch, n_cls, dilation * 3 * ATT_WIDTH)
    cur = lambda which: pl.BlockSpec((None, tq, ATT_WIDTH), lambda i, j, r: (i, j, 3 * r + which))
    prev = lambda which: pl.BlockSpec((None, ATT_HALF, ATT_WIDTH),
                                      lambda i, j, r: (i, jnp.maximum(j * hb - 1, 0), 3 * r + which))
    nxt = lambda which: pl.BlockSpec((None, ATT_HALF, ATT_WIDTH),
                                     lambda i, j, r: (i, jnp.minimum((j + 1) * hb, n_halo - 1), 3 * r + which))
    npair = ATT_WIDTH // LANES
    o_spec = pl.BlockSpec((None, npair, tq * dilation, LANES), lambda i, j, r: (i, 0, j, 0))
    return pl.pallas_call(
        functools.partial(_dilated_attn_kernel, n_cls=n_cls, dilation=dilation, slopes=slopes),
        grid=(batch, n_cls // tq, dilation),
        in_specs=[cur(0), prev(1), cur(1), nxt(1), prev(2), cur(2), nxt(2)],
        out_specs=[o_spec, o_spec],
        out_shape=[jax.ShapeDtypeStruct((batch, npair, seq, LANES), F32)] * 2,
        compiler_params=_params("parallel", "parallel", "arbitrary"),
        name="dilated_attn",
    )(x, x, x, x, x, x, x)


def _merge_kernel(x_ref, ga_ref, gb_ref, ya_ref, o0_ref, o1_ref, o2_ref, l0_ref, l1_ref, l2_ref,
                  wa_ref, wb_ref, wo_ref, g2_ref, x1_ref, xn_ref):
    parts = []
    for pair in range(o0_ref.shape[0]):
        l0, l1, l2 = l0_ref[pair], l1_ref[pair], l2_ref[pair]
        m = jnp.maximum(jnp.maximum(l0, l1), l2)
        e0, e1, e2 = jnp.exp(l0 - m), jnp.exp(l1 - m), jnp.exp(l2 - m)
        yb = (e0 * o0_ref[pair] + e1 * o1_ref[pair] + e2 * o2_ref[pair]) / (e0 + e1 + e2)
        parts.append(yb.astype(BF16))
    pa = jnp.dot(ya_ref[...].astype(BF16), wa_ref[...], preferred_element_type=F32)
    pb = jnp.dot(jnp.concatenate(parts, axis=1), wb_ref[...], preferred_element_type=F32)
    merged = (jax.nn.sigmoid(ga_ref[...].astype(F32)) * pa
              + jax.nn.sigmoid(gb_ref[...].astype(F32)) * pb)
    x1 = x_ref[...] + jnp.dot(merged.astype(BF16), wo_ref[...], preferred_element_type=F32)
    x1_ref[...] = x1
    ms = jnp.mean(x1 * x1, axis=-1, keepdims=True)
    xn_ref[...] = (x1 * lax.rsqrt(ms + EPS) * g2_ref[...]).astype(BF16)


def merge(x, glog, ya, outs, lses, wa, wb, wo, g2, tm=256):
    t, d = x.shape
    _, npair, seq, _ = outs[0].shape
    tm = min(tm, seq)
    assert seq % tm == 0
    tiles = seq // tm
    row = lambda c, blk=0: pl.BlockSpec((tm, c), lambda i: (i, blk))
    const = lambda r, c: pl.BlockSpec((r, c), lambda i: (0, 0), pipeline_mode=pl.Buffered(1))
    att = pl.BlockSpec((None, npair, tm, LANES), lambda i: (i // tiles, 0, i % tiles, 0))
    aw = ATT_WIDTH
    return pl.pallas_call(
        _merge_kernel,
        grid=(t // tm,),
        in_specs=[row(d), row(d, 0), row(d, 1), row(HYENA_WIDTH), att, att, att, att, att, att,
                  const(HYENA_WIDTH, d), const(aw, d), const(d, d), const(1, d)],
        out_specs=[row(d), row(d)],
        out_shape=[jax.ShapeDtypeStruct((t, d), F32), jax.ShapeDtypeStruct((t, d), BF16)],
        compiler_params=_params("parallel"),
        name="merge",
    )(x, glog, glog, ya, *outs, *lses, wa, wb, wo, g2.reshape(1, d).astype(F32))


def _descending_max(curs, k, emit):
    curs = list(curs)
    for it in range(k):
        for c, cur in enumerate(curs):
            m = jnp.max(cur, axis=0, keepdims=True)
            emit(c, it, m)
            if it + 1 < k:
                curs[c] = jnp.where(cur == m, -jnp.inf, cur)


PEER_LOCKSTEP_HEADS = 2


def _peer_scores_kernel(xn_ref, wq_ref, sk_ref, g1_ref, g2_ref, top_ref, s_ref):
    k = PEER_TOPK
    sub = 8
    nh = PEER_HEADS * N_KEYS
    q = jnp.dot(xn_ref[...], wq_ref[...], preferred_element_type=F32).astype(BF16)
    low_rank = lax.broadcasted_iota(jnp.int32, (sub, LANES), 0) < 4
    for tc in range(xn_ref.shape[0] // LANES):
        cols = slice(tc * LANES, (tc + 1) * LANES)
        for h0 in range(0, PEER_HEADS, PEER_LOCKSTEP_HEADS):
            heads = range(h0, h0 + PEER_LOCKSTEP_HEADS)
            chains = [(hh, p) for hh in range(PEER_LOCKSTEP_HEADS) for p in range(2)]
            for c, (hh, p) in enumerate(chains):
                hp = 2 * (h0 + hh) + p
                qs = q[cols, hp * N_KEYS:(hp + 1) * N_KEYS]
                s_ref[c] = lax.dot_general(sk_ref[hp], qs, (((1,), (1,)), ((), ())), preferred_element_type=F32)

            def put_top(c, it, m):
                top_ref[c, it:it + 1, :] = m

            _descending_max([s_ref[c] for c in range(len(chains))], k, put_top)

            cands = []
            for hh in range(PEER_LOCKSTEP_HEADS):
                v1, v2 = top_ref.at[2 * hh], top_ref.at[2 * hh + 1]
                tiles = [v1[r:r + 1, :] + v2[0:sub, :] for r in range(4)]
                tiles += [v1[0:sub, :] + v2[r:r + 1, :] for r in range(4)]
                tiles += [v1[sub:k, :] + v2[0:1, :], v1[0:1, :] + v2[sub:k, :]]
                cands.append(jnp.concatenate(tiles, axis=0))
            repeated = [None] * 4 + [low_rank] * 4 + [None] * 2
            taus = [None] * PEER_LOCKSTEP_HEADS

            def put_tau(c, it, m):
                taus[c] = m

            _descending_max(cands, k, put_tau)
            for hh, h in enumerate(heads):
                rows = slice(h * N_KEYS, (h + 1) * N_KEYS)
                rows_e = slice(nh + h * N_KEYS, nh + (h + 1) * N_KEYS)
                v1, v2 = top_ref.at[2 * hh], top_ref.at[2 * hh + 1]
                tau = taus[hh]
                best = v1[0:1, :] + v2[0:1, :]
                zsum = None
                for ti, dup in enumerate(repeated):
                    tile = cands[hh][ti * sub:(ti + 1) * sub, :]
                    keep = tile >= tau
                    if dup is not None:
                        keep = keep & ~dup
                    part = jnp.sum(jnp.where(keep, jnp.exp(tile - best), 0.0), axis=0, keepdims=True)
                    zsum = part if zsum is None else zsum + part
                inv_z = 1.0 / zsum
                v2_all = v2[...]
                e2_top = jnp.exp(v2_all - v2[0:1, :]) * inv_z
                s1 = s_ref[2 * hh]
                theta = jnp.full(s1.shape, jnp.inf, F32)
                for r in range(k):
                    c = v1[r:r + 1, :] + v2_all
                    theta_r = jnp.min(jnp.where(c >= tau, e2_top, jnp.inf), axis=0, keepdims=True)
                    theta = jnp.where(s1 == v1[r:r + 1, :], theta_r, theta)
                g1_ref[rows, cols] = theta
                g1_ref[rows_e, cols] = jnp.exp(s1 - v1[0:1, :])
                g2_ref[tc, rows, :] = jnp.exp(s_ref[2 * hh + 1] - v2[0:1, :]) * inv_z


def peer_scores(xn, wq, subkeys, tm=256):
    t, d = xn.shape
    tm = min(tm, t)
    assert t % tm == 0 and tm % LANES == 0
    rows = 2 * PEER_HEADS * N_KEYS
    return pl.pallas_call(
        _peer_scores_kernel,
        grid=(t // tm,),
        in_specs=[pl.BlockSpec((tm, d), lambda i: (i, 0)),
                  pl.BlockSpec((d, rows), lambda i: (0, 0), pipeline_mode=pl.Buffered(1)),
                  pl.BlockSpec((2 * PEER_HEADS, N_KEYS, N_KEYS), lambda i: (0, 0, 0))],
        out_specs=[pl.BlockSpec((rows, tm), lambda i: (0, i)),
                   pl.BlockSpec((tm // LANES, rows // 2, LANES), lambda i: (i, 0, 0))],
        out_shape=[jax.ShapeDtypeStruct((rows, t), F32),
                   jax.ShapeDtypeStruct((t // LANES, rows // 2, LANES), F32)],
        scratch_shapes=[pltpu.VMEM((2 * PEER_LOCKSTEP_HEADS, PEER_TOPK, LANES), F32),
                        pltpu.VMEM((2 * PEER_LOCKSTEP_HEADS, N_KEYS, LANES), F32)],
        compiler_params=_params("parallel"),
        name="peer_scores",
    )(xn, wq, subkeys)


GELU_C0 = math.sqrt(2.0 / math.pi)
GELU_C1 = GELU_C0 * 0.044715


def _gelu_tanh(x):
    return x * (0.5 + 0.5 * jnp.tanh(x * (GELU_C0 + GELU_C1 * (x * x))))


def _peer_experts_kernel(xn_ref, g1_ref, g2_ref, u_ref, vt_ref, x1_ref, g_ref, y_ref, acc_ref, a_ref, wa_ref):
    j = pl.program_id(1)
    nblk = pl.num_programs(1) - 2
    _, nchunk, eb, _ = a_ref.shape
    tm = nchunk * LANES
    a_even, a_odd = a_ref.at[0], a_ref.at[1]
    wa_even, wa_odd = wa_ref.at[0], wa_ref.at[1]
    nh = PEER_HEADS * N_KEYS
    per = eb // N_KEYS
    half = tm // 2

    mrows = 256
    arows = 256

    def activations(a_ref, m, c):
        rows = slice(m * arows, (m + 1) * arows)
        cols = slice(c * half, (c + 1) * half)
        a = lax.dot_general(u_ref[rows, :], xn_ref[cols, :], (((1,), (1,)), ((), ())),
                            preferred_element_type=F32)
        for k in range(half // LANES):
            a_ref[c * (half // LANES) + k, rows, :] = a[:, k * LANES:(k + 1) * LANES]

    def gate_rows(block, ii):
        key1 = block * per + ii
        return ([g1_ref[pl.ds(h * N_KEYS + key1, 1), :] for h in range(PEER_HEADS)],
                [g1_ref[pl.ds(nh + h * N_KEYS + key1, 1), :] for h in range(PEER_HEADS)])

    def gates(a_ref, wa_ref, theta_rows, e1_rows, ii, tc):
        rows = slice(ii * N_KEYS, (ii + 1) * N_KEYS)
        cols = slice(tc * LANES, (tc + 1) * LANES)
        wsum = None
        for h in range(PEER_HEADS):
            e2 = g2_ref[tc, h * N_KEYS:(h + 1) * N_KEYS, :]
            term = e1_rows[h][:, cols] * jnp.where(e2 >= theta_rows[h][:, cols], e2, 0.0)
            wsum = term if wsum is None else wsum + term
        wa_ref[rows, cols] = (_gelu_tanh(a_ref[tc, rows, :]) * wsum).astype(BF16)

    def outputs(wa_ref, r, c):
        rows = slice(r * mrows, (r + 1) * mrows)
        cols = slice(c * half, (c + 1) * half)
        acc_ref[rows, cols] += jnp.dot(vt_ref[rows, :], wa_ref[:, cols], preferred_element_type=F32)

    def step(act, gate, out):
        vec, mxu_a, mxu_o = [], [], []
        rows_cache = {}

        def gate_piece(ii, tc):
            if ii not in rows_cache:
                rows_cache[ii] = gate_rows(j - 1, ii)
            gates(*gate, *rows_cache[ii], ii, tc)

        if gate:
            vec = [functools.partial(gate_piece, ii, tc) for ii in range(per) for tc in range(nchunk)]
        if act is not None:
            mxu_a = [functools.partial(activations, act, m, c) for m in range(eb // arows) for c in range(2)]
        if out is not None:
            mxu_o = [functools.partial(outputs, out, r, c) for c in range(2) for r in range(acc_ref.shape[0] // mrows)]
        n = max(len(vec), len(mxu_a), len(mxu_o))
        for k in range(n):
            for stage in (vec, mxu_a, mxu_o):
                for piece in stage[k * len(stage) // n:(k + 1) * len(stage) // n]:
                    piece()

    steady = (j > 1) & (j < nblk)

    @pl.when(j == 0)
    def _():
        acc_ref[...] = jnp.zeros_like(acc_ref)
        step(a_even, None, None)

    @pl.when(j == 1)
    def _():
        step(a_odd, (a_even, wa_even), None)

    @pl.when(steady & (j % 2 == 0))
    def _():
        step(a_even, (a_odd, wa_odd), wa_even)

    @pl.when(steady & (j % 2 == 1))
    def _():
        step(a_odd, (a_even, wa_even), wa_odd)

    @pl.when(j == nblk)
    def _():
        step(None, (a_odd, wa_odd), wa_even)

    @pl.when(j == nblk + 1)
    def _():
        step(None, None, wa_odd)
        x2 = x1_ref[...] + acc_ref[...].T
        ms = jnp.mean(x2 * x2, axis=-1, keepdims=True)
        y_ref[...] = x2 * lax.rsqrt(ms + EPS) * g_ref[...]


def peer_experts(xn, g1, g2, u, vt, x1, final_g, tm=512):
    t, d = xn.shape
    eb = vt.shape[2]
    tm = min(tm, t)
    assert t % tm == 0 and N_EXPERTS % (2 * eb) == 0 and eb % N_KEYS == 0 and tm % (2 * LANES) == 0
    rows = 2 * PEER_HEADS * N_KEYS
    nblk = N_EXPERTS // eb
    return pl.pallas_call(
        _peer_experts_kernel,
        grid=(t // tm, nblk + 2),
        in_specs=[pl.BlockSpec((tm, d), lambda i, j: (i, 0)),
                  pl.BlockSpec((rows, tm), lambda i, j: (0, i)),
                  pl.BlockSpec((tm // LANES, rows // 2, LANES), lambda i, j: (i, 0, 0)),
                  pl.BlockSpec((eb, d), lambda i, j: (jnp.minimum(j, nblk - 1), 0)),
                  pl.BlockSpec((None, d, eb), lambda i, j: (jnp.clip(j - 2, 0, nblk - 1), 0, 0)),
                  pl.BlockSpec((tm, d), lambda i, j: (i, 0), pipeline_mode=pl.Buffered(1)),
                  pl.BlockSpec((1, d), lambda i, j: (0, 0))],
        out_specs=pl.BlockSpec((tm, d), lambda i, j: (i, 0)),
        out_shape=jax.ShapeDtypeStruct((t, d), F32),
        scratch_shapes=[pltpu.VMEM((d, tm), F32), pltpu.VMEM((2, tm // LANES, eb, LANES), F32),
                        pltpu.VMEM((2, eb, tm), BF16)],
        compiler_params=_params("parallel", "arbitrary"),
        name="peer_experts",
    )(xn, g1, g2, u, vt, x1, final_g.reshape(1, d).astype(F32))


def _encoder(x, p):
    b, seq, d = x.shape
    t = b * seq
    xt = x.reshape(t, d)
    xn1 = rms_norm_bf16(xt, p["norm1_g"])
    hproj = project(xn1, p["w_in"], 0, HYENA_COLS, BF16)
    glog = project(xn1, p["w_in"], HYENA_COLS + ATT_COLS, GATE_COLS, BF16)
    ya = hyena_branch(hproj.reshape(b, seq, HYENA_COLS), p["conv_w"], p["conv_b"], p["filt_w1"], p["filt_b1"],
                      p["filt_w2"], p["filt_b2"], p["filt_w3"], p["filt_b3"], p["filt_w4"], p["filt_freq"],
                      p["hyena_bias"])
    att = [dilated_attention(project_classes(xn1, p["w_in"], g), b, g) for g in range(N_GROUPS)]
    x1, xn2 = merge(xt, glog, ya.reshape(t, HYENA_WIDTH), [o for o, _ in att], [l for _, l in att],
                    p["w_branch_a"], p["w_branch_b"], p["w_out"], p["norm2_g"])
    g1, g2 = peer_scores(xn2, p["peer_wq"], p["peer_subkeys"])
    y = peer_experts(xn2, g1, g2, p["peer_u"], p["peer_vt"], x1, p["final_g"])
    return y.reshape(b, seq, d)


def kernel(x_prompt, x_sample, norm1_g, w_in, conv_w, conv_b, filt_w1, filt_b1, filt_w2, filt_b2, filt_w3,
           filt_b3, filt_w4, filt_freq, hyena_bias, w_branch_a, w_branch_b, w_out, norm2_g, peer_wq,
           peer_subkeys, peer_u, peer_v, final_g):
    assert norm1_g.shape[0] == 1, "single-layer encoder"
    p = dict(
        norm1_g=norm1_g[0].astype(F32), w_in=w_in[0].astype(BF16), conv_w=conv_w[0], conv_b=conv_b[0],
        filt_w1=filt_w1[0], filt_b1=filt_b1[0], filt_w2=filt_w2[0], filt_b2=filt_b2[0], filt_w3=filt_w3[0],
        filt_b3=filt_b3[0], filt_w4=filt_w4[0], filt_freq=filt_freq[0], hyena_bias=hyena_bias[0],
        w_branch_a=w_branch_a[0].astype(BF16), w_branch_b=w_branch_b[0].astype(BF16),
        w_out=w_out[0].astype(BF16), norm2_g=norm2_g[0], peer_wq=peer_wq[0].astype(BF16),
        peer_subkeys=peer_subkeys[0].reshape(2 * PEER_HEADS, N_KEYS, N_KEYS).astype(BF16),
        peer_u=peer_u[0].astype(BF16),
        peer_vt=peer_v[0].astype(BF16).reshape(N_EXPERTS // PEER_EB, PEER_EB, D_MODEL).transpose(0, 2, 1),
        final_g=final_g,
    )
    return (_encoder(x_prompt, p), _encoder(x_sample, p))
```

```python
import functools
import math

import jax
import jax.numpy as jnp
from jax import lax
from jax.experimental import pallas as pl
from jax.experimental.pallas import tpu as pltpu

F32 = jnp.float32
BF16 = jnp.bfloat16
F8 = jnp.float8_e4m3fn
PEER_U_SCALE = 32.0
PEER_W_SCALE = 16.0

D_MODEL = 2048
HYENA_WIDTH = 1024
FILTER_EMB = 33
FILTER_EMB_PAD = 128
FILTER_HIDDEN = 64
DECAY_TARGET = 1e-2
FAST_DECAY_PCT = 0.3
SLOW_DECAY_PCT = 1.5
ATT_GROUPS = ((128, 1), (512, 4), (2048, 16))
N_GROUPS = 3
HEADS_PER_GROUP = 8
HEAD_DIM = 64
N_ATT_HEADS = N_GROUPS * HEADS_PER_GROUP
ATT_WIDTH = HEADS_PER_GROUP * HEAD_DIM
HYENA_COLS = 3 * HYENA_WIDTH
ATT_COLS = 3 * N_GROUPS * ATT_WIDTH
GATE_COLS = 2 * D_MODEL
PEER_HEADS = 8
N_KEYS = 128
N_EXPERTS = N_KEYS * N_KEYS
PEER_TOPK = 16
PEER_EB = 512
EPS = 1e-6
NEG_INF = -1e30
ATT_HALF = 64
LANES = 128
VMEM_LIMIT = 56 * 1024 * 1024
HIGHEST = lax.Precision.HIGHEST


def _params(*sem, flags=None):
    return pltpu.CompilerParams(dimension_semantics=sem, vmem_limit_bytes=VMEM_LIMIT, flags=flags)


def _rms_norm_kernel(x_ref, g_ref, o_ref):
    x = x_ref[...]
    ms = jnp.mean(x * x, axis=-1, keepdims=True)
    o_ref[...] = (x * lax.rsqrt(ms + EPS) * g_ref[...]).astype(o_ref.dtype)


def rms_norm_bf16(x, g, tm=512):
    t, d = x.shape
    tm = min(tm, t)
    assert t % tm == 0
    return pl.pallas_call(
        _rms_norm_kernel,
        grid=(t // tm,),
        in_specs=[pl.BlockSpec((tm, d), lambda i: (i, 0)), pl.BlockSpec((1, d), lambda i: (0, 0))],
        out_specs=pl.BlockSpec((tm, d), lambda i: (i, 0)),
        out_shape=jax.ShapeDtypeStruct((t, d), BF16),
        compiler_params=_params("parallel"),
        name="rms_norm",
    )(x, g.reshape(1, d))


def _project_kernel(x_ref, w_ref, o_ref):
    o_ref[...] = jnp.dot(x_ref[...], w_ref[...], preferred_element_type=F32).astype(o_ref.dtype)


def project(xn, w, col0, ncols, out_dtype, tm=2048, tn=512):
    t, d = xn.shape
    tm = min(tm, t)
    assert t % tm == 0 and ncols % tn == 0 and col0 % tn == 0
    cb0 = col0 // tn
    return pl.pallas_call(
        _project_kernel,
        grid=(t // tm, ncols // tn),
        in_specs=[
            pl.BlockSpec((tm, d), lambda i, j: (i, 0)),
            pl.BlockSpec((d, tn), lambda i, j: (0, cb0 + j)),
        ],
        out_specs=pl.BlockSpec((tm, tn), lambda i, j: (i, j)),
        out_shape=jax.ShapeDtypeStruct((t, ncols), out_dtype),
        compiler_params=_params("parallel", "arbitrary"),
        name="project",
    )(xn, w)


def _project_classes_kernel(x_ref, w_ref, o_ref, acc_ref, *, dilation):
    which = pl.program_id(1)
    acc = jnp.dot(x_ref[...], w_ref[...], preferred_element_type=F32)
    nlane, tm, _ = acc_ref.shape
    for k in range(nlane):
        acc_ref[k] = acc[:, k * LANES:(k + 1) * LANES]
    rows = tm // dilation
    for part in range(3):
        @pl.when(which == part)
        def _(part=part):
            for r in range(dilation):
                for k in range(nlane):
                    piece = acc_ref[k] if dilation == 1 else acc_ref[k, pl.ds(r, rows, stride=dilation), :]
                    c0 = (3 * r + part) * nlane * LANES + k * LANES
                    o_ref[:, c0:c0 + LANES] = piece.astype(o_ref.dtype)


def project_classes(xn, w, group, tm=2048):
    t, d = xn.shape
    dilation = ATT_GROUPS[group][1]
    tm = min(tm, t)
    assert t % tm == 0 and tm % (8 * dilation) == 0
    cb0 = HYENA_COLS // ATT_WIDTH + group
    return pl.pallas_call(
        functools.partial(_project_classes_kernel, dilation=dilation),
        grid=(t // tm, 3),
        in_specs=[
            pl.BlockSpec((tm, d), lambda i, j: (i, 0)),
            pl.BlockSpec((d, ATT_WIDTH), lambda i, j: (0, cb0 + N_GROUPS * j)),
        ],
        out_specs=pl.BlockSpec((tm // dilation, dilation * 3 * ATT_WIDTH), lambda i, j: (i, 0)),
        out_shape=jax.ShapeDtypeStruct((t // dilation, dilation * 3 * ATT_WIDTH), BF16),
        scratch_shapes=[pltpu.VMEM((ATT_WIDTH // LANES, tm, LANES), F32)],
        compiler_params=_params("parallel", "arbitrary"),
        name="project_classes",
    )(xn, w)


def _filter_mlp_kernel(z_ref, t_ref, keep_ref, w1_ref, b1_ref, w2_ref, b2_ref, w3_ref, b3_ref, w4_ref, fr_ref,
                       dl_ref, h_ref, l1_ref):
    fr = fr_ref[...]
    h = jnp.sin(fr * (jnp.dot(z_ref[...], w1_ref[...], precision=HIGHEST, preferred_element_type=F32)
                      + b1_ref[...]))
    h = jnp.sin(fr * (jnp.dot(h, w2_ref[...], precision=HIGHEST, preferred_element_type=F32) + b2_ref[...]))
    h = jnp.sin(fr * (jnp.dot(h, w3_ref[...], precision=HIGHEST, preferred_element_type=F32) + b3_ref[...]))
    h = jnp.dot(h, w4_ref[...], precision=HIGHEST, preferred_element_type=F32)
    h = h * jnp.exp(-t_ref[...] * dl_ref[...])
    h_ref[...] = h * keep_ref[...]

    @pl.when(pl.program_id(0) == 0)
    def _():
        l1_ref[...] = jnp.zeros_like(l1_ref)

    l1_ref[...] += jnp.sum(jnp.abs(h), axis=0, keepdims=True)


def filter_mlp(seq, w1, b1, w2, b2, w3, b3, w4, freq, tl=512):
    t = jnp.linspace(0.0, 1.0, seq, dtype=F32)[:, None]
    bands = (FILTER_EMB - 1) // 2
    ang = 2.0 * math.pi * jnp.arange(seq, dtype=F32)[:, None] / seq
    fb = jnp.linspace(1e-4, bands - 1, bands, dtype=F32)[None, :]
    z = jnp.concatenate([t, jnp.cos(fb * ang), -jnp.sin(fb * ang)], axis=-1)
    z = jnp.pad(z, ((0, 0), (0, FILTER_EMB_PAD - FILTER_EMB)))
    pos = jnp.concatenate([jnp.arange(seq), jnp.zeros((1,), jnp.int32), jnp.arange(seq - 1, 0, -1)])
    keep = (jnp.arange(2 * seq) != seq).astype(F32)[:, None]
    w1p = jnp.pad(w1.astype(F32), ((0, FILTER_EMB_PAD - FILTER_EMB), (0, 0)))
    min_decay = math.log(DECAY_TARGET) / SLOW_DECAY_PCT
    max_decay = math.log(DECAY_TARGET) / FAST_DECAY_PCT
    deltas = jnp.abs(jnp.linspace(min_decay, max_decay, HYENA_WIDTH, dtype=F32))[None, :]
    tl = min(tl, seq)
    assert seq % tl == 0
    c = HYENA_WIDTH
    hid = FILTER_HIDDEN
    per_dir = seq // tl
    full = lambda r, cc: pl.BlockSpec((r, cc), lambda i: (0, 0))
    return pl.pallas_call(
        _filter_mlp_kernel,
        grid=(2 * per_dir,),
        in_specs=[
            pl.BlockSpec((tl, FILTER_EMB_PAD), lambda i: (i, 0)),
            pl.BlockSpec((tl, 1), lambda i: (i, 0)),
            pl.BlockSpec((tl, 1), lambda i: (i, 0)),
            full(FILTER_EMB_PAD, hid), full(1, hid), full(hid, hid), full(1, hid), full(hid, hid),
            full(1, hid), pl.BlockSpec((hid, c), lambda i: (0, i // per_dir)), full(1, hid), full(1, c),
        ],
        out_specs=[pl.BlockSpec((tl, c), lambda i: (i, 0)), pl.BlockSpec((1, c), lambda i: (0, 0))],
        out_shape=[jax.ShapeDtypeStruct((2 * seq, c), F32), jax.ShapeDtypeStruct((1, c), F32)],
        compiler_params=_params("arbitrary"),
        name="filter_mlp",
    )(z[pos], t[pos], keep, w1p, b1.reshape(1, hid).astype(F32), w2.astype(F32), b2.reshape(1, hid).astype(F32),
      w3.astype(F32), b3.reshape(1, hid).astype(F32), w4.astype(F32), freq.reshape(1, hid).astype(F32), deltas)


def _fft_split(n):
    lg = int(math.log2(n))
    assert 1 << lg == n
    n1 = 1 << ((lg + 1) // 2)
    return n1, n // n1


def _slow_dft_tables(n1):
    a = jnp.arange(n1, dtype=jnp.int32)
    ang = ((a[:, None] * a[None, :]) % n1).astype(F32) * (2.0 * math.pi / n1)
    return jnp.cos(ang), jnp.sin(ang)


def _fast_dft_tables(n1, n2):
    n = n1 * n2
    k1 = jnp.arange(n1, dtype=jnp.int32)[:, None, None]
    k2 = jnp.arange(n2, dtype=jnp.int32)[None, :, None]
    f = jnp.arange(n2, dtype=jnp.int32)[None, None, :]
    ang = ((f * (k1 + n1 * k2)) % n).astype(F32) * (2.0 * math.pi / n)
    c, s = jnp.cos(ang), jnp.sin(ang)
    top = jnp.concatenate([c, s], axis=2)
    bot = jnp.concatenate([-s, c], axis=2)
    return jnp.concatenate([top, bot], axis=1)


def _left_matmul_kernel(f_ref, x_ref, o_ref, *, precise):
    if precise:
        o = jnp.dot(f_ref[...], x_ref[...], precision=HIGHEST, preferred_element_type=F32)
    else:
        o = jnp.dot(f_ref[...], x_ref[...].astype(BF16), preferred_element_type=F32)
    o_ref[...] = o.astype(o_ref.dtype)


def left_matmul(f, x, out_dtype, precise, ln=4096):
    b, k, lanes = x.shape
    m = f.shape[0]
    ln = min(ln, lanes)
    assert lanes % ln == 0 and f.shape[1] == k
    return pl.pallas_call(
        functools.partial(_left_matmul_kernel, precise=precise),
        grid=(b, lanes // ln),
        in_specs=[pl.BlockSpec((m, k), lambda i, j: (0, 0)),
                  pl.BlockSpec((None, k, ln), lambda i, j: (i, 0, j))],
        out_specs=pl.BlockSpec((None, m, ln), lambda i, j: (i, 0, j)),
        out_shape=jax.ShapeDtypeStruct((b, m, lanes), out_dtype),
        compiler_params=_params("parallel", "parallel"),
        name="left_matmul",
    )(f, x)


def _spectrum_mid_kernel(g_ref, a_ref, sc_ref, o_ref):
    n2 = a_ref.shape[1]
    a = a_ref[...].reshape(2 * n2, a_ref.shape[2])
    x = jnp.dot(g_ref[...], a, precision=HIGHEST, preferred_element_type=F32) * sc_ref[...]
    o_ref[...] = x.reshape(o_ref.shape)


def spectrum_mid(g, a, scale, cb=512):
    _, n1, n2, c = a.shape
    cb = min(cb, c)
    return pl.pallas_call(
        _spectrum_mid_kernel,
        grid=(n1, c // cb),
        in_specs=[pl.BlockSpec((None, 2 * n2, 2 * n2), lambda i, j: (i, 0, 0)),
                  pl.BlockSpec((2, None, n2, cb), lambda i, j: (0, i, 0, j)),
                  pl.BlockSpec((1, cb), lambda i, j: (0, j))],
        out_specs=pl.BlockSpec((2, None, n2, cb), lambda i, j: (0, i, 0, j)),
        out_shape=jax.ShapeDtypeStruct(a.shape, F32),
        compiler_params=_params("parallel", "parallel"),
        name="spectrum_mid",
    )(g, a, scale)


def _conv_mid_kernel(g_ref, gi_ref, a_ref, k_ref, o_ref):
    bblk, _, n2, cb = a_ref.shape
    kr, ki = k_ref[0], k_ref[1]
    for bi in range(bblk):
        a = a_ref[bi].reshape(2 * n2, cb)
        x = jnp.dot(g_ref[...], a, preferred_element_type=F32)
        xr, xi = x[:n2], x[n2:]
        y = jnp.concatenate([xr * kr - xi * ki, xr * ki + xi * kr], axis=0).astype(BF16)
        z = jnp.dot(gi_ref[...], y, preferred_element_type=F32)
        o_ref[bi] = z.reshape(2, n2, cb).astype(o_ref.dtype)


def conv_mid(g, gi, a, kf, block_bytes=2 << 20):
    b, _, n1, n2, c = a.shape
    bblk = max(1, min(b, block_bytes // (2 * n2 * c * 2)))
    assert b % bblk == 0
    return pl.pallas_call(
        _conv_mid_kernel,
        grid=(n1, b // bblk),
        in_specs=[pl.BlockSpec((None, 2 * n2, 2 * n2), lambda i, bb: (i, 0, 0)),
                  pl.BlockSpec((None, 2 * n2, 2 * n2), lambda i, bb: (i, 0, 0)),
                  pl.BlockSpec((bblk, 2, None, n2, c), lambda i, bb: (bb, 0, i, 0, 0)),
                  pl.BlockSpec((2, None, n2, c), lambda i, bb: (0, i, 0, 0))],
        out_specs=pl.BlockSpec((bblk, 2, None, n2, c), lambda i, bb: (bb, 0, i, 0, 0)),
        out_shape=jax.ShapeDtypeStruct(a.shape, BF16),
        compiler_params=_params("parallel", "parallel"),
        name="conv_mid",
    )(g, gi, a, kf)


def _conv_out_kernel(f_ref, z_ref, w_ref, x0_ref, bias_ref, o_ref):
    y = jnp.dot(f_ref[...], z_ref[...], preferred_element_type=F32)
    o_ref[...] = x0_ref[...] * (y + w_ref[...] * bias_ref[...])


def conv_out(f, z, w, x0, bias_l, ln=4096):
    b, k, lanes = z.shape
    m = f.shape[0]
    ln = min(ln, lanes)
    assert lanes % ln == 0 and bias_l.shape == (1, ln)
    return pl.pallas_call(
        _conv_out_kernel,
        grid=(b, lanes // ln),
        in_specs=[pl.BlockSpec((m, k), lambda i, j: (0, 0)),
                  pl.BlockSpec((None, k, ln), lambda i, j: (i, 0, j)),
                  pl.BlockSpec((None, m, ln), lambda i, j: (i, 0, j)),
                  pl.BlockSpec((None, m, ln), lambda i, j: (i, 0, j)),
                  pl.BlockSpec((1, ln), lambda i, j: (0, 0))],
        out_specs=pl.BlockSpec((None, m, ln), lambda i, j: (i, 0, j)),
        out_shape=jax.ShapeDtypeStruct((b, m, lanes), F32),
        compiler_params=_params("parallel", "parallel"),
        name="conv_out",
    )(f, z, w, x0, bias_l)


def _hyena_pre_kernel(x0_ref, x1_ref, v_ref, w0_ref, w1_ref, wv_ref, b0_ref, b1_ref, bv_ref, x0o_ref, wo_ref):
    seq = x0_ref.shape[0]
    row = lax.broadcasted_iota(jnp.int32, x0_ref.shape, 0)
    first, last = row == 0, row == seq - 1

    def conv(u_ref, w_ref, b_ref):
        u = u_ref[...].astype(F32)
        prev = jnp.where(first, 0.0, pltpu.roll(u, 1, axis=0))
        nxt = jnp.where(last, 0.0, pltpu.roll(u, seq - 1, axis=0))
        return b_ref[...] + prev * w_ref[0:1, :] + u * w_ref[1:2, :] + nxt * w_ref[2:3, :]

    x0o_ref[...] = conv(x0_ref, w0_ref, b0_ref)
    wo_ref[...] = conv(v_ref, wv_ref, bv_ref) * conv(x1_ref, w1_ref, b1_ref)


def hyena_pre(hproj, conv_w, conv_b):
    b, seq, _ = hproj.shape
    c = HYENA_WIDTH
    cb = max(LANES, min(c, (1 << 20) // seq))
    nb = c // cb
    u_spec = lambda g: pl.BlockSpec((None, seq, cb), lambda i, j: (i, 0, g * nb + j))
    w_spec = lambda g: pl.BlockSpec((3, cb), lambda i, j: (0, g * nb + j))
    b_spec = lambda g: pl.BlockSpec((1, cb), lambda i, j: (0, g * nb + j))
    o_spec = pl.BlockSpec((None, seq, cb), lambda i, j: (i, 0, j))
    cw = conv_w.astype(F32)
    cbias = conv_b.reshape(1, 3 * c).astype(F32)
    return pl.pallas_call(
        _hyena_pre_kernel,
        grid=(b, nb),
        in_specs=[u_spec(0), u_spec(1), u_spec(2), w_spec(0), w_spec(1), w_spec(2),
                  b_spec(0), b_spec(1), b_spec(2)],
        out_specs=[o_spec, o_spec],
        out_shape=[jax.ShapeDtypeStruct((b, seq, c), F32)] * 2,
        compiler_params=_params("parallel", "parallel"),
        name="hyena_pre",
    )(hproj, hproj, hproj, cw, cw, cw, cbias, cbias, cbias)


def hyena_branch(hproj, conv_w, conv_b, fw1, fb1, fw2, fb2, fw3, fb3, fw4, freq, hyena_bias):
    b, seq, _ = hproj.shape
    c = HYENA_WIDTH
    n = 2 * seq
    n1, n2 = _fft_split(n)
    n1h = n1 // 2
    cos1, sin1 = _slow_dft_tables(n1)
    g = _fast_dft_tables(n1, n2)
    gi = jnp.swapaxes(g, 1, 2)

    k_full, l1 = filter_mlp(seq, fw1, fb1, fw2, fb2, fw3, fb3, fw4, freq)
    scale = 1.0 / ((l1 + EPS) * n)
    fa_full = jnp.concatenate([cos1, -sin1], axis=0)
    ka = left_matmul(fa_full, k_full.reshape(1, n1, n2 * c), F32, precise=True)
    kf = spectrum_mid(g, ka.reshape(2, n1, n2, c), scale)

    x0c, w = hyena_pre(hproj, conv_w, conv_b)
    fa = fa_full[:, :n1h].astype(BF16)
    a = left_matmul(fa, w.reshape(b, n1h, n2 * c), BF16, precise=False)
    z = conv_mid(g.astype(BF16), gi.astype(BF16), a.reshape(b, 2, n1, n2, c), kf)
    fc = jnp.concatenate([cos1[:n1h], -sin1[:n1h]], axis=1).astype(BF16)
    ln = min(4096, n2 * c)
    bias_l = jnp.tile(hyena_bias.reshape(1, c).astype(F32), (1, ln // c))
    y = conv_out(fc, z.reshape(b, 2 * n1, n2 * c), w.reshape(b, n1h, n2 * c), x0c.reshape(b, n1h, n2 * c),
                 bias_l, ln=ln)
    return y.reshape(b, seq, c)


ATT_SUB = 128


def _dilated_attn_kernel(q_ref, kp_ref, kc_ref, kn_ref, vp_ref, vc_ref, vn_ref, o_ref, l_ref, *,
                         n_cls, dilation, slopes):
    tq = q_ref.shape[0]
    q0 = pl.program_id(1) * tq
    res = pl.program_id(2)
    sk = ATT_SUB + 2 * ATT_HALF
    scale = 1.0 / math.sqrt(HEAD_DIM)
    qi = lax.broadcasted_iota(jnp.int32, (ATT_SUB, sk), 0)
    kj = lax.broadcasted_iota(jnp.int32, (ATT_SUB, sk), 1) - ATT_HALF
    dist = jnp.abs(kj - qi)
    band = dist <= ATT_HALF
    adist = (dilation * dist).astype(F32)
    lane = lax.broadcasted_iota(jnp.int32, (sk, LANES), 1)
    low_k = lane < HEAD_DIM
    low_q = lax.broadcasted_iota(jnp.int32, (ATT_SUB, LANES), 1) < HEAD_DIM
    for sub in range(tq // ATT_SUB):
        r0 = sub * ATT_SUB
        kabs = q0 + r0 + kj
        mask = band & (kabs >= 0) & (kabs < n_cls)
        for pair in range(HEADS_PER_GROUP // 2):
            cs = slice(pair * LANES, (pair + 1) * LANES)
            q = q_ref[r0:r0 + ATT_SUB, cs]

            def keys(p_ref, c_ref, n_ref):
                parts = []
                if r0 == 0:
                    parts.append(p_ref[:, cs])
                    parts.append(c_ref[0:min(tq, ATT_SUB + ATT_HALF), cs])
                else:
                    parts.append(c_ref[r0 - ATT_HALF:min(tq, r0 + ATT_SUB + ATT_HALF), cs])
                if r0 + ATT_SUB + ATT_HALF > tq:
                    parts.append(n_ref[:, cs])
                return jnp.concatenate(parts, axis=0)

            k = keys(kp_ref, kc_ref, kn_ref)
            v = keys(vp_ref, vc_ref, vn_ref)
            acc = None
            ls, lses = [], []
            for e in range(2):
                sel = low_k if e == 0 else ~low_k
                ke = jnp.where(sel, k, jnp.zeros_like(k))
                ve = jnp.where(sel, v, jnp.zeros_like(v))
                s = lax.dot_general(q, ke, (((1,), (1,)), ((), ())), preferred_element_type=F32) * scale
                s = jnp.where(mask, s - slopes[2 * pair + e] * adist, NEG_INF)
                m = jnp.max(s, axis=-1, keepdims=True)
                p = jnp.exp(s - m)
                l = jnp.sum(p, axis=-1, keepdims=True)
                pv = jnp.dot(p.astype(BF16), ve, preferred_element_type=F32)
                acc = pv if acc is None else acc + pv
                ls.append(l)
                lses.append(m + jnp.log(l))
            inv = 1.0 / jnp.where(low_q, ls[0], ls[1])
            if dilation == 1:
                tok = slice(r0, r0 + ATT_SUB)
            else:
                tok = pl.ds(r0 * dilation + res, ATT_SUB, stride=dilation)
            o_ref[pair, tok, :] = acc * inv
            l_ref[pair, tok, :] = jnp.where(low_q, lses[0], lses[1])


def dilated_attention(qkv, batch, group, out_block_bytes=4 << 20):
    window, dilation = ATT_GROUPS[group]
    assert window // (2 * dilation) == ATT_HALF
    n_cls = qkv.shape[0] // batch
    seq = n_cls * dilation
    tq = min(256, n_cls, max(ATT_SUB, out_block_bytes // (dilation * ATT_WIDTH * 4)))
    assert n_cls % tq == 0 and tq % ATT_SUB == 0
    hb = tq // ATT_HALF
    n_halo = n_cls // ATT_HALF
    slopes = tuple(2.0 ** (-8.0 * (group * HEADS_PER_GROUP + h + 1.0) / N_ATT_HEADS)
                   for h in range(HEADS_PER_GROUP))
    x = qkv.reshape(batch, n_cls, dilation * 3 * ATT_WIDTH)
    cur = lambda which: pl.BlockSpec((None, tq, ATT_WIDTH), lambda i, j, r: (i, j, 3 * r + which))
    prev = lambda which: pl.BlockSpec((None, ATT_HALF, ATT_WIDTH),
                                      lambda i, j, r: (i, jnp.maximum(j * hb - 1, 0), 3 * r + which))
    nxt = lambda which: pl.BlockSpec((None, ATT_HALF, ATT_WIDTH),
                                     lambda i, j, r: (i, jnp.minimum((j + 1) * hb, n_halo - 1), 3 * r + which))
    npair = ATT_WIDTH // LANES
    o_spec = pl.BlockSpec((None, npair, tq * dilation, LANES), lambda i, j, r: (i, 0, j, 0))
    return pl.pallas_call(
        functools.partial(_dilated_attn_kernel, n_cls=n_cls, dilation=dilation, slopes=slopes),
        grid=(batch, n_cls // tq, dilation),
        in_specs=[cur(0), prev(1), cur(1), nxt(1), prev(2), cur(2), nxt(2)],
        out_specs=[o_spec, o_spec],
        out_shape=[jax.ShapeDtypeStruct((batch, npair, seq, LANES), F32)] * 2,
        compiler_params=_params("parallel", "parallel", "arbitrary"),
        name="dilated_attn",
    )(x, x, x, x, x, x, x)


def _merge_kernel(x_ref, ga_ref, gb_ref, ya_ref, o0_ref, o1_ref, o2_ref, l0_ref, l1_ref, l2_ref,
                  wa_ref, wb_ref, wo_ref, g2_ref, x1_ref, xn_ref, xn8_ref):
    parts = []
    for pair in range(o0_ref.shape[0]):
        l0, l1, l2 = l0_ref[pair], l1_ref[pair], l2_ref[pair]
        m = jnp.maximum(jnp.maximum(l0, l1), l2)
        e0, e1, e2 = jnp.exp(l0 - m), jnp.exp(l1 - m), jnp.exp(l2 - m)
        yb = (e0 * o0_ref[pair] + e1 * o1_ref[pair] + e2 * o2_ref[pair]) / (e0 + e1 + e2)
        parts.append(yb.astype(BF16))
    pa = jnp.dot(ya_ref[...].astype(BF16), wa_ref[...], preferred_element_type=F32)
    pb = jnp.dot(jnp.concatenate(parts, axis=1), wb_ref[...], preferred_element_type=F32)
    merged = (jax.nn.sigmoid(ga_ref[...].astype(F32)) * pa
              + jax.nn.sigmoid(gb_ref[...].astype(F32)) * pb)
    x1 = x_ref[...] + jnp.dot(merged.astype(BF16), wo_ref[...], preferred_element_type=F32)
    x1_ref[...] = x1
    ms = jnp.mean(x1 * x1, axis=-1, keepdims=True)
    xn = x1 * lax.rsqrt(ms + EPS) * g2_ref[...]
    xn_ref[...] = xn.astype(xn_ref.dtype)
    xn8_ref[...] = xn.astype(xn8_ref.dtype)


def merge(x, glog, ya, outs, lses, wa, wb, wo, g2, tm=256):
    t, d = x.shape
    _, npair, seq, _ = outs[0].shape
    tm = min(tm, seq)
    assert seq % tm == 0
    tiles = seq // tm
    row = lambda c, blk=0: pl.BlockSpec((tm, c), lambda i: (i, blk))
    const = lambda r, c: pl.BlockSpec((r, c), lambda i: (0, 0), pipeline_mode=pl.Buffered(1))
    att = pl.BlockSpec((None, npair, tm, LANES), lambda i: (i // tiles, 0, i % tiles, 0))
    aw = ATT_WIDTH
    return pl.pallas_call(
        _merge_kernel,
        grid=(t // tm,),
        in_specs=[row(d), row(d, 0), row(d, 1), row(HYENA_WIDTH), att, att, att, att, att, att,
                  const(HYENA_WIDTH, d), const(aw, d), const(d, d), const(1, d)],
        out_specs=[row(d), row(d), row(d)],
        out_shape=[jax.ShapeDtypeStruct((t, d), F32), jax.ShapeDtypeStruct((t, d), BF16),
                   jax.ShapeDtypeStruct((t, d), F8)],
        compiler_params=_params("parallel"),
        name="merge",
    )(x, glog, glog, ya, *outs, *lses, wa, wb, wo, g2.reshape(1, d).astype(F32))


def _descending_max(curs, k, emit):
    curs = list(curs)
    for it in range(k):
        for c, cur in enumerate(curs):
            m = jnp.max(cur, axis=0, keepdims=True)
            emit(c, it, m)
            if it + 1 < k:
                curs[c] = jnp.where(cur == m, -jnp.inf, cur)


PEER_LOCKSTEP_HEADS = 2


def _peer_scores_kernel(xn_ref, wq_ref, sk_ref, g1_ref, g2_ref, top_ref, s_ref):
    k = PEER_TOPK
    sub = 8
    nh = PEER_HEADS * N_KEYS
    q = jnp.dot(xn_ref[...], wq_ref[...], preferred_element_type=F32).astype(BF16)
    low_rank = lax.broadcasted_iota(jnp.int32, (sub, LANES), 0) < 4
    for tc in range(xn_ref.shape[0] // LANES):
        cols = slice(tc * LANES, (tc + 1) * LANES)
        for h0 in range(0, PEER_HEADS, PEER_LOCKSTEP_HEADS):
            heads = range(h0, h0 + PEER_LOCKSTEP_HEADS)
            chains = [(hh, p) for hh in range(PEER_LOCKSTEP_HEADS) for p in range(2)]
            for c, (hh, p) in enumerate(chains):
                hp = 2 * (h0 + hh) + p
                qs = q[cols, hp * N_KEYS:(hp + 1) * N_KEYS]
                s_ref[c] = lax.dot_general(sk_ref[hp], qs, (((1,), (1,)), ((), ())), preferred_element_type=F32)

            def put_top(c, it, m):
                top_ref[c, it:it + 1, :] = m

            _descending_max([s_ref[c] for c in range(len(chains))], k, put_top)

            cands = []
            for hh in range(PEER_LOCKSTEP_HEADS):
                v1, v2 = top_ref.at[2 * hh], top_ref.at[2 * hh + 1]
                tiles = [v1[r:r + 1, :] + v2[0:sub, :] for r in range(4)]
                tiles += [v1[0:sub, :] + v2[r:r + 1, :] for r in range(4)]
                tiles += [v1[sub:k, :] + v2[0:1, :], v1[0:1, :] + v2[sub:k, :]]
                cands.append(jnp.concatenate(tiles, axis=0))
            repeated = [None] * 4 + [low_rank] * 4 + [None] * 2
            taus = [None] * PEER_LOCKSTEP_HEADS

            def put_tau(c, it, m):
                taus[c] = m

            _descending_max(cands, k, put_tau)
            for hh, h in enumerate(heads):
                rows = slice(h * N_KEYS, (h + 1) * N_KEYS)
                rows_e = slice(nh + h * N_KEYS, nh + (h + 1) * N_KEYS)
                v1, v2 = top_ref.at[2 * hh], top_ref.at[2 * hh + 1]
                tau = taus[hh]
                best = v1[0:1, :] + v2[0:1, :]
                zsum = None
                for ti, dup in enumerate(repeated):
                    tile = cands[hh][ti * sub:(ti + 1) * sub, :]
                    keep = tile >= tau
                    if dup is not None:
                        keep = keep & ~dup
                    part = jnp.sum(jnp.where(keep, jnp.exp(tile - best), 0.0), axis=0, keepdims=True)
                    zsum = part if zsum is None else zsum + part
                inv_z = 1.0 / zsum
                v2_all = v2[...]
                e2_top = jnp.exp(v2_all - v2[0:1, :]) * inv_z
                s1 = s_ref[2 * hh]
                theta = jnp.full(s1.shape, jnp.inf, F32)
                for r in range(k):
                    c = v1[r:r + 1, :] + v2_all
                    theta_r = jnp.min(jnp.where(c >= tau, e2_top, jnp.inf), axis=0, keepdims=True)
                    theta = jnp.where(s1 == v1[r:r + 1, :], theta_r, theta)
                g1_ref[rows, cols] = theta
                g1_ref[rows_e, cols] = jnp.exp(s1 - v1[0:1, :])
                g2_ref[tc, rows, :] = jnp.exp(s_ref[2 * hh + 1] - v2[0:1, :]) * inv_z


def peer_scores(xn, wq, subkeys, tm=256):
    t, d = xn.shape
    tm = min(tm, t)
    assert t % tm == 0 and tm % LANES == 0
    rows = 2 * PEER_HEADS * N_KEYS
    return pl.pallas_call(
        _peer_scores_kernel,
        grid=(t // tm,),
        in_specs=[pl.BlockSpec((tm, d), lambda i: (i, 0)),
                  pl.BlockSpec((d, rows), lambda i: (0, 0), pipeline_mode=pl.Buffered(1)),
                  pl.BlockSpec((2 * PEER_HEADS, N_KEYS, N_KEYS), lambda i: (0, 0, 0))],
        out_specs=[pl.BlockSpec((rows, tm), lambda i: (0, i)),
                   pl.BlockSpec((tm // LANES, rows // 2, LANES), lambda i: (i, 0, 0))],
        out_shape=[jax.ShapeDtypeStruct((rows, t), F32),
                   jax.ShapeDtypeStruct((t // LANES, rows // 2, LANES), F32)],
        scratch_shapes=[pltpu.VMEM((2 * PEER_LOCKSTEP_HEADS, PEER_TOPK, LANES), F32),
                        pltpu.VMEM((2 * PEER_LOCKSTEP_HEADS, N_KEYS, LANES), F32)],
        compiler_params=_params("parallel"),
        name="peer_scores",
    )(xn, wq, subkeys)


GELU_C0 = math.sqrt(2.0 / math.pi)
GELU_C1 = GELU_C0 * 0.044715


def _gelu_tanh(x):
    return x * (0.5 + 0.5 * jnp.tanh(x * (GELU_C0 + GELU_C1 * (x * x))))


def _peer_experts_kernel(xn_ref, g1_ref, g2_ref, u_ref, vt_ref, x1_ref, g_ref, y_ref, acc_ref, a_ref, wa_ref):
    j = pl.program_id(1)
    nblk = pl.num_programs(1) - 2
    _, nchunk, eb, _ = a_ref.shape
    tm = nchunk * LANES
    a_even, a_odd = a_ref.at[0], a_ref.at[1]
    wa_even, wa_odd = wa_ref.at[0], wa_ref.at[1]
    nh = PEER_HEADS * N_KEYS
    per = eb // N_KEYS
    half = tm // 2

    mrows = 256
    arows = 256

    def activations(a_ref, m, c):
        rows = slice(m * arows, (m + 1) * arows)
        cols = slice(c * half, (c + 1) * half)
        a = lax.dot_general(u_ref[rows, :], xn_ref[cols, :], (((1,), (1,)), ((), ())),
                            preferred_element_type=F32)
        for k in range(half // LANES):
            a_ref[c * (half // LANES) + k, rows, :] = a[:, k * LANES:(k + 1) * LANES]

    def gate_rows(block, ii):
        key1 = block * per + ii
        return ([g1_ref[pl.ds(h * N_KEYS + key1, 1), :] for h in range(PEER_HEADS)],
                [g1_ref[pl.ds(nh + h * N_KEYS + key1, 1), :] for h in range(PEER_HEADS)])

    def gates(a_ref, wa_ref, theta_rows, e1_rows, ii, tc):
        rows = slice(ii * N_KEYS, (ii + 1) * N_KEYS)
        cols = slice(tc * LANES, (tc + 1) * LANES)
        wsum = None
        for h in range(PEER_HEADS):
            e2 = g2_ref[tc, h * N_KEYS:(h + 1) * N_KEYS, :]
            term = e1_rows[h][:, cols] * jnp.where(e2 >= theta_rows[h][:, cols], e2, 0.0)
            wsum = term if wsum is None else wsum + term
        act = _gelu_tanh(a_ref[tc, rows, :] * (1.0 / PEER_U_SCALE))
        wa_ref[rows, cols] = (act * (wsum * PEER_W_SCALE)).astype(wa_ref.dtype)

    def outputs(wa_ref, r, c):
        rows = slice(r * mrows, (r + 1) * mrows)
        cols = slice(c * half, (c + 1) * half)
        acc_ref[rows, cols] += jnp.dot(vt_ref[rows, :], wa_ref[:, cols], preferred_element_type=F32)

    def step(act, gate, out):
        vec, mxu_a, mxu_o = [], [], []
        rows_cache = {}

        def gate_piece(ii, tc):
            if ii not in rows_cache:
                rows_cache[ii] = gate_rows(j - 1, ii)
            gates(*gate, *rows_cache[ii], ii, tc)

        if gate:
            vec = [functools.partial(gate_piece, ii, tc) for ii in range(per) for tc in range(nchunk)]
        if act is not None:
            mxu_a = [functools.partial(activations, act, m, c) for m in range(eb // arows) for c in range(2)]
        if out is not None:
            mxu_o = [functools.partial(outputs, out, r, c) for c in range(2) for r in range(acc_ref.shape[0] // mrows)]
        n = max(len(vec), len(mxu_a), len(mxu_o))
        for k in range(n):
            for stage in (vec, mxu_a, mxu_o):
                for piece in stage[k * len(stage) // n:(k + 1) * len(stage) // n]:
                    piece()

    steady = (j > 1) & (j < nblk)

    @pl.when(j == 0)
    def _():
        acc_ref[...] = jnp.zeros_like(acc_ref)
        step(a_even, None, None)

    @pl.when(j == 1)
    def _():
        step(a_odd, (a_even, wa_even), None)

    @pl.when(steady & (j % 2 == 0))
    def _():
        step(a_even, (a_odd, wa_odd), wa_even)

    @pl.when(steady & (j % 2 == 1))
    def _():
        step(a_odd, (a_even, wa_even), wa_odd)

    @pl.when(j == nblk)
    def _():
        step(None, (a_odd, wa_odd), wa_even)

    @pl.when(j == nblk + 1)
    def _():
        step(None, None, wa_odd)
        x2 = x1_ref[...] + acc_ref[...].T * (1.0 / PEER_W_SCALE)
        ms = jnp.mean(x2 * x2, axis=-1, keepdims=True)
        y_ref[...] = x2 * lax.rsqrt(ms + EPS) * g_ref[...]


def peer_experts(xn, g1, g2, u, vt, x1, final_g, tm=512):
    t, d = xn.shape
    eb = vt.shape[2]
    tm = min(tm, t)
    assert t % tm == 0 and N_EXPERTS % (2 * eb) == 0 and eb % N_KEYS == 0 and tm % (2 * LANES) == 0
    rows = 2 * PEER_HEADS * N_KEYS
    nblk = N_EXPERTS // eb
    return pl.pallas_call(
        _peer_experts_kernel,
        grid=(t // tm, nblk + 2),
        in_specs=[pl.BlockSpec((tm, d), lambda i, j: (i, 0)),
                  pl.BlockSpec((rows, tm), lambda i, j: (0, i)),
                  pl.BlockSpec((tm // LANES, rows // 2, LANES), lambda i, j: (i, 0, 0)),
                  pl.BlockSpec((eb, d), lambda i, j: (jnp.minimum(j, nblk - 1), 0)),
                  pl.BlockSpec((None, d, eb), lambda i, j: (jnp.clip(j - 2, 0, nblk - 1), 0, 0)),
                  pl.BlockSpec((tm, d), lambda i, j: (i, 0), pipeline_mode=pl.Buffered(1)),
                  pl.BlockSpec((1, d), lambda i, j: (0, 0))],
        out_specs=pl.BlockSpec((tm, d), lambda i, j: (i, 0)),
        out_shape=jax.ShapeDtypeStruct((t, d), F32),
        scratch_shapes=[pltpu.VMEM((d, tm), F32), pltpu.VMEM((2, tm // LANES, eb, LANES), F32),
                        pltpu.VMEM((2, eb, tm), F8)],
        compiler_params=_params("parallel", "arbitrary"),
        name="peer_experts",
    )(xn, g1, g2, u, vt, x1, final_g.reshape(1, d).astype(F32))


def _encoder(x, p):
    b, seq, d = x.shape
    t = b * seq
    xt = x.reshape(t, d)
    xn1 = rms_norm_bf16(xt, p["norm1_g"])
    hproj = project(xn1, p["w_in"], 0, HYENA_COLS, BF16)
    glog = project(xn1, p["w_in"], HYENA_COLS + ATT_COLS, GATE_COLS, BF16)
    ya = hyena_branch(hproj.reshape(b, seq, HYENA_COLS), p["conv_w"], p["conv_b"], p["filt_w1"], p["filt_b1"],
                      p["filt_w2"], p["filt_b2"], p["filt_w3"], p["filt_b3"], p["filt_w4"], p["filt_freq"],
                      p["hyena_bias"])
    att = [dilated_attention(project_classes(xn1, p["w_in"], g), b, g) for g in range(N_GROUPS)]
    x1, xn2, xn2_f8 = merge(xt, glog, ya.reshape(t, HYENA_WIDTH), [o for o, _ in att], [l for _, l in att],
                            p["w_branch_a"], p["w_branch_b"], p["w_out"], p["norm2_g"])
    g1, g2 = peer_scores(xn2, p["peer_wq"], p["peer_subkeys"])
    y = peer_experts(xn2_f8, g1, g2, p["peer_u"], p["peer_vt"], x1, p["final_g"])
    return y.reshape(b, seq, d)


def kernel(x_prompt, x_sample, norm1_g, w_in, conv_w, conv_b, filt_w1, filt_b1, filt_w2, filt_b2, filt_w3,
           filt_b3, filt_w4, filt_freq, hyena_bias, w_branch_a, w_branch_b, w_out, norm2_g, peer_wq,
           peer_subkeys, peer_u, peer_v, final_g):
    assert norm1_g.shape[0] == 1, "single-layer encoder"
    p = dict(
        norm1_g=norm1_g[0].astype(F32), w_in=w_in[0].astype(BF16), conv_w=conv_w[0], conv_b=conv_b[0],
        filt_w1=filt_w1[0], filt_b1=filt_b1[0], filt_w2=filt_w2[0], filt_b2=filt_b2[0], filt_w3=filt_w3[0],
        filt_b3=filt_b3[0], filt_w4=filt_w4[0], filt_freq=filt_freq[0], hyena_bias=hyena_bias[0],
        w_branch_a=w_branch_a[0].astype(BF16), w_branch_b=w_branch_b[0].astype(BF16),
        w_out=w_out[0].astype(BF16), norm2_g=norm2_g[0], peer_wq=peer_wq[0].astype(BF16),
        peer_subkeys=peer_subkeys[0].reshape(2 * PEER_HEADS, N_KEYS, N_KEYS).astype(BF16),
        peer_u=(peer_u[0] * PEER_U_SCALE).astype(F8),
        peer_vt=peer_v[0].astype(F8).reshape(N_EXPERTS // PEER_EB, PEER_EB, D_MODEL).transpose(0, 2, 1),
        final_g=final_g,
    )
    return (_encoder(x_prompt, p), _encoder(x_sample, p))
```

```python
import functools
import math

import jax
import jax.numpy as jnp
from jax import lax
from jax.experimental import pallas as pl
from jax.experimental.pallas import tpu as pltpu

F32 = jnp.float32
BF16 = jnp.bfloat16
F8 = jnp.float8_e4m3fn
PEER_U_SCALE = 32.0
PEER_W_SCALE = 16.0

D_MODEL = 2048
HYENA_WIDTH = 1024
FILTER_EMB = 33
FILTER_EMB_PAD = 128
FILTER_HIDDEN = 64
DECAY_TARGET = 1e-2
FAST_DECAY_PCT = 0.3
SLOW_DECAY_PCT = 1.5
ATT_GROUPS = ((128, 1), (512, 4), (2048, 16))
N_GROUPS = 3
HEADS_PER_GROUP = 8
HEAD_DIM = 64
N_ATT_HEADS = N_GROUPS * HEADS_PER_GROUP
ATT_WIDTH = HEADS_PER_GROUP * HEAD_DIM
HYENA_COLS = 3 * HYENA_WIDTH
ATT_COLS = 3 * N_GROUPS * ATT_WIDTH
GATE_COLS = 2 * D_MODEL
PEER_HEADS = 8
N_KEYS = 128
N_EXPERTS = N_KEYS * N_KEYS
PEER_TOPK = 16
PEER_EB = 512
EPS = 1e-6
NEG_INF = -1e30
ATT_HALF = 64
LANES = 128
VMEM_LIMIT = 56 * 1024 * 1024
HIGHEST = lax.Precision.HIGHEST


def _params(*sem, flags=None):
    return pltpu.CompilerParams(dimension_semantics=sem, vmem_limit_bytes=VMEM_LIMIT, flags=flags)


def _rms_norm_kernel(x_ref, g_ref, o_ref):
    x = x_ref[...]
    ms = jnp.mean(x * x, axis=-1, keepdims=True)
    o_ref[...] = (x * lax.rsqrt(ms + EPS) * g_ref[...]).astype(o_ref.dtype)


def rms_norm_bf16(x, g, tm=512):
    t, d = x.shape
    tm = min(tm, t)
    assert t % tm == 0
    return pl.pallas_call(
        _rms_norm_kernel,
        grid=(t // tm,),
        in_specs=[pl.BlockSpec((tm, d), lambda i: (i, 0)), pl.BlockSpec((1, d), lambda i: (0, 0))],
        out_specs=pl.BlockSpec((tm, d), lambda i: (i, 0)),
        out_shape=jax.ShapeDtypeStruct((t, d), BF16),
        compiler_params=_params("parallel"),
        name="rms_norm",
    )(x, g.reshape(1, d))


def _project_kernel(x_ref, w_ref, o_ref):
    o_ref[...] = jnp.dot(x_ref[...], w_ref[...], preferred_element_type=F32).astype(o_ref.dtype)


def project(xn, w, col0, ncols, out_dtype, tm=2048, tn=512):
    t, d = xn.shape
    tm = min(tm, t)
    assert t % tm == 0 and ncols % tn == 0 and col0 % tn == 0
    cb0 = col0 // tn
    return pl.pallas_call(
        _project_kernel,
        grid=(t // tm, ncols // tn),
        in_specs=[
            pl.BlockSpec((tm, d), lambda i, j: (i, 0)),
            pl.BlockSpec((d, tn), lambda i, j: (0, cb0 + j)),
        ],
        out_specs=pl.BlockSpec((tm, tn), lambda i, j: (i, j)),
        out_shape=jax.ShapeDtypeStruct((t, ncols), out_dtype),
        compiler_params=_params("parallel", "arbitrary"),
        name="project",
    )(xn, w)


def _project_classes_kernel(x_ref, w_ref, o_ref, acc_ref, *, dilation):
    which = pl.program_id(1)
    acc = jnp.dot(x_ref[...], w_ref[...], preferred_element_type=F32)
    nlane, tm, _ = acc_ref.shape
    for k in range(nlane):
        acc_ref[k] = acc[:, k * LANES:(k + 1) * LANES]
    rows = tm // dilation
    for part in range(3):
        @pl.when(which == part)
        def _(part=part):
            for r in range(dilation):
                for k in range(nlane):
                    piece = acc_ref[k] if dilation == 1 else acc_ref[k, pl.ds(r, rows, stride=dilation), :]
                    c0 = (3 * r + part) * nlane * LANES + k * LANES
                    o_ref[:, c0:c0 + LANES] = piece.astype(o_ref.dtype)


def project_classes(xn, w, group, tm=2048):
    t, d = xn.shape
    dilation = ATT_GROUPS[group][1]
    tm = min(tm, t)
    assert t % tm == 0 and tm % (8 * dilation) == 0
    cb0 = HYENA_COLS // ATT_WIDTH + group
    return pl.pallas_call(
        functools.partial(_project_classes_kernel, dilation=dilation),
        grid=(t // tm, 3),
        in_specs=[
            pl.BlockSpec((tm, d), lambda i, j: (i, 0)),
            pl.BlockSpec((d, ATT_WIDTH), lambda i, j: (0, cb0 + N_GROUPS * j)),
        ],
        out_specs=pl.BlockSpec((tm // dilation, dilation * 3 * ATT_WIDTH), lambda i, j: (i, 0)),
        out_shape=jax.ShapeDtypeStruct((t // dilation, dilation * 3 * ATT_WIDTH), BF16),
        scratch_shapes=[pltpu.VMEM((ATT_WIDTH // LANES, tm, LANES), F32)],
        compiler_params=_params("parallel", "arbitrary"),
        name="project_classes",
    )(xn, w)


def _filter_mlp_kernel(z_ref, t_ref, keep_ref, w1_ref, b1_ref, w2_ref, b2_ref, w3_ref, b3_ref, w4_ref, fr_ref,
                       dl_ref, h_ref, l1_ref):
    fr = fr_ref[...]
    h = jnp.sin(fr * (jnp.dot(z_ref[...], w1_ref[...], precision=HIGHEST, preferred_element_type=F32)
                      + b1_ref[...]))
    h = jnp.sin(fr * (jnp.dot(h, w2_ref[...], precision=HIGHEST, preferred_element_type=F32) + b2_ref[...]))
    h = jnp.sin(fr * (jnp.dot(h, w3_ref[...], precision=HIGHEST, preferred_element_type=F32) + b3_ref[...]))
    h = jnp.dot(h, w4_ref[...], precision=HIGHEST, preferred_element_type=F32)
    h = h * jnp.exp(-t_ref[...] * dl_ref[...])
    h_ref[...] = (h * keep_ref[...]).astype(h_ref.dtype)

    @pl.when(pl.program_id(0) == 0)
    def _():
        l1_ref[...] = jnp.zeros_like(l1_ref)

    l1_ref[...] += jnp.sum(jnp.abs(h), axis=0, keepdims=True)


def filter_mlp(seq, w1, b1, w2, b2, w3, b3, w4, freq, tl=512):
    t = jnp.linspace(0.0, 1.0, seq, dtype=F32)[:, None]
    bands = (FILTER_EMB - 1) // 2
    ang = 2.0 * math.pi * jnp.arange(seq, dtype=F32)[:, None] / seq
    fb = jnp.linspace(1e-4, bands - 1, bands, dtype=F32)[None, :]
    z = jnp.concatenate([t, jnp.cos(fb * ang), -jnp.sin(fb * ang)], axis=-1)
    z = jnp.pad(z, ((0, 0), (0, FILTER_EMB_PAD - FILTER_EMB)))
    by_row = lambda v: jnp.concatenate([v, v[0:1], v[1:][::-1]], axis=0)
    keep = (jnp.arange(2 * seq) != seq).astype(F32)[:, None]
    w1p = jnp.pad(w1.astype(F32), ((0, FILTER_EMB_PAD - FILTER_EMB), (0, 0)))
    min_decay = math.log(DECAY_TARGET) / SLOW_DECAY_PCT
    max_decay = math.log(DECAY_TARGET) / FAST_DECAY_PCT
    deltas = jnp.abs(jnp.linspace(min_decay, max_decay, HYENA_WIDTH, dtype=F32))[None, :]
    tl = min(tl, seq)
    assert seq % tl == 0
    c = HYENA_WIDTH
    hid = FILTER_HIDDEN
    per_dir = seq // tl
    full = lambda r, cc: pl.BlockSpec((r, cc), lambda i: (0, 0))
    return pl.pallas_call(
        _filter_mlp_kernel,
        grid=(2 * per_dir,),
        in_specs=[
            pl.BlockSpec((tl, FILTER_EMB_PAD), lambda i: (i, 0)),
            pl.BlockSpec((tl, 1), lambda i: (i, 0)),
            pl.BlockSpec((tl, 1), lambda i: (i, 0)),
            full(FILTER_EMB_PAD, hid), full(1, hid), full(hid, hid), full(1, hid), full(hid, hid),
            full(1, hid), pl.BlockSpec((hid, c), lambda i: (0, i // per_dir)), full(1, hid), full(1, c),
        ],
        out_specs=[pl.BlockSpec((tl, c), lambda i: (i, 0)), pl.BlockSpec((1, c), lambda i: (0, 0))],
        out_shape=[jax.ShapeDtypeStruct((2 * seq, c), BF16), jax.ShapeDtypeStruct((1, c), F32)],
        compiler_params=_params("arbitrary"),
        name="filter_mlp",
    )(by_row(z), by_row(t), keep, w1p, b1.reshape(1, hid).astype(F32), w2.astype(F32), b2.reshape(1, hid).astype(F32),
      w3.astype(F32), b3.reshape(1, hid).astype(F32), w4.astype(F32), freq.reshape(1, hid).astype(F32), deltas)


def _fft_split(n):
    lg = int(math.log2(n))
    assert 1 << lg == n
    n1 = 1 << ((lg + 1) // 2)
    return n1, n // n1


def _slow_dft_tables(n1):
    a = jnp.arange(n1, dtype=jnp.int32)
    ang = ((a[:, None] * a[None, :]) % n1).astype(F32) * (2.0 * math.pi / n1)
    return jnp.cos(ang), jnp.sin(ang)


def _fast_dft_tables(n1, n2):
    n = n1 * n2
    k1 = jnp.arange(n1, dtype=jnp.int32)[:, None, None]
    k2 = jnp.arange(n2, dtype=jnp.int32)[None, :, None]
    f = jnp.arange(n2, dtype=jnp.int32)[None, None, :]
    ang = ((f * (k1 + n1 * k2)) % n).astype(F32) * (2.0 * math.pi / n)
    c, s = jnp.cos(ang), jnp.sin(ang)
    top = jnp.concatenate([c, s], axis=2)
    bot = jnp.concatenate([-s, c], axis=2)
    return jnp.concatenate([top, bot], axis=1)


def _left_matmul_kernel(f_ref, x_ref, o_ref):
    o = jnp.dot(f_ref[...], x_ref[...].astype(f_ref.dtype), preferred_element_type=F32)
    o_ref[...] = o.astype(o_ref.dtype)


def left_matmul(f, x, out_dtype, ln=4096):
    b, k, lanes = x.shape
    m = f.shape[0]
    ln = min(ln, lanes)
    assert lanes % ln == 0 and f.shape[1] == k
    return pl.pallas_call(
        _left_matmul_kernel,
        grid=(b, lanes // ln),
        in_specs=[pl.BlockSpec((m, k), lambda i, j: (0, 0)),
                  pl.BlockSpec((None, k, ln), lambda i, j: (i, 0, j))],
        out_specs=pl.BlockSpec((None, m, ln), lambda i, j: (i, 0, j)),
        out_shape=jax.ShapeDtypeStruct((b, m, lanes), out_dtype),
        compiler_params=_params("parallel", "parallel"),
        name="left_matmul",
    )(f, x)


def _spectrum_mid_kernel(g_ref, a_ref, sc_ref, o_ref):
    n2 = a_ref.shape[1]
    a = a_ref[...].reshape(2 * n2, a_ref.shape[2])
    x = jnp.dot(g_ref[...], a, preferred_element_type=F32) * sc_ref[...]
    o_ref[...] = x.reshape(o_ref.shape)


def spectrum_mid(g, a, scale, cb=512):
    _, n1, n2, c = a.shape
    cb = min(cb, c)
    return pl.pallas_call(
        _spectrum_mid_kernel,
        grid=(n1, c // cb),
        in_specs=[pl.BlockSpec((None, 2 * n2, 2 * n2), lambda i, j: (i, 0, 0)),
                  pl.BlockSpec((2, None, n2, cb), lambda i, j: (0, i, 0, j)),
                  pl.BlockSpec((1, cb), lambda i, j: (0, j))],
        out_specs=pl.BlockSpec((2, None, n2, cb), lambda i, j: (0, i, 0, j)),
        out_shape=jax.ShapeDtypeStruct(a.shape, F32),
        compiler_params=_params("parallel", "parallel"),
        name="spectrum_mid",
    )(g, a, scale)


def _conv_mid_kernel(g_ref, gi_ref, a_ref, k_ref, o_ref):
    bblk, _, n2, cb = a_ref.shape
    kr, ki = k_ref[0], k_ref[1]
    for bi in range(bblk):
        a = a_ref[bi].reshape(2 * n2, cb)
        x = jnp.dot(g_ref[...], a, preferred_element_type=F32)
        xr, xi = x[:n2], x[n2:]
        y = jnp.concatenate([xr * kr - xi * ki, xr * ki + xi * kr], axis=0).astype(BF16)
        z = jnp.dot(gi_ref[...], y, preferred_element_type=F32)
        o_ref[bi] = z.reshape(2, n2, cb).astype(o_ref.dtype)


def conv_mid(g, gi, a, kf, block_bytes=2 << 20):
    b, _, n1, n2, c = a.shape
    bblk = max(1, min(b, block_bytes // (2 * n2 * c * 2)))
    assert b % bblk == 0
    return pl.pallas_call(
        _conv_mid_kernel,
        grid=(n1, b // bblk),
        in_specs=[pl.BlockSpec((None, 2 * n2, 2 * n2), lambda i, bb: (i, 0, 0)),
                  pl.BlockSpec((None, 2 * n2, 2 * n2), lambda i, bb: (i, 0, 0)),
                  pl.BlockSpec((bblk, 2, None, n2, c), lambda i, bb: (bb, 0, i, 0, 0)),
                  pl.BlockSpec((2, None, n2, c), lambda i, bb: (0, i, 0, 0))],
        out_specs=pl.BlockSpec((bblk, 2, None, n2, c), lambda i, bb: (bb, 0, i, 0, 0)),
        out_shape=jax.ShapeDtypeStruct(a.shape, BF16),
        compiler_params=_params("parallel", "parallel"),
        name="conv_mid",
    )(g, gi, a, kf)


def _conv_out_kernel(f_ref, z_ref, w_ref, x0_ref, bias_ref, o_ref):
    y = jnp.dot(f_ref[...], z_ref[...], preferred_element_type=F32)
    o_ref[...] = x0_ref[...] * (y + w_ref[...] * bias_ref[...])


def conv_out(f, z, w, x0, bias_l, ln=4096):
    b, k, lanes = z.shape
    m = f.shape[0]
    ln = min(ln, lanes)
    assert lanes % ln == 0 and bias_l.shape == (1, ln)
    return pl.pallas_call(
        _conv_out_kernel,
        grid=(b, lanes // ln),
        in_specs=[pl.BlockSpec((m, k), lambda i, j: (0, 0)),
                  pl.BlockSpec((None, k, ln), lambda i, j: (i, 0, j)),
                  pl.BlockSpec((None, m, ln), lambda i, j: (i, 0, j)),
                  pl.BlockSpec((None, m, ln), lambda i, j: (i, 0, j)),
                  pl.BlockSpec((1, ln), lambda i, j: (0, 0))],
        out_specs=pl.BlockSpec((None, m, ln), lambda i, j: (i, 0, j)),
        out_shape=jax.ShapeDtypeStruct((b, m, lanes), F32),
        compiler_params=_params("parallel", "parallel"),
        name="conv_out",
    )(f, z, w, x0, bias_l)


def _hyena_pre_kernel(x0_ref, x1_ref, v_ref, w0_ref, w1_ref, wv_ref, b0_ref, b1_ref, bv_ref, x0o_ref, wo_ref):
    seq = x0_ref.shape[0]
    row = lax.broadcasted_iota(jnp.int32, x0_ref.shape, 0)
    first, last = row == 0, row == seq - 1

    def conv(u_ref, w_ref, b_ref):
        u = u_ref[...].astype(F32)
        prev = jnp.where(first, 0.0, pltpu.roll(u, 1, axis=0))
        nxt = jnp.where(last, 0.0, pltpu.roll(u, seq - 1, axis=0))
        return b_ref[...] + prev * w_ref[0:1, :] + u * w_ref[1:2, :] + nxt * w_ref[2:3, :]

    x0o_ref[...] = conv(x0_ref, w0_ref, b0_ref)
    wo_ref[...] = conv(v_ref, wv_ref, bv_ref) * conv(x1_ref, w1_ref, b1_ref)


def hyena_pre(hproj, conv_w, conv_b):
    b, seq, _ = hproj.shape
    c = HYENA_WIDTH
    cb = max(LANES, min(c, (1 << 20) // seq))
    nb = c // cb
    u_spec = lambda g: pl.BlockSpec((None, seq, cb), lambda i, j: (i, 0, g * nb + j))
    w_spec = lambda g: pl.BlockSpec((3, cb), lambda i, j: (0, g * nb + j))
    b_spec = lambda g: pl.BlockSpec((1, cb), lambda i, j: (0, g * nb + j))
    o_spec = pl.BlockSpec((None, seq, cb), lambda i, j: (i, 0, j))
    cw = conv_w.astype(F32)
    cbias = conv_b.reshape(1, 3 * c).astype(F32)
    return pl.pallas_call(
        _hyena_pre_kernel,
        grid=(b, nb),
        in_specs=[u_spec(0), u_spec(1), u_spec(2), w_spec(0), w_spec(1), w_spec(2),
                  b_spec(0), b_spec(1), b_spec(2)],
        out_specs=[o_spec, o_spec],
        out_shape=[jax.ShapeDtypeStruct((b, seq, c), F32)] * 2,
        compiler_params=_params("parallel", "parallel"),
        name="hyena_pre",
    )(hproj, hproj, hproj, cw, cw, cw, cbias, cbias, cbias)


def hyena_branch(hproj, conv_w, conv_b, fw1, fb1, fw2, fb2, fw3, fb3, fw4, freq, hyena_bias):
    b, seq, _ = hproj.shape
    c = HYENA_WIDTH
    n = 2 * seq
    n1, n2 = _fft_split(n)
    n1h = n1 // 2
    cos1, sin1 = _slow_dft_tables(n1)
    g = _fast_dft_tables(n1, n2)
    gi = jnp.swapaxes(g, 1, 2)

    k_full, l1 = filter_mlp(seq, fw1, fb1, fw2, fb2, fw3, fb3, fw4, freq)
    scale = 1.0 / ((l1 + EPS) * n)
    fa_full = jnp.concatenate([cos1, -sin1], axis=0)
    fa_full = fa_full.astype(BF16)
    g, gi = g.astype(BF16), gi.astype(BF16)
    ka = left_matmul(fa_full, k_full.reshape(1, n1, n2 * c), BF16)
    kf = spectrum_mid(g, ka.reshape(2, n1, n2, c), scale)

    x0c, w = hyena_pre(hproj, conv_w, conv_b)
    a = left_matmul(fa_full[:, :n1h], w.reshape(b, n1h, n2 * c), BF16)
    z = conv_mid(g, gi, a.reshape(b, 2, n1, n2, c), kf)
    fc = jnp.concatenate([cos1[:n1h], -sin1[:n1h]], axis=1).astype(BF16)
    ln = min(4096, n2 * c)
    bias_l = jnp.tile(hyena_bias.reshape(1, c).astype(F32), (1, ln // c))
    y = conv_out(fc, z.reshape(b, 2 * n1, n2 * c), w.reshape(b, n1h, n2 * c), x0c.reshape(b, n1h, n2 * c),
                 bias_l, ln=ln)
    return y.reshape(b, seq, c)


ATT_SUB = 128


def _dilated_attn_kernel(q_ref, kp_ref, kc_ref, kn_ref, vp_ref, vc_ref, vn_ref, o_ref, l_ref, *,
                         n_cls, dilation, slopes):
    tq = q_ref.shape[0]
    q0 = pl.program_id(1) * tq
    res = pl.program_id(2)
    sk = ATT_SUB + 2 * ATT_HALF
    scale = 1.0 / math.sqrt(HEAD_DIM)
    qi = lax.broadcasted_iota(jnp.int32, (ATT_SUB, sk), 0)
    kj = lax.broadcasted_iota(jnp.int32, (ATT_SUB, sk), 1) - ATT_HALF
    dist = jnp.abs(kj - qi)
    band = dist <= ATT_HALF
    adist = (dilation * dist).astype(F32)
    lane = lax.broadcasted_iota(jnp.int32, (sk, LANES), 1)
    low_k = lane < HEAD_DIM
    low_q = lax.broadcasted_iota(jnp.int32, (ATT_SUB, LANES), 1) < HEAD_DIM
    for sub in range(tq // ATT_SUB):
        r0 = sub * ATT_SUB
        kabs = q0 + r0 + kj
        mask = band & (kabs >= 0) & (kabs < n_cls)
        for pair in range(HEADS_PER_GROUP // 2):
            cs = slice(pair * LANES, (pair + 1) * LANES)
            q = q_ref[r0:r0 + ATT_SUB, cs]

            def keys(p_ref, c_ref, n_ref):
                parts = []
                if r0 == 0:
                    parts.append(p_ref[:, cs])
                    parts.append(c_ref[0:min(tq, ATT_SUB + ATT_HALF), cs])
                else:
                    parts.append(c_ref[r0 - ATT_HALF:min(tq, r0 + ATT_SUB + ATT_HALF), cs])
                if r0 + ATT_SUB + ATT_HALF > tq:
                    parts.append(n_ref[:, cs])
                return jnp.concatenate(parts, axis=0)

            k = keys(kp_ref, kc_ref, kn_ref)
            v = keys(vp_ref, vc_ref, vn_ref)
            acc = None
            ls, lses = [], []
            for e in range(2):
                sel = low_k if e == 0 else ~low_k
                ke = jnp.where(sel, k, jnp.zeros_like(k))
                ve = jnp.where(sel, v, jnp.zeros_like(v))
                s = lax.dot_general(q, ke, (((1,), (1,)), ((), ())), preferred_element_type=F32) * scale
                s = jnp.where(mask, s - slopes[2 * pair + e] * adist, NEG_INF)
                m = jnp.max(s, axis=-1, keepdims=True)
                p = jnp.exp(s - m)
                l = jnp.sum(p, axis=-1, keepdims=True)
                pv = jnp.dot(p.astype(BF16), ve, preferred_element_type=F32)
                acc = pv if acc is None else acc + pv
                ls.append(l)
                lses.append(m + jnp.log(l))
            inv = 1.0 / jnp.where(low_q, ls[0], ls[1])
            if dilation == 1:
                tok = slice(r0, r0 + ATT_SUB)
            else:
                tok = pl.ds(r0 * dilation + res, ATT_SUB, stride=dilation)
            o_ref[pair, tok, :] = acc * inv
            l_ref[pair, tok, :] = jnp.where(low_q, lses[0], lses[1])


def dilated_attention(qkv, batch, group, out_block_bytes=4 << 20):
    window, dilation = ATT_GROUPS[group]
    assert window // (2 * dilation) == ATT_HALF
    n_cls = qkv.shape[0] // batch
    seq = n_cls * dilation
    tq = min(256, n_cls, max(ATT_SUB, out_block_bytes // (dilation * ATT_WIDTH * 4)))
    assert n_cls % tq == 0 and tq % ATT_SUB == 0
    hb = tq // ATT_HALF
    n_halo = n_cls // ATT_HALF
    slopes = tuple(2.0 ** (-8.0 * (group * HEADS_PER_GROUP + h + 1.0) / N_ATT_HEADS)
                   for h in range(HEADS_PER_GROUP))
    x = qkv.reshape(batch, n_cls, dilation * 3 * ATT_WIDTH)
    cur = lambda which: pl.BlockSpec((None, tq, ATT_WIDTH), lambda i, j, r: (i, j, 3 * r + which))
    prev = lambda which: pl.BlockSpec((None, ATT_HALF, ATT_WIDTH),
                                      lambda i, j, r: (i, jnp.maximum(j * hb - 1, 0), 3 * r + which))
    nxt = lambda which: pl.BlockSpec((None, ATT_HALF, ATT_WIDTH),
                                     lambda i, j, r: (i, jnp.minimum((j + 1) * hb, n_halo - 1), 3 * r + which))
    npair = ATT_WIDTH // LANES
    o_spec = pl.BlockSpec((None, npair, tq * dilation, LANES), lambda i, j, r: (i, 0, j, 0))
    return pl.pallas_call(
        functools.partial(_dilated_attn_kernel, n_cls=n_cls, dilation=dilation, slopes=slopes),
        grid=(batch, n_cls // tq, dilation),
        in_specs=[cur(0), prev(1), cur(1), nxt(1), prev(2), cur(2), nxt(2)],
        out_specs=[o_spec, o_spec],
        out_shape=[jax.ShapeDtypeStruct((batch, npair, seq, LANES), F32)] * 2,
        compiler_params=_params("parallel", "parallel", "arbitrary"),
        name="dilated_attn",
    )(x, x, x, x, x, x, x)


def _merge_kernel(x_ref, ga_ref, gb_ref, ya_ref, o0_ref, o1_ref, o2_ref, l0_ref, l1_ref, l2_ref,
                  wa_ref, wb_ref, wo_ref, g2_ref, x1_ref, xn_ref, xn8_ref):
    parts = []
    for pair in range(o0_ref.shape[0]):
        l0, l1, l2 = l0_ref[pair], l1_ref[pair], l2_ref[pair]
        m = jnp.maximum(jnp.maximum(l0, l1), l2)
        e0, e1, e2 = jnp.exp(l0 - m), jnp.exp(l1 - m), jnp.exp(l2 - m)
        yb = (e0 * o0_ref[pair] + e1 * o1_ref[pair] + e2 * o2_ref[pair]) / (e0 + e1 + e2)
        parts.append(yb.astype(BF16))
    pa = jnp.dot(ya_ref[...].astype(BF16), wa_ref[...], preferred_element_type=F32)
    pb = jnp.dot(jnp.concatenate(parts, axis=1), wb_ref[...], preferred_element_type=F32)
    merged = (jax.nn.sigmoid(ga_ref[...].astype(F32)) * pa
              + jax.nn.sigmoid(gb_ref[...].astype(F32)) * pb)
    x1 = x_ref[...] + jnp.dot(merged.astype(BF16), wo_ref[...], preferred_element_type=F32)
    x1_ref[...] = x1
    ms = jnp.mean(x1 * x1, axis=-1, keepdims=True)
    xn = x1 * lax.rsqrt(ms + EPS) * g2_ref[...]
    xn_ref[...] = xn.astype(xn_ref.dtype)
    xn8_ref[...] = xn.astype(xn8_ref.dtype)


def merge(x, glog, ya, outs, lses, wa, wb, wo, g2, tm=256):
    t, d = x.shape
    _, npair, seq, _ = outs[0].shape
    tm = min(tm, seq)
    assert seq % tm == 0
    tiles = seq // tm
    row = lambda c, blk=0: pl.BlockSpec((tm, c), lambda i: (i, blk))
    const = lambda r, c: pl.BlockSpec((r, c), lambda i: (0, 0), pipeline_mode=pl.Buffered(1))
    att = pl.BlockSpec((None, npair, tm, LANES), lambda i: (i // tiles, 0, i % tiles, 0))
    aw = ATT_WIDTH
    return pl.pallas_call(
        _merge_kernel,
        grid=(t // tm,),
        in_specs=[row(d), row(d, 0), row(d, 1), row(HYENA_WIDTH), att, att, att, att, att, att,
                  const(HYENA_WIDTH, d), const(aw, d), const(d, d), const(1, d)],
        out_specs=[row(d), row(d), row(d)],
        out_shape=[jax.ShapeDtypeStruct((t, d), F32), jax.ShapeDtypeStruct((t, d), BF16),
                   jax.ShapeDtypeStruct((t, d), F8)],
        compiler_params=_params("parallel"),
        name="merge",
    )(x, glog, glog, ya, *outs, *lses, wa, wb, wo, g2.reshape(1, d).astype(F32))


def _descending_max(curs, k, emit):
    curs = list(curs)
    for it in range(k):
        for c, cur in enumerate(curs):
            m = jnp.max(cur, axis=0, keepdims=True)
            emit(c, it, m)
            if it + 1 < k:
                curs[c] = jnp.where(cur == m, -jnp.inf, cur)


PEER_LOCKSTEP_HEADS = 2


def _peer_scores_kernel(xn_ref, wq_ref, sk_ref, g1_ref, g2_ref, top_ref, s_ref):
    k = PEER_TOPK
    sub = 8
    nh = PEER_HEADS * N_KEYS
    q = jnp.dot(xn_ref[...], wq_ref[...], preferred_element_type=F32).astype(BF16)
    low_rank = lax.broadcasted_iota(jnp.int32, (sub, LANES), 0) < 4
    for tc in range(xn_ref.shape[0] // LANES):
        cols = slice(tc * LANES, (tc + 1) * LANES)
        for h0 in range(0, PEER_HEADS, PEER_LOCKSTEP_HEADS):
            heads = range(h0, h0 + PEER_LOCKSTEP_HEADS)
            chains = [(hh, p) for hh in range(PEER_LOCKSTEP_HEADS) for p in range(2)]
            for c, (hh, p) in enumerate(chains):
                hp = 2 * (h0 + hh) + p
                qs = q[cols, hp * N_KEYS:(hp + 1) * N_KEYS]
                s_ref[c] = lax.dot_general(sk_ref[hp], qs, (((1,), (1,)), ((), ())), preferred_element_type=F32)

            def put_top(c, it, m):
                top_ref[c, it:it + 1, :] = m

            _descending_max([s_ref[c] for c in range(len(chains))], k, put_top)

            cands = []
            for hh in range(PEER_LOCKSTEP_HEADS):
                v1, v2 = top_ref.at[2 * hh], top_ref.at[2 * hh + 1]
                tiles = [v1[r:r + 1, :] + v2[0:sub, :] for r in range(4)]
                tiles += [v1[0:sub, :] + v2[r:r + 1, :] for r in range(4)]
                tiles += [v1[sub:k, :] + v2[0:1, :], v1[0:1, :] + v2[sub:k, :]]
                cands.append(jnp.concatenate(tiles, axis=0))
            repeated = [None] * 4 + [low_rank] * 4 + [None] * 2
            taus = [None] * PEER_LOCKSTEP_HEADS

            def put_tau(c, it, m):
                taus[c] = m

            _descending_max(cands, k, put_tau)
            for hh, h in enumerate(heads):
                rows = slice(h * N_KEYS, (h + 1) * N_KEYS)
                rows_e = slice(nh + h * N_KEYS, nh + (h + 1) * N_KEYS)
                v1, v2 = top_ref.at[2 * hh], top_ref.at[2 * hh + 1]
                tau = taus[hh]
                best = v1[0:1, :] + v2[0:1, :]
                zsum = None
                for ti, dup in enumerate(repeated):
                    tile = cands[hh][ti * sub:(ti + 1) * sub, :]
                    keep = tile >= tau
                    if dup is not None:
                        keep = keep & ~dup
                    part = jnp.sum(jnp.where(keep, jnp.exp(tile - best), 0.0), axis=0, keepdims=True)
                    zsum = part if zsum is None else zsum + part
                inv_z = 1.0 / zsum
                v2_all = v2[...]
                e2_top = jnp.exp(v2_all - v2[0:1, :]) * inv_z
                s1 = s_ref[2 * hh]
                theta = jnp.full(s1.shape, jnp.inf, F32)
                for r in range(k):
                    c = v1[r:r + 1, :] + v2_all
                    theta_r = jnp.min(jnp.where(c >= tau, e2_top, jnp.inf), axis=0, keepdims=True)
                    theta = jnp.where(s1 == v1[r:r + 1, :], theta_r, theta)
                g1_ref[rows, cols] = theta
                g1_ref[rows_e, cols] = jnp.exp(s1 - v1[0:1, :])
                g2_ref[tc, rows, :] = jnp.exp(s_ref[2 * hh + 1] - v2[0:1, :]) * inv_z


def peer_scores(xn, wq, subkeys, tm=256):
    t, d = xn.shape
    tm = min(tm, t)
    assert t % tm == 0 and tm % LANES == 0
    rows = 2 * PEER_HEADS * N_KEYS
    return pl.pallas_call(
        _peer_scores_kernel,
        grid=(t // tm,),
        in_specs=[pl.BlockSpec((tm, d), lambda i: (i, 0)),
                  pl.BlockSpec((d, rows), lambda i: (0, 0), pipeline_mode=pl.Buffered(1)),
                  pl.BlockSpec((2 * PEER_HEADS, N_KEYS, N_KEYS), lambda i: (0, 0, 0))],
        out_specs=[pl.BlockSpec((rows, tm), lambda i: (0, i)),
                   pl.BlockSpec((tm // LANES, rows // 2, LANES), lambda i: (i, 0, 0))],
        out_shape=[jax.ShapeDtypeStruct((rows, t), F32),
                   jax.ShapeDtypeStruct((t // LANES, rows // 2, LANES), F32)],
        scratch_shapes=[pltpu.VMEM((2 * PEER_LOCKSTEP_HEADS, PEER_TOPK, LANES), F32),
                        pltpu.VMEM((2 * PEER_LOCKSTEP_HEADS, N_KEYS, LANES), F32)],
        compiler_params=_params("parallel"),
        name="peer_scores",
    )(xn, wq, subkeys)


GELU_C0 = math.sqrt(2.0 / math.pi)
GELU_C1 = GELU_C0 * 0.044715


def _peer_experts_kernel(xn_ref, g1_ref, g2_ref, u_ref, vt_ref, x1_ref, g_ref, y_ref, acc_ref, a_ref, wa_ref):
    j = pl.program_id(1)
    nblk = pl.num_programs(1) - 2
    _, nchunk, eb, _ = a_ref.shape
    tm = nchunk * LANES
    a_even, a_odd = a_ref.at[0], a_ref.at[1]
    wa_even, wa_odd = wa_ref.at[0], wa_ref.at[1]
    nh = PEER_HEADS * N_KEYS
    per = eb // N_KEYS
    half = tm // 2

    mrows = 256
    arows = 256

    def activations(a_ref, m, c):
        rows = slice(m * arows, (m + 1) * arows)
        cols = slice(c * half, (c + 1) * half)
        a = lax.dot_general(u_ref[rows, :], xn_ref[cols, :], (((1,), (1,)), ((), ())),
                            preferred_element_type=F32)
        for k in range(half // LANES):
            a_ref[c * (half // LANES) + k, rows, :] = a[:, k * LANES:(k + 1) * LANES]

    def gate_rows(block, ii):
        key1 = block * per + ii
        fold = 0.5 * PEER_W_SCALE / PEER_U_SCALE
        return ([g1_ref[pl.ds(h * N_KEYS + key1, 1), :] for h in range(PEER_HEADS)],
                [g1_ref[pl.ds(nh + h * N_KEYS + key1, 1), :] * fold for h in range(PEER_HEADS)])

    def gates(a_ref, wa_ref, theta_rows, e1_rows, ii, tc):
        rows = slice(ii * N_KEYS, (ii + 1) * N_KEYS)
        cols = slice(tc * LANES, (tc + 1) * LANES)
        wsum = None
        for h in range(PEER_HEADS):
            e2 = g2_ref[tc, h * N_KEYS:(h + 1) * N_KEYS, :]
            term = e1_rows[h][:, cols] * jnp.where(e2 >= theta_rows[h][:, cols], e2, 0.0)
            wsum = term if wsum is None else wsum + term
        a = a_ref[tc, rows, :]
        inner = a * (GELU_C0 / PEER_U_SCALE + (GELU_C1 / PEER_U_SCALE ** 3) * (a * a))
        wa_ref[rows, cols] = ((a * wsum) * (1.0 + jnp.tanh(inner))).astype(wa_ref.dtype)

    def outputs(wa_ref, r, c):
        rows = slice(r * mrows, (r + 1) * mrows)
        cols = slice(c * half, (c + 1) * half)
        acc_ref[rows, cols] += jnp.dot(vt_ref[rows, :], wa_ref[:, cols], preferred_element_type=F32)

    def step(act, gate, out):
        vec, mxu_a, mxu_o = [], [], []
        rows_cache = {}

        def gate_piece(ii, tc):
            if ii not in rows_cache:
                rows_cache[ii] = gate_rows(j - 1, ii)
            gates(*gate, *rows_cache[ii], ii, tc)

        if gate:
            vec = [functools.partial(gate_piece, ii, tc) for ii in range(per) for tc in range(nchunk)]
        if act is not None:
            mxu_a = [functools.partial(activations, act, m, c) for m in range(eb // arows) for c in range(2)]
        if out is not None:
            mxu_o = [functools.partial(outputs, out, r, c) for c in range(2) for r in range(acc_ref.shape[0] // mrows)]
        n = max(len(vec), len(mxu_a), len(mxu_o))
        for k in range(n):
            for stage in (vec, mxu_a, mxu_o):
                for piece in stage[k * len(stage) // n:(k + 1) * len(stage) // n]:
                    piece()

    steady = (j > 1) & (j < nblk)

    @pl.when(j == 0)
    def _():
        acc_ref[...] = jnp.zeros_like(acc_ref)
        step(a_even, None, None)

    @pl.when(j == 1)
    def _():
        step(a_odd, (a_even, wa_even), None)

    @pl.when(steady & (j % 2 == 0))
    def _():
        step(a_even, (a_odd, wa_odd), wa_even)

    @pl.when(steady & (j % 2 == 1))
    def _():
        step(a_odd, (a_even, wa_even), wa_odd)

    @pl.when(j == nblk)
    def _():
        step(None, (a_odd, wa_odd), wa_even)

    @pl.when(j == nblk + 1)
    def _():
        step(None, None, wa_odd)
        x2 = x1_ref[...] + acc_ref[...].T * (1.0 / PEER_W_SCALE)
        ms = jnp.mean(x2 * x2, axis=-1, keepdims=True)
        y_ref[...] = x2 * lax.rsqrt(ms + EPS) * g_ref[...]


def peer_experts(xn, g1, g2, u, vt, x1, final_g, tm=512):
    t, d = xn.shape
    eb = vt.shape[2]
    tm = min(tm, t)
    assert t % tm == 0 and N_EXPERTS % (2 * eb) == 0 and eb % N_KEYS == 0 and tm % (2 * LANES) == 0
    rows = 2 * PEER_HEADS * N_KEYS
    nblk = N_EXPERTS // eb
    return pl.pallas_call(
        _peer_experts_kernel,
        grid=(t // tm, nblk + 2),
        in_specs=[pl.BlockSpec((tm, d), lambda i, j: (i, 0)),
                  pl.BlockSpec((rows, tm), lambda i, j: (0, i)),
                  pl.BlockSpec((tm // LANES, rows // 2, LANES), lambda i, j: (i, 0, 0)),
                  pl.BlockSpec((eb, d), lambda i, j: (jnp.minimum(j, nblk - 1), 0)),
                  pl.BlockSpec((None, d, eb), lambda i, j: (jnp.clip(j - 2, 0, nblk - 1), 0, 0)),
                  pl.BlockSpec((tm, d), lambda i, j: (i, 0), pipeline_mode=pl.Buffered(1)),
                  pl.BlockSpec((1, d), lambda i, j: (0, 0))],
        out_specs=pl.BlockSpec((tm, d), lambda i, j: (i, 0)),
        out_shape=jax.ShapeDtypeStruct((t, d), F32),
        scratch_shapes=[pltpu.VMEM((d, tm), F32), pltpu.VMEM((2, tm // LANES, eb, LANES), F32),
                        pltpu.VMEM((2, eb, tm), F8)],
        compiler_params=_params("parallel", "arbitrary"),
        name="peer_experts",
    )(xn, g1, g2, u, vt, x1, final_g.reshape(1, d).astype(F32))


def _encoder(x, p):
    b, seq, d = x.shape
    t = b * seq
    xt = x.reshape(t, d)
    xn1 = rms_norm_bf16(xt, p["norm1_g"])
    hproj = project(xn1, p["w_in"], 0, HYENA_COLS, BF16)
    glog = project(xn1, p["w_in"], HYENA_COLS + ATT_COLS, GATE_COLS, BF16)
    ya = hyena_branch(hproj.reshape(b, seq, HYENA_COLS), p["conv_w"], p["conv_b"], p["filt_w1"], p["filt_b1"],
                      p["filt_w2"], p["filt_b2"], p["filt_w3"], p["filt_b3"], p["filt_w4"], p["filt_freq"],
                      p["hyena_bias"])
    att = [dilated_attention(project_classes(xn1, p["w_in"], g), b, g) for g in range(N_GROUPS)]
    x1, xn2, xn2_f8 = merge(xt, glog, ya.reshape(t, HYENA_WIDTH), [o for o, _ in att], [l for _, l in att],
                            p["w_branch_a"], p["w_branch_b"], p["w_out"], p["norm2_g"])
    g1, g2 = peer_scores(xn2, p["peer_wq"], p["peer_subkeys"])
    y = peer_experts(xn2_f8, g1, g2, p["peer_u"], p["peer_vt"], x1, p["final_g"])
    return y.reshape(b, seq, d)


def kernel(x_prompt, x_sample, norm1_g, w_in, conv_w, conv_b, filt_w1, filt_b1, filt_w2, filt_b2, filt_w3,
           filt_b3, filt_w4, filt_freq, hyena_bias, w_branch_a, w_branch_b, w_out, norm2_g, peer_wq,
           peer_subkeys, peer_u, peer_v, final_g):
    assert norm1_g.shape[0] == 1, "single-layer encoder"
    p = dict(
        norm1_g=norm1_g[0].astype(F32), w_in=w_in[0].astype(BF16), conv_w=conv_w[0], conv_b=conv_b[0],
        filt_w1=filt_w1[0], filt_b1=filt_b1[0], filt_w2=filt_w2[0], filt_b2=filt_b2[0], filt_w3=filt_w3[0],
        filt_b3=filt_b3[0], filt_w4=filt_w4[0], filt_freq=filt_freq[0], hyena_bias=hyena_bias[0],
        w_branch_a=w_branch_a[0].astype(BF16), w_branch_b=w_branch_b[0].astype(BF16),
        w_out=w_out[0].astype(BF16), norm2_g=norm2_g[0], peer_wq=peer_wq[0].astype(BF16),
        peer_subkeys=peer_subkeys[0].reshape(2 * PEER_HEADS, N_KEYS, N_KEYS).astype(BF16),
        peer_u=(peer_u[0] * PEER_U_SCALE).astype(F8),
        peer_vt=peer_v[0].astype(F8).reshape(N_EXPERTS // PEER_EB, PEER_EB, D_MODEL).transpose(0, 2, 1),
        final_g=final_g,
    )
    return (_encoder(x_prompt, p), _encoder(x_sample, p))
```

```python
import functools
import math

import jax
import jax.numpy as jnp
from jax import lax
from jax.experimental import pallas as pl
from jax.experimental.pallas import tpu as pltpu

F32 = jnp.float32
BF16 = jnp.bfloat16
F8 = jnp.float8_e4m3fn
PEER_U_SCALE = 32.0
PEER_W_SCALE = 16.0

D_MODEL = 2048
HYENA_WIDTH = 1024
FILTER_EMB = 33
FILTER_EMB_PAD = 128
FILTER_HIDDEN = 64
DECAY_TARGET = 1e-2
FAST_DECAY_PCT = 0.3
SLOW_DECAY_PCT = 1.5
ATT_GROUPS = ((128, 1), (512, 4), (2048, 16))
N_GROUPS = 3
HEADS_PER_GROUP = 8
HEAD_DIM = 64
N_ATT_HEADS = N_GROUPS * HEADS_PER_GROUP
ATT_WIDTH = HEADS_PER_GROUP * HEAD_DIM
HYENA_COLS = 3 * HYENA_WIDTH
ATT_COLS = 3 * N_GROUPS * ATT_WIDTH
GATE_COLS = 2 * D_MODEL
PEER_HEADS = 8
N_KEYS = 128
N_EXPERTS = N_KEYS * N_KEYS
PEER_TOPK = 16
PEER_EB = 512
EPS = 1e-6
NEG_INF = -1e30
ATT_HALF = 64
LANES = 128
VMEM_LIMIT = 56 * 1024 * 1024
HIGHEST = lax.Precision.HIGHEST


def _params(*sem, flags=None):
    return pltpu.CompilerParams(dimension_semantics=sem, vmem_limit_bytes=VMEM_LIMIT, flags=flags)


def _rms_norm_kernel(x_ref, g_ref, o_ref):
    x = x_ref[...]
    ms = jnp.mean(x * x, axis=-1, keepdims=True)
    o_ref[...] = (x * lax.rsqrt(ms + EPS) * g_ref[...]).astype(o_ref.dtype)


def rms_norm_bf16(x, g, tm=512):
    t, d = x.shape
    tm = min(tm, t)
    assert t % tm == 0
    return pl.pallas_call(
        _rms_norm_kernel,
        grid=(t // tm,),
        in_specs=[pl.BlockSpec((tm, d), lambda i: (i, 0)), pl.BlockSpec((1, d), lambda i: (0, 0))],
        out_specs=pl.BlockSpec((tm, d), lambda i: (i, 0)),
        out_shape=jax.ShapeDtypeStruct((t, d), BF16),
        compiler_params=_params("parallel"),
        name="rms_norm",
    )(x, g.reshape(1, d))


def _project_kernel(x_ref, w_ref, o_ref):
    o_ref[...] = jnp.dot(x_ref[...], w_ref[...], preferred_element_type=F32).astype(o_ref.dtype)


def project(xn, w, col0, ncols, out_dtype, tm=2048, tn=512):
    t, d = xn.shape
    tm = min(tm, t)
    assert t % tm == 0 and ncols % tn == 0 and col0 % tn == 0
    cb0 = col0 // tn
    return pl.pallas_call(
        _project_kernel,
        grid=(t // tm, ncols // tn),
        in_specs=[
            pl.BlockSpec((tm, d), lambda i, j: (i, 0)),
            pl.BlockSpec((d, tn), lambda i, j: (0, cb0 + j)),
        ],
        out_specs=pl.BlockSpec((tm, tn), lambda i, j: (i, j)),
        out_shape=jax.ShapeDtypeStruct((t, ncols), out_dtype),
        compiler_params=_params("parallel", "arbitrary"),
        name="project",
    )(xn, w)


def _project_classes_kernel(x_ref, w_ref, o_ref, acc_ref, *, dilation):
    which = pl.program_id(1)
    acc = jnp.dot(x_ref[...], w_ref[...], preferred_element_type=F32)
    nlane, tm, _ = acc_ref.shape
    for k in range(nlane):
        acc_ref[k] = acc[:, k * LANES:(k + 1) * LANES]
    rows = tm // dilation
    for part in range(3):
        @pl.when(which == part)
        def _(part=part):
            for r in range(dilation):
                for k in range(nlane):
                    piece = acc_ref[k] if dilation == 1 else acc_ref[k, pl.ds(r, rows, stride=dilation), :]
                    c0 = (3 * r + part) * nlane * LANES + k * LANES
                    o_ref[:, c0:c0 + LANES] = piece.astype(o_ref.dtype)


def project_classes(xn, w, group, tm=2048):
    t, d = xn.shape
    dilation = ATT_GROUPS[group][1]
    tm = min(tm, t)
    assert t % tm == 0 and tm % (8 * dilation) == 0
    cb0 = HYENA_COLS // ATT_WIDTH + group
    return pl.pallas_call(
        functools.partial(_project_classes_kernel, dilation=dilation),
        grid=(t // tm, 3),
        in_specs=[
            pl.BlockSpec((tm, d), lambda i, j: (i, 0)),
            pl.BlockSpec((d, ATT_WIDTH), lambda i, j: (0, cb0 + N_GROUPS * j)),
        ],
        out_specs=pl.BlockSpec((tm // dilation, dilation * 3 * ATT_WIDTH), lambda i, j: (i, 0)),
        out_shape=jax.ShapeDtypeStruct((t // dilation, dilation * 3 * ATT_WIDTH), BF16),
        scratch_shapes=[pltpu.VMEM((ATT_WIDTH // LANES, tm, LANES), F32)],
        compiler_params=_params("parallel", "arbitrary"),
        name="project_classes",
    )(xn, w)


def _filter_mlp_kernel(z_ref, t_ref, keep_ref, w1_ref, b1_ref, w2_ref, b2_ref, w3_ref, b3_ref, w4_ref, fr_ref,
                       dl_ref, h_ref, l1_ref):
    fr = fr_ref[...]
    h = jnp.sin(fr * (jnp.dot(z_ref[...], w1_ref[...], precision=HIGHEST, preferred_element_type=F32)
                      + b1_ref[...]))
    h = jnp.sin(fr * (jnp.dot(h, w2_ref[...], precision=HIGHEST, preferred_element_type=F32) + b2_ref[...]))
    h = jnp.sin(fr * (jnp.dot(h, w3_ref[...], precision=HIGHEST, preferred_element_type=F32) + b3_ref[...]))
    h = jnp.dot(h, w4_ref[...], precision=HIGHEST, preferred_element_type=F32)
    h = h * jnp.exp(-t_ref[...] * dl_ref[...])
    h_ref[...] = (h * keep_ref[...]).astype(h_ref.dtype)

    @pl.when(pl.program_id(0) == 0)
    def _():
        l1_ref[...] = jnp.zeros_like(l1_ref)

    l1_ref[...] += jnp.sum(jnp.abs(h), axis=0, keepdims=True)


def filter_mlp(seq, w1, b1, w2, b2, w3, b3, w4, freq, tl=512):
    t = jnp.linspace(0.0, 1.0, seq, dtype=F32)[:, None]
    bands = (FILTER_EMB - 1) // 2
    ang = 2.0 * math.pi * jnp.arange(seq, dtype=F32)[:, None] / seq
    fb = jnp.linspace(1e-4, bands - 1, bands, dtype=F32)[None, :]
    z = jnp.concatenate([t, jnp.cos(fb * ang), -jnp.sin(fb * ang)], axis=-1)
    z = jnp.pad(z, ((0, 0), (0, FILTER_EMB_PAD - FILTER_EMB)))
    by_row = lambda v: jnp.concatenate([v, v[0:1], v[1:][::-1]], axis=0)
    keep = (jnp.arange(2 * seq) != seq).astype(F32)[:, None]
    w1p = jnp.pad(w1.astype(F32), ((0, FILTER_EMB_PAD - FILTER_EMB), (0, 0)))
    min_decay = math.log(DECAY_TARGET) / SLOW_DECAY_PCT
    max_decay = math.log(DECAY_TARGET) / FAST_DECAY_PCT
    deltas = jnp.abs(jnp.linspace(min_decay, max_decay, HYENA_WIDTH, dtype=F32))[None, :]
    tl = min(tl, seq)
    assert seq % tl == 0
    c = HYENA_WIDTH
    hid = FILTER_HIDDEN
    per_dir = seq // tl
    full = lambda r, cc: pl.BlockSpec((r, cc), lambda i: (0, 0))
    return pl.pallas_call(
        _filter_mlp_kernel,
        grid=(2 * per_dir,),
        in_specs=[
            pl.BlockSpec((tl, FILTER_EMB_PAD), lambda i: (i, 0)),
            pl.BlockSpec((tl, 1), lambda i: (i, 0)),
            pl.BlockSpec((tl, 1), lambda i: (i, 0)),
            full(FILTER_EMB_PAD, hid), full(1, hid), full(hid, hid), full(1, hid), full(hid, hid),
            full(1, hid), pl.BlockSpec((hid, c), lambda i: (0, i // per_dir)), full(1, hid), full(1, c),
        ],
        out_specs=[pl.BlockSpec((tl, c), lambda i: (i, 0)), pl.BlockSpec((1, c), lambda i: (0, 0))],
        out_shape=[jax.ShapeDtypeStruct((2 * seq, c), BF16), jax.ShapeDtypeStruct((1, c), F32)],
        compiler_params=_params("arbitrary"),
        name="filter_mlp",
    )(by_row(z), by_row(t), keep, w1p, b1.reshape(1, hid).astype(F32), w2.astype(F32), b2.reshape(1, hid).astype(F32),
      w3.astype(F32), b3.reshape(1, hid).astype(F32), w4.astype(F32), freq.reshape(1, hid).astype(F32), deltas)


def _fft_split(n):
    lg = int(math.log2(n))
    assert 1 << lg == n
    n1 = 1 << ((lg + 1) // 2)
    return n1, n // n1


def _slow_dft_tables(n1):
    a = jnp.arange(n1, dtype=jnp.int32)
    ang = ((a[:, None] * a[None, :]) % n1).astype(F32) * (2.0 * math.pi / n1)
    return jnp.cos(ang), jnp.sin(ang)


def _fast_dft_tables(n1, n2):
    n = n1 * n2
    k1 = jnp.arange(n1, dtype=jnp.int32)[:, None, None]
    k2 = jnp.arange(n2, dtype=jnp.int32)[None, :, None]
    f = jnp.arange(n2, dtype=jnp.int32)[None, None, :]
    ang = ((f * (k1 + n1 * k2)) % n).astype(F32) * (2.0 * math.pi / n)
    c, s = jnp.cos(ang), jnp.sin(ang)
    top = jnp.concatenate([c, s], axis=2)
    bot = jnp.concatenate([-s, c], axis=2)
    return jnp.concatenate([top, bot], axis=1)


def _left_matmul_kernel(f_ref, x_ref, o_ref):
    o = jnp.dot(f_ref[...], x_ref[...].astype(f_ref.dtype), preferred_element_type=F32)
    o_ref[...] = o.astype(o_ref.dtype)


def left_matmul(f, x, out_dtype, ln=4096):
    b, k, lanes = x.shape
    m = f.shape[0]
    ln = min(ln, lanes)
    assert lanes % ln == 0 and f.shape[1] == k
    return pl.pallas_call(
        _left_matmul_kernel,
        grid=(b, lanes // ln),
        in_specs=[pl.BlockSpec((m, k), lambda i, j: (0, 0)),
                  pl.BlockSpec((None, k, ln), lambda i, j: (i, 0, j))],
        out_specs=pl.BlockSpec((None, m, ln), lambda i, j: (i, 0, j)),
        out_shape=jax.ShapeDtypeStruct((b, m, lanes), out_dtype),
        compiler_params=_params("parallel", "parallel"),
        name="left_matmul",
    )(f, x)


def _spectrum_mid_kernel(g_ref, a_ref, sc_ref, o_ref):
    n2 = a_ref.shape[1]
    a = a_ref[...].reshape(2 * n2, a_ref.shape[2])
    x = jnp.dot(g_ref[...], a, preferred_element_type=F32) * sc_ref[...]
    o_ref[...] = x.reshape(o_ref.shape)


def spectrum_mid(g, a, scale, cb=512):
    _, n1, n2, c = a.shape
    cb = min(cb, c)
    return pl.pallas_call(
        _spectrum_mid_kernel,
        grid=(n1, c // cb),
        in_specs=[pl.BlockSpec((None, 2 * n2, 2 * n2), lambda i, j: (i, 0, 0)),
                  pl.BlockSpec((2, None, n2, cb), lambda i, j: (0, i, 0, j)),
                  pl.BlockSpec((1, cb), lambda i, j: (0, j))],
        out_specs=pl.BlockSpec((2, None, n2, cb), lambda i, j: (0, i, 0, j)),
        out_shape=jax.ShapeDtypeStruct(a.shape, F32),
        compiler_params=_params("parallel", "parallel"),
        name="spectrum_mid",
    )(g, a, scale)


def _conv_mid_kernel(g_ref, gi_ref, a_ref, k_ref, o_ref):
    bblk, _, n2, cb = a_ref.shape
    kr, ki = k_ref[0], k_ref[1]
    for bi in range(bblk):
        a = a_ref[bi].reshape(2 * n2, cb)
        x = jnp.dot(g_ref[...], a, preferred_element_type=F32)
        xr, xi = x[:n2], x[n2:]
        y = jnp.concatenate([xr * kr - xi * ki, xr * ki + xi * kr], axis=0).astype(BF16)
        z = jnp.dot(gi_ref[...], y, preferred_element_type=F32)
        o_ref[bi] = z.reshape(2, n2, cb).astype(o_ref.dtype)


def conv_mid(g, gi, a, kf, block_bytes=2 << 20):
    b, _, n1, n2, c = a.shape
    bblk = max(1, min(b, block_bytes // (2 * n2 * c * 2)))
    assert b % bblk == 0
    return pl.pallas_call(
        _conv_mid_kernel,
        grid=(n1, b // bblk),
        in_specs=[pl.BlockSpec((None, 2 * n2, 2 * n2), lambda i, bb: (i, 0, 0)),
                  pl.BlockSpec((None, 2 * n2, 2 * n2), lambda i, bb: (i, 0, 0)),
                  pl.BlockSpec((bblk, 2, None, n2, c), lambda i, bb: (bb, 0, i, 0, 0)),
                  pl.BlockSpec((2, None, n2, c), lambda i, bb: (0, i, 0, 0))],
        out_specs=pl.BlockSpec((bblk, 2, None, n2, c), lambda i, bb: (bb, 0, i, 0, 0)),
        out_shape=jax.ShapeDtypeStruct(a.shape, BF16),
        compiler_params=_params("parallel", "parallel"),
        name="conv_mid",
    )(g, gi, a, kf)


def _conv_out_kernel(f_ref, z_ref, w_ref, x0_ref, bias_ref, o_ref):
    y = jnp.dot(f_ref[...], z_ref[...], preferred_element_type=F32)
    o_ref[...] = x0_ref[...] * (y + w_ref[...] * bias_ref[...])


def conv_out(f, z, w, x0, bias_l, ln=4096):
    b, k, lanes = z.shape
    m = f.shape[0]
    ln = min(ln, lanes)
    assert lanes % ln == 0 and bias_l.shape == (1, ln)
    return pl.pallas_call(
        _conv_out_kernel,
        grid=(b, lanes // ln),
        in_specs=[pl.BlockSpec((m, k), lambda i, j: (0, 0)),
                  pl.BlockSpec((None, k, ln), lambda i, j: (i, 0, j)),
                  pl.BlockSpec((None, m, ln), lambda i, j: (i, 0, j)),
                  pl.BlockSpec((None, m, ln), lambda i, j: (i, 0, j)),
                  pl.BlockSpec((1, ln), lambda i, j: (0, 0))],
        out_specs=pl.BlockSpec((None, m, ln), lambda i, j: (i, 0, j)),
        out_shape=jax.ShapeDtypeStruct((b, m, lanes), F32),
        compiler_params=_params("parallel", "parallel"),
        name="conv_out",
    )(f, z, w, x0, bias_l)


def _hyena_pre_kernel(x0_ref, x1_ref, v_ref, w0_ref, w1_ref, wv_ref, b0_ref, b1_ref, bv_ref, x0o_ref, wo_ref):
    seq = x0_ref.shape[0]
    row = lax.broadcasted_iota(jnp.int32, x0_ref.shape, 0)
    first, last = row == 0, row == seq - 1

    def conv(u_ref, w_ref, b_ref):
        u = u_ref[...].astype(F32)
        prev = jnp.where(first, 0.0, pltpu.roll(u, 1, axis=0))
        nxt = jnp.where(last, 0.0, pltpu.roll(u, seq - 1, axis=0))
        return b_ref[...] + prev * w_ref[0:1, :] + u * w_ref[1:2, :] + nxt * w_ref[2:3, :]

    x0o_ref[...] = conv(x0_ref, w0_ref, b0_ref)
    wo_ref[...] = conv(v_ref, wv_ref, bv_ref) * conv(x1_ref, w1_ref, b1_ref)


def hyena_pre(hproj, conv_w, conv_b):
    b, seq, _ = hproj.shape
    c = HYENA_WIDTH
    cb = max(LANES, min(c, (1 << 20) // seq))
    nb = c // cb
    u_spec = lambda g: pl.BlockSpec((None, seq, cb), lambda i, j: (i, 0, g * nb + j))
    w_spec = lambda g: pl.BlockSpec((3, cb), lambda i, j: (0, g * nb + j))
    b_spec = lambda g: pl.BlockSpec((1, cb), lambda i, j: (0, g * nb + j))
    o_spec = pl.BlockSpec((None, seq, cb), lambda i, j: (i, 0, j))
    cw = conv_w.astype(F32)
    cbias = conv_b.reshape(1, 3 * c).astype(F32)
    return pl.pallas_call(
        _hyena_pre_kernel,
        grid=(b, nb),
        in_specs=[u_spec(0), u_spec(1), u_spec(2), w_spec(0), w_spec(1), w_spec(2),
                  b_spec(0), b_spec(1), b_spec(2)],
        out_specs=[o_spec, o_spec],
        out_shape=[jax.ShapeDtypeStruct((b, seq, c), F32)] * 2,
        compiler_params=_params("parallel", "parallel"),
        name="hyena_pre",
    )(hproj, hproj, hproj, cw, cw, cw, cbias, cbias, cbias)


def hyena_branch(hproj, conv_w, conv_b, fw1, fb1, fw2, fb2, fw3, fb3, fw4, freq, hyena_bias):
    b, seq, _ = hproj.shape
    c = HYENA_WIDTH
    n = 2 * seq
    n1, n2 = _fft_split(n)
    n1h = n1 // 2
    cos1, sin1 = _slow_dft_tables(n1)
    g = _fast_dft_tables(n1, n2)
    gi = jnp.swapaxes(g, 1, 2)

    k_full, l1 = filter_mlp(seq, fw1, fb1, fw2, fb2, fw3, fb3, fw4, freq)
    scale = 1.0 / ((l1 + EPS) * n)
    fa_full = jnp.concatenate([cos1, -sin1], axis=0)
    fa_full = fa_full.astype(BF16)
    g, gi = g.astype(BF16), gi.astype(BF16)
    ka = left_matmul(fa_full, k_full.reshape(1, n1, n2 * c), BF16)
    kf = spectrum_mid(g, ka.reshape(2, n1, n2, c), scale)

    x0c, w = hyena_pre(hproj, conv_w, conv_b)
    a = left_matmul(fa_full[:, :n1h], w.reshape(b, n1h, n2 * c), BF16)
    z = conv_mid(g, gi, a.reshape(b, 2, n1, n2, c), kf)
    fc = jnp.concatenate([cos1[:n1h], -sin1[:n1h]], axis=1).astype(BF16)
    ln = min(4096, n2 * c)
    bias_l = jnp.tile(hyena_bias.reshape(1, c).astype(F32), (1, ln // c))
    y = conv_out(fc, z.reshape(b, 2 * n1, n2 * c), w.reshape(b, n1h, n2 * c), x0c.reshape(b, n1h, n2 * c),
                 bias_l, ln=ln)
    return y.reshape(b, seq, c)


ATT_SUB = 128


def _dilated_attn_kernel(q_ref, kp_ref, kc_ref, kn_ref, vp_ref, vc_ref, vn_ref, o_ref, l_ref, *,
                         n_cls, dilation, slopes):
    tq = q_ref.shape[0]
    q0 = pl.program_id(1) * tq
    res = pl.program_id(2)
    sk = ATT_SUB + 2 * ATT_HALF
    scale = 1.0 / math.sqrt(HEAD_DIM)
    qi = lax.broadcasted_iota(jnp.int32, (ATT_SUB, sk), 0)
    kj = lax.broadcasted_iota(jnp.int32, (ATT_SUB, sk), 1) - ATT_HALF
    dist = jnp.abs(kj - qi)
    band = dist <= ATT_HALF
    adist = (dilation * dist).astype(F32)
    lane = lax.broadcasted_iota(jnp.int32, (sk, LANES), 1)
    low_k = lane < HEAD_DIM
    low_q = lax.broadcasted_iota(jnp.int32, (ATT_SUB, LANES), 1) < HEAD_DIM
    for sub in range(tq // ATT_SUB):
        r0 = sub * ATT_SUB
        kabs = q0 + r0 + kj
        mask = band & (kabs >= 0) & (kabs < n_cls)
        def keys(p_ref, c_ref, n_ref, cs):
            parts = []
            if r0 == 0:
                parts.append(p_ref[:, cs])
                parts.append(c_ref[0:min(tq, ATT_SUB + ATT_HALF), cs])
            else:
                parts.append(c_ref[r0 - ATT_HALF:min(tq, r0 + ATT_SUB + ATT_HALF), cs])
            if r0 + ATT_SUB + ATT_HALF > tq:
                parts.append(n_ref[:, cs])
            return jnp.concatenate(parts, axis=0)

        heads = [(pair, e) for pair in range(HEADS_PER_GROUP // 2) for e in range(2)]
        scores, values = [], []
        for pair, e in heads:
            cs = slice(pair * LANES, (pair + 1) * LANES)
            sel = low_k if e == 0 else ~low_k
            k = keys(kp_ref, kc_ref, kn_ref, cs)
            v = keys(vp_ref, vc_ref, vn_ref, cs)
            ke = jnp.where(sel, k, jnp.zeros_like(k))
            values.append(jnp.where(sel, v, jnp.zeros_like(v)))
            s = lax.dot_general(q_ref[r0:r0 + ATT_SUB, cs], ke, (((1,), (1,)), ((), ())),
                                preferred_element_type=F32) * scale
            scores.append(jnp.where(mask, s - slopes[2 * pair + e] * adist, NEG_INF))
        maxes = [jnp.max(s, axis=-1, keepdims=True) for s in scores]
        probs = [jnp.exp(s - m) for s, m in zip(scores, maxes)]
        sums = [jnp.sum(p, axis=-1, keepdims=True) for p in probs]
        outs = [jnp.dot(p.astype(BF16), v, preferred_element_type=F32) for p, v in zip(probs, values)]
        if dilation == 1:
            tok = slice(r0, r0 + ATT_SUB)
        else:
            tok = pl.ds(r0 * dilation + res, ATT_SUB, stride=dilation)
        for pair in range(HEADS_PER_GROUP // 2):
            h0, h1 = 2 * pair, 2 * pair + 1
            inv = 1.0 / jnp.where(low_q, sums[h0], sums[h1])
            o_ref[pair, tok, :] = (outs[h0] + outs[h1]) * inv
            l_ref[pair, tok, :] = jnp.where(low_q, maxes[h0] + jnp.log(sums[h0]), maxes[h1] + jnp.log(sums[h1]))


def dilated_attention(qkv, batch, group, out_block_bytes=4 << 20):
    window, dilation = ATT_GROUPS[group]
    assert window // (2 * dilation) == ATT_HALF
    n_cls = qkv.shape[0] // batch
    seq = n_cls * dilation
    tq = min(256, n_cls, max(ATT_SUB, out_block_bytes // (dilation * ATT_WIDTH * 4)))
    assert n_cls % tq == 0 and tq % ATT_SUB == 0
    hb = tq // ATT_HALF
    n_halo = n_cls // ATT_HALF
    slopes = tuple(2.0 ** (-8.0 * (group * HEADS_PER_GROUP + h + 1.0) / N_ATT_HEADS)
                   for h in range(HEADS_PER_GROUP))
    x = qkv.reshape(batch, n_cls, dilation * 3 * ATT_WIDTH)
    cur = lambda which: pl.BlockSpec((None, tq, ATT_WIDTH), lambda i, j, r: (i, j, 3 * r + which))
    prev = lambda which: pl.BlockSpec((None, ATT_HALF, ATT_WIDTH),
                                      lambda i, j, r: (i, jnp.maximum(j * hb - 1, 0), 3 * r + which))
    nxt = lambda which: pl.BlockSpec((None, ATT_HALF, ATT_WIDTH),
                                     lambda i, j, r: (i, jnp.minimum((j + 1) * hb, n_halo - 1), 3 * r + which))
    npair = ATT_WIDTH // LANES
    o_spec = pl.BlockSpec((None, npair, tq * dilation, LANES), lambda i, j, r: (i, 0, j, 0))
    return pl.pallas_call(
        functools.partial(_dilated_attn_kernel, n_cls=n_cls, dilation=dilation, slopes=slopes),
        grid=(batch, n_cls // tq, dilation),
        in_specs=[cur(0), prev(1), cur(1), nxt(1), prev(2), cur(2), nxt(2)],
        out_specs=[o_spec, o_spec],
        out_shape=[jax.ShapeDtypeStruct((batch, npair, seq, LANES), F32)] * 2,
        compiler_params=_params("parallel", "parallel", "arbitrary"),
        name="dilated_attn",
    )(x, x, x, x, x, x, x)


def _merge_kernel(x_ref, ga_ref, gb_ref, ya_ref, o0_ref, o1_ref, o2_ref, l0_ref, l1_ref, l2_ref,
                  wa_ref, wb_ref, wo_ref, g2_ref, x1_ref, xn_ref, xn8_ref):
    parts = []
    for pair in range(o0_ref.shape[0]):
        l0, l1, l2 = l0_ref[pair], l1_ref[pair], l2_ref[pair]
        m = jnp.maximum(jnp.maximum(l0, l1), l2)
        e0, e1, e2 = jnp.exp(l0 - m), jnp.exp(l1 - m), jnp.exp(l2 - m)
        yb = (e0 * o0_ref[pair] + e1 * o1_ref[pair] + e2 * o2_ref[pair]) / (e0 + e1 + e2)
        parts.append(yb.astype(BF16))
    pa = jnp.dot(ya_ref[...].astype(BF16), wa_ref[...], preferred_element_type=F32)
    pb = jnp.dot(jnp.concatenate(parts, axis=1), wb_ref[...], preferred_element_type=F32)
    merged = (jax.nn.sigmoid(ga_ref[...].astype(F32)) * pa
              + jax.nn.sigmoid(gb_ref[...].astype(F32)) * pb)
    x1 = x_ref[...] + jnp.dot(merged.astype(BF16), wo_ref[...], preferred_element_type=F32)
    x1_ref[...] = x1
    ms = jnp.mean(x1 * x1, axis=-1, keepdims=True)
    xn = x1 * lax.rsqrt(ms + EPS) * g2_ref[...]
    xn_ref[...] = xn.astype(xn_ref.dtype)
    xn8_ref[...] = xn.astype(xn8_ref.dtype)


def merge(x, glog, ya, outs, lses, wa, wb, wo, g2, tm=256):
    t, d = x.shape
    _, npair, seq, _ = outs[0].shape
    tm = min(tm, seq)
    assert seq % tm == 0
    tiles = seq // tm
    row = lambda c, blk=0: pl.BlockSpec((tm, c), lambda i: (i, blk))
    const = lambda r, c: pl.BlockSpec((r, c), lambda i: (0, 0), pipeline_mode=pl.Buffered(1))
    att = pl.BlockSpec((None, npair, tm, LANES), lambda i: (i // tiles, 0, i % tiles, 0))
    aw = ATT_WIDTH
    return pl.pallas_call(
        _merge_kernel,
        grid=(t // tm,),
        in_specs=[row(d), row(d, 0), row(d, 1), row(HYENA_WIDTH), att, att, att, att, att, att,
                  const(HYENA_WIDTH, d), const(aw, d), const(d, d), const(1, d)],
        out_specs=[row(d), row(d), row(d)],
        out_shape=[jax.ShapeDtypeStruct((t, d), F32), jax.ShapeDtypeStruct((t, d), BF16),
                   jax.ShapeDtypeStruct((t, d), F8)],
        compiler_params=_params("parallel"),
        name="merge",
    )(x, glog, glog, ya, *outs, *lses, wa, wb, wo, g2.reshape(1, d).astype(F32))


def _descending_max(curs, k, emit):
    curs = list(curs)
    for it in range(k):
        for c, cur in enumerate(curs):
            m = jnp.max(cur, axis=0, keepdims=True)
            emit(c, it, m)
            if it + 1 < k:
                curs[c] = jnp.where(cur == m, -jnp.inf, cur)


PEER_LOCKSTEP_HEADS = 2


def _peer_scores_kernel(xn_ref, wq_ref, sk_ref, g1_ref, g2_ref, top_ref, s_ref):
    k = PEER_TOPK
    sub = 8
    nh = PEER_HEADS * N_KEYS
    q = jnp.dot(xn_ref[...], wq_ref[...], preferred_element_type=F32).astype(BF16)
    low_rank = lax.broadcasted_iota(jnp.int32, (sub, LANES), 0) < 4
    for tc in range(xn_ref.shape[0] // LANES):
        cols = slice(tc * LANES, (tc + 1) * LANES)
        for h0 in range(0, PEER_HEADS, PEER_LOCKSTEP_HEADS):
            heads = range(h0, h0 + PEER_LOCKSTEP_HEADS)
            chains = [(hh, p) for hh in range(PEER_LOCKSTEP_HEADS) for p in range(2)]
            for c, (hh, p) in enumerate(chains):
                hp = 2 * (h0 + hh) + p
                qs = q[cols, hp * N_KEYS:(hp + 1) * N_KEYS]
                s_ref[c] = lax.dot_general(sk_ref[hp], qs, (((1,), (1,)), ((), ())), preferred_element_type=F32)

            def put_top(c, it, m):
                top_ref[c, it:it + 1, :] = m

            _descending_max([s_ref[c] for c in range(len(chains))], k, put_top)

            cands = []
            for hh in range(PEER_LOCKSTEP_HEADS):
                v1, v2 = top_ref.at[2 * hh], top_ref.at[2 * hh + 1]
                tiles = [v1[r:r + 1, :] + v2[0:sub, :] for r in range(4)]
                tiles += [v1[0:sub, :] + v2[r:r + 1, :] for r in range(4)]
                tiles += [v1[sub:k, :] + v2[0:1, :], v1[0:1, :] + v2[sub:k, :]]
                cands.append(jnp.concatenate(tiles, axis=0))
            repeated = [None] * 4 + [low_rank] * 4 + [None] * 2
            taus = [None] * PEER_LOCKSTEP_HEADS

            def put_tau(c, it, m):
                taus[c] = m

            _descending_max(cands, k, put_tau)
            for hh, h in enumerate(heads):
                rows = slice(h * N_KEYS, (h + 1) * N_KEYS)
                rows_e = slice(nh + h * N_KEYS, nh + (h + 1) * N_KEYS)
                v1, v2 = top_ref.at[2 * hh], top_ref.at[2 * hh + 1]
                tau = taus[hh]
                best = v1[0:1, :] + v2[0:1, :]
                zsum = None
                for ti, dup in enumerate(repeated):
                    tile = cands[hh][ti * sub:(ti + 1) * sub, :]
                    keep = tile >= tau
                    if dup is not None:
                        keep = keep & ~dup
                    part = jnp.sum(jnp.where(keep, jnp.exp(tile - best), 0.0), axis=0, keepdims=True)
                    zsum = part if zsum is None else zsum + part
                inv_z = 1.0 / zsum
                v2_all = v2[...]
                e2_top = jnp.exp(v2_all - v2[0:1, :]) * inv_z
                s1 = s_ref[2 * hh]
                theta = jnp.full(s1.shape, jnp.inf, F32)
                for r in range(k):
                    c = v1[r:r + 1, :] + v2_all
                    theta_r = jnp.min(jnp.where(c >= tau, e2_top, jnp.inf), axis=0, keepdims=True)
                    theta = jnp.where(s1 == v1[r:r + 1, :], theta_r, theta)
                g1_ref[rows, cols] = theta
                g1_ref[rows_e, cols] = jnp.exp(s1 - v1[0:1, :])
                g2_ref[tc, rows, :] = jnp.exp(s_ref[2 * hh + 1] - v2[0:1, :]) * inv_z


def peer_scores(xn, wq, subkeys, tm=256):
    t, d = xn.shape
    tm = min(tm, t)
    assert t % tm == 0 and tm % LANES == 0
    rows = 2 * PEER_HEADS * N_KEYS
    return pl.pallas_call(
        _peer_scores_kernel,
        grid=(t // tm,),
        in_specs=[pl.BlockSpec((tm, d), lambda i: (i, 0)),
                  pl.BlockSpec((d, rows), lambda i: (0, 0), pipeline_mode=pl.Buffered(1)),
                  pl.BlockSpec((2 * PEER_HEADS, N_KEYS, N_KEYS), lambda i: (0, 0, 0))],
        out_specs=[pl.BlockSpec((rows, tm), lambda i: (0, i)),
                   pl.BlockSpec((tm // LANES, rows // 2, LANES), lambda i: (i, 0, 0))],
        out_shape=[jax.ShapeDtypeStruct((rows, t), F32),
                   jax.ShapeDtypeStruct((t // LANES, rows // 2, LANES), F32)],
        scratch_shapes=[pltpu.VMEM((2 * PEER_LOCKSTEP_HEADS, PEER_TOPK, LANES), F32),
                        pltpu.VMEM((2 * PEER_LOCKSTEP_HEADS, N_KEYS, LANES), F32)],
        compiler_params=_params("parallel"),
        name="peer_scores",
    )(xn, wq, subkeys)


GELU_C0 = math.sqrt(2.0 / math.pi)
GELU_C1 = GELU_C0 * 0.044715


def _peer_experts_kernel(xn_ref, g1_ref, g2_ref, u_ref, vt_ref, x1_ref, g_ref, y_ref, acc_ref, a_ref, wa_ref):
    j = pl.program_id(1)
    nblk = pl.num_programs(1) - 2
    _, nchunk, eb, _ = a_ref.shape
    tm = nchunk * LANES
    a_even, a_odd = a_ref.at[0], a_ref.at[1]
    wa_even, wa_odd = wa_ref.at[0], wa_ref.at[1]
    nh = PEER_HEADS * N_KEYS
    per = eb // N_KEYS
    half = tm // 2

    mrows = 256
    arows = 256

    def activations(a_ref, m, c):
        rows = slice(m * arows, (m + 1) * arows)
        cols = slice(c * half, (c + 1) * half)
        a = lax.dot_general(u_ref[rows, :], xn_ref[cols, :], (((1,), (1,)), ((), ())),
                            preferred_element_type=F32)
        for k in range(half // LANES):
            a_ref[c * (half // LANES) + k, rows, :] = a[:, k * LANES:(k + 1) * LANES]

    def gate_rows(block, ii):
        key1 = block * per + ii
        fold = 0.5 * PEER_W_SCALE / PEER_U_SCALE
        return ([g1_ref[pl.ds(h * N_KEYS + key1, 1), :] for h in range(PEER_HEADS)],
                [g1_ref[pl.ds(nh + h * N_KEYS + key1, 1), :] * fold for h in range(PEER_HEADS)])

    def gates(a_ref, wa_ref, theta_rows, e1_rows, ii, tc):
        rows = slice(ii * N_KEYS, (ii + 1) * N_KEYS)
        cols = slice(tc * LANES, (tc + 1) * LANES)
        wsum = None
        for h in range(PEER_HEADS):
            e2 = g2_ref[tc, h * N_KEYS:(h + 1) * N_KEYS, :]
            term = e1_rows[h][:, cols] * jnp.where(e2 >= theta_rows[h][:, cols], e2, 0.0)
            wsum = term if wsum is None else wsum + term
        a = a_ref[tc, rows, :]
        inner = a * (GELU_C0 / PEER_U_SCALE + (GELU_C1 / PEER_U_SCALE ** 3) * (a * a))
        wa_ref[rows, cols] = ((a * wsum) * (1.0 + jnp.tanh(inner))).astype(wa_ref.dtype)

    def outputs(wa_ref, r, c):
        rows = slice(r * mrows, (r + 1) * mrows)
        cols = slice(c * half, (c + 1) * half)
        acc_ref[rows, cols] += jnp.dot(vt_ref[rows, :], wa_ref[:, cols], preferred_element_type=F32)

    def step(act, gate, out):
        vec, mxu_a, mxu_o = [], [], []
        rows_cache = {}

        def gate_piece(ii, tc):
            if ii not in rows_cache:
                rows_cache[ii] = gate_rows(j - 1, ii)
            gates(*gate, *rows_cache[ii], ii, tc)

        if gate:
            vec = [functools.partial(gate_piece, ii, tc) for ii in range(per) for tc in range(nchunk)]
        if act is not None:
            mxu_a = [functools.partial(activations, act, m, c) for m in range(eb // arows) for c in range(2)]
        if out is not None:
            mxu_o = [functools.partial(outputs, out, r, c) for c in range(2) for r in range(acc_ref.shape[0] // mrows)]
        n = max(len(vec), len(mxu_a), len(mxu_o))
        for k in range(n):
            for stage in (vec, mxu_a, mxu_o):
                for piece in stage[k * len(stage) // n:(k + 1) * len(stage) // n]:
                    piece()

    steady = (j > 1) & (j < nblk)

    @pl.when(j == 0)
    def _():
        acc_ref[...] = jnp.zeros_like(acc_ref)
        step(a_even, None, None)

    @pl.when(j == 1)
    def _():
        step(a_odd, (a_even, wa_even), None)

    @pl.when(steady & (j % 2 == 0))
    def _():
        step(a_even, (a_odd, wa_odd), wa_even)

    @pl.when(steady & (j % 2 == 1))
    def _():
        step(a_odd, (a_even, wa_even), wa_odd)

    @pl.when(j == nblk)
    def _():
        step(None, (a_odd, wa_odd), wa_even)

    @pl.when(j == nblk + 1)
    def _():
        step(None, None, wa_odd)
        x2 = x1_ref[...] + acc_ref[...].T * (1.0 / PEER_W_SCALE)
        ms = jnp.mean(x2 * x2, axis=-1, keepdims=True)
        y_ref[...] = x2 * lax.rsqrt(ms + EPS) * g_ref[...]


def peer_experts(xn, g1, g2, u, vt, x1, final_g, tm=512):
    t, d = xn.shape
    eb = vt.shape[2]
    tm = min(tm, t)
    assert t % tm == 0 and N_EXPERTS % (2 * eb) == 0 and eb % N_KEYS == 0 and tm % (2 * LANES) == 0
    rows = 2 * PEER_HEADS * N_KEYS
    nblk = N_EXPERTS // eb
    return pl.pallas_call(
        _peer_experts_kernel,
        grid=(t // tm, nblk + 2),
        in_specs=[pl.BlockSpec((tm, d), lambda i, j: (i, 0)),
                  pl.BlockSpec((rows, tm), lambda i, j: (0, i)),
                  pl.BlockSpec((tm // LANES, rows // 2, LANES), lambda i, j: (i, 0, 0)),
                  pl.BlockSpec((eb, d), lambda i, j: (jnp.minimum(j, nblk - 1), 0)),
                  pl.BlockSpec((None, d, eb), lambda i, j: (jnp.clip(j - 2, 0, nblk - 1), 0, 0)),
                  pl.BlockSpec((tm, d), lambda i, j: (i, 0), pipeline_mode=pl.Buffered(1)),
                  pl.BlockSpec((1, d), lambda i, j: (0, 0))],
        out_specs=pl.BlockSpec((tm, d), lambda i, j: (i, 0)),
        out_shape=jax.ShapeDtypeStruct((t, d), F32),
        scratch_shapes=[pltpu.VMEM((d, tm), F32), pltpu.VMEM((2, tm // LANES, eb, LANES), F32),
                        pltpu.VMEM((2, eb, tm), F8)],
        compiler_params=_params("parallel", "arbitrary"),
        name="peer_experts",
    )(xn, g1, g2, u, vt, x1, final_g.reshape(1, d).astype(F32))


def _encoder(x, p):
    b, seq, d = x.shape
    t = b * seq
    xt = x.reshape(t, d)
    xn1 = rms_norm_bf16(xt, p["norm1_g"])
    hproj = project(xn1, p["w_in"], 0, HYENA_COLS, BF16)
    glog = project(xn1, p["w_in"], HYENA_COLS + ATT_COLS, GATE_COLS, BF16)
    ya = hyena_branch(hproj.reshape(b, seq, HYENA_COLS), p["conv_w"], p["conv_b"], p["filt_w1"], p["filt_b1"],
                      p["filt_w2"], p["filt_b2"], p["filt_w3"], p["filt_b3"], p["filt_w4"], p["filt_freq"],
                      p["hyena_bias"])
    att = [dilated_attention(project_classes(xn1, p["w_in"], g), b, g) for g in range(N_GROUPS)]
    x1, xn2, xn2_f8 = merge(xt, glog, ya.reshape(t, HYENA_WIDTH), [o for o, _ in att], [l for _, l in att],
                            p["w_branch_a"], p["w_branch_b"], p["w_out"], p["norm2_g"])
    g1, g2 = peer_scores(xn2, p["peer_wq"], p["peer_subkeys"])
    y = peer_experts(xn2_f8, g1, g2, p["peer_u"], p["peer_vt"], x1, p["final_g"])
    return y.reshape(b, seq, d)


def kernel(x_prompt, x_sample, norm1_g, w_in, conv_w, conv_b, filt_w1, filt_b1, filt_w2, filt_b2, filt_w3,
           filt_b3, filt_w4, filt_freq, hyena_bias, w_branch_a, w_branch_b, w_out, norm2_g, peer_wq,
           peer_subkeys, peer_u, peer_v, final_g):
    assert norm1_g.shape[0] == 1, "single-layer encoder"
    p = dict(
        norm1_g=norm1_g[0].astype(F32), w_in=w_in[0].astype(BF16), conv_w=conv_w[0], conv_b=conv_b[0],
        filt_w1=filt_w1[0], filt_b1=filt_b1[0], filt_w2=filt_w2[0], filt_b2=filt_b2[0], filt_w3=filt_w3[0],
        filt_b3=filt_b3[0], filt_w4=filt_w4[0], filt_freq=filt_freq[0], hyena_bias=hyena_bias[0],
        w_branch_a=w_branch_a[0].astype(BF16), w_branch_b=w_branch_b[0].astype(BF16),
        w_out=w_out[0].astype(BF16), norm2_g=norm2_g[0], peer_wq=peer_wq[0].astype(BF16),
        peer_subkeys=peer_subkeys[0].reshape(2 * PEER_HEADS, N_KEYS, N_KEYS).astype(BF16),
        peer_u=(peer_u[0] * PEER_U_SCALE).astype(F8),
        peer_vt=peer_v[0].astype(F8).reshape(N_EXPERTS // PEER_EB, PEER_EB, D_MODEL).transpose(0, 2, 1),
        final_g=final_g,
    )
    return (_encoder(x_prompt, p), _encoder(x_sample, p))
```

```python
import functools
import math

import jax
import jax.numpy as jnp
from jax import lax
from jax.experimental import pallas as pl
from jax.experimental.pallas import tpu as pltpu

F32 = jnp.float32
BF16 = jnp.bfloat16
F8 = jnp.float8_e4m3fn
PEER_U_SCALE = 32.0
PEER_W_SCALE = 16.0

D_MODEL = 2048
HYENA_WIDTH = 1024
FILTER_EMB = 33
FILTER_EMB_PAD = 128
FILTER_HIDDEN = 64
DECAY_TARGET = 1e-2
FAST_DECAY_PCT = 0.3
SLOW_DECAY_PCT = 1.5
ATT_GROUPS = ((128, 1), (512, 4), (2048, 16))
N_GROUPS = 3
HEADS_PER_GROUP = 8
HEAD_DIM = 64
N_ATT_HEADS = N_GROUPS * HEADS_PER_GROUP
ATT_WIDTH = HEADS_PER_GROUP * HEAD_DIM
HYENA_COLS = 3 * HYENA_WIDTH
ATT_COLS = 3 * N_GROUPS * ATT_WIDTH
GATE_COLS = 2 * D_MODEL
PEER_HEADS = 8
N_KEYS = 128
N_EXPERTS = N_KEYS * N_KEYS
PEER_TOPK = 16
PEER_EB = 512
EPS = 1e-6
NEG_INF = -1e30
ATT_HALF = 64
LANES = 128
VMEM_LIMIT = 56 * 1024 * 1024
HIGHEST = lax.Precision.HIGHEST


def _params(*sem, flags=None):
    return pltpu.CompilerParams(dimension_semantics=sem, vmem_limit_bytes=VMEM_LIMIT, flags=flags)


def _rms_norm_kernel(x_ref, g_ref, o_ref):
    x = x_ref[...]
    ms = jnp.mean(x * x, axis=-1, keepdims=True)
    o_ref[...] = (x * lax.rsqrt(ms + EPS) * g_ref[...]).astype(o_ref.dtype)


def rms_norm_bf16(x, g, tm=512):
    t, d = x.shape
    tm = min(tm, t)
    assert t % tm == 0
    return pl.pallas_call(
        _rms_norm_kernel,
        grid=(t // tm,),
        in_specs=[pl.BlockSpec((tm, d), lambda i: (i, 0)), pl.BlockSpec((1, d), lambda i: (0, 0))],
        out_specs=pl.BlockSpec((tm, d), lambda i: (i, 0)),
        out_shape=jax.ShapeDtypeStruct((t, d), BF16),
        compiler_params=_params("parallel"),
        name="rms_norm",
    )(x, g.reshape(1, d))


def _project_kernel(x_ref, w_ref, o_ref):
    o_ref[...] = jnp.dot(x_ref[...], w_ref[...], preferred_element_type=F32).astype(o_ref.dtype)


def project(xn, w, col0, ncols, out_dtype, tm=2048, tn=512):
    t, d = xn.shape
    tm = min(tm, t)
    assert t % tm == 0 and ncols % tn == 0 and col0 % tn == 0
    cb0 = col0 // tn
    return pl.pallas_call(
        _project_kernel,
        grid=(t // tm, ncols // tn),
        in_specs=[
            pl.BlockSpec((tm, d), lambda i, j: (i, 0)),
            pl.BlockSpec((d, tn), lambda i, j: (0, cb0 + j)),
        ],
        out_specs=pl.BlockSpec((tm, tn), lambda i, j: (i, j)),
        out_shape=jax.ShapeDtypeStruct((t, ncols), out_dtype),
        compiler_params=_params("parallel", "arbitrary"),
        name="project",
    )(xn, w)


def _project_classes_kernel(x_ref, w_ref, o_ref, acc_ref, *, dilation):
    which = pl.program_id(1)
    acc = jnp.dot(x_ref[...], w_ref[...], preferred_element_type=F32)
    nlane, tm, _ = acc_ref.shape
    for k in range(nlane):
        acc_ref[k] = acc[:, k * LANES:(k + 1) * LANES]
    rows = tm // dilation
    for part in range(3):
        @pl.when(which == part)
        def _(part=part):
            for r in range(dilation):
                for k in range(nlane):
                    piece = acc_ref[k] if dilation == 1 else acc_ref[k, pl.ds(r, rows, stride=dilation), :]
                    c0 = (3 * r + part) * nlane * LANES + k * LANES
                    o_ref[:, c0:c0 + LANES] = piece.astype(o_ref.dtype)


def project_classes(xn, w, group, tm=2048):
    t, d = xn.shape
    dilation = ATT_GROUPS[group][1]
    tm = min(tm, t)
    assert t % tm == 0 and tm % (8 * dilation) == 0
    cb0 = HYENA_COLS // ATT_WIDTH + group
    return pl.pallas_call(
        functools.partial(_project_classes_kernel, dilation=dilation),
        grid=(t // tm, 3),
        in_specs=[
            pl.BlockSpec((tm, d), lambda i, j: (i, 0)),
            pl.BlockSpec((d, ATT_WIDTH), lambda i, j: (0, cb0 + N_GROUPS * j)),
        ],
        out_specs=pl.BlockSpec((tm // dilation, dilation * 3 * ATT_WIDTH), lambda i, j: (i, 0)),
        out_shape=jax.ShapeDtypeStruct((t // dilation, dilation * 3 * ATT_WIDTH), BF16),
        scratch_shapes=[pltpu.VMEM((ATT_WIDTH // LANES, tm, LANES), F32)],
        compiler_params=_params("parallel", "arbitrary"),
        name="project_classes",
    )(xn, w)


def _filter_mlp_kernel(z_ref, t_ref, keep_ref, w1_ref, b1_ref, w2_ref, b2_ref, w3_ref, b3_ref, w4_ref, fr_ref,
                       dl_ref, h_ref, l1_ref):
    fr = fr_ref[...]
    h = jnp.sin(fr * (jnp.dot(z_ref[...], w1_ref[...], precision=HIGHEST, preferred_element_type=F32)
                      + b1_ref[...]))
    h = jnp.sin(fr * (jnp.dot(h, w2_ref[...], precision=HIGHEST, preferred_element_type=F32) + b2_ref[...]))
    h = jnp.sin(fr * (jnp.dot(h, w3_ref[...], precision=HIGHEST, preferred_element_type=F32) + b3_ref[...]))
    h = jnp.dot(h, w4_ref[...], precision=HIGHEST, preferred_element_type=F32)
    h = h * jnp.exp(-t_ref[...] * dl_ref[...])
    h_ref[...] = (h * keep_ref[...]).astype(h_ref.dtype)

    @pl.when(pl.program_id(0) == 0)
    def _():
        l1_ref[...] = jnp.zeros_like(l1_ref)

    l1_ref[...] += jnp.sum(jnp.abs(h), axis=0, keepdims=True)


def filter_mlp(seq, w1, b1, w2, b2, w3, b3, w4, freq, tl=512):
    t = jnp.linspace(0.0, 1.0, seq, dtype=F32)[:, None]
    bands = (FILTER_EMB - 1) // 2
    ang = 2.0 * math.pi * jnp.arange(seq, dtype=F32)[:, None] / seq
    fb = jnp.linspace(1e-4, bands - 1, bands, dtype=F32)[None, :]
    z = jnp.concatenate([t, jnp.cos(fb * ang), -jnp.sin(fb * ang)], axis=-1)
    z = jnp.pad(z, ((0, 0), (0, FILTER_EMB_PAD - FILTER_EMB)))
    by_row = lambda v: jnp.concatenate([v, v[0:1], v[1:][::-1]], axis=0)
    keep = (jnp.arange(2 * seq) != seq).astype(F32)[:, None]
    w1p = jnp.pad(w1.astype(F32), ((0, FILTER_EMB_PAD - FILTER_EMB), (0, 0)))
    min_decay = math.log(DECAY_TARGET) / SLOW_DECAY_PCT
    max_decay = math.log(DECAY_TARGET) / FAST_DECAY_PCT
    deltas = jnp.abs(jnp.linspace(min_decay, max_decay, HYENA_WIDTH, dtype=F32))[None, :]
    tl = min(tl, seq)
    assert seq % tl == 0
    c = HYENA_WIDTH
    hid = FILTER_HIDDEN
    per_dir = seq // tl
    full = lambda r, cc: pl.BlockSpec((r, cc), lambda i: (0, 0))
    return pl.pallas_call(
        _filter_mlp_kernel,
        grid=(2 * per_dir,),
        in_specs=[
            pl.BlockSpec((tl, FILTER_EMB_PAD), lambda i: (i, 0)),
            pl.BlockSpec((tl, 1), lambda i: (i, 0)),
            pl.BlockSpec((tl, 1), lambda i: (i, 0)),
            full(FILTER_EMB_PAD, hid), full(1, hid), full(hid, hid), full(1, hid), full(hid, hid),
            full(1, hid), pl.BlockSpec((hid, c), lambda i: (0, i // per_dir)), full(1, hid), full(1, c),
        ],
        out_specs=[pl.BlockSpec((tl, c), lambda i: (i, 0)), pl.BlockSpec((1, c), lambda i: (0, 0))],
        out_shape=[jax.ShapeDtypeStruct((2 * seq, c), BF16), jax.ShapeDtypeStruct((1, c), F32)],
        compiler_params=_params("arbitrary"),
        name="filter_mlp",
    )(by_row(z), by_row(t), keep, w1p, b1.reshape(1, hid).astype(F32), w2.astype(F32), b2.reshape(1, hid).astype(F32),
      w3.astype(F32), b3.reshape(1, hid).astype(F32), w4.astype(F32), freq.reshape(1, hid).astype(F32), deltas)


def _fft_split(n):
    lg = int(math.log2(n))
    assert 1 << lg == n
    n1 = 1 << ((lg + 1) // 2)
    return n1, n // n1


def _slow_dft_tables(n1):
    a = jnp.arange(n1, dtype=jnp.int32)
    ang = ((a[:, None] * a[None, :]) % n1).astype(F32) * (2.0 * math.pi / n1)
    return jnp.cos(ang), jnp.sin(ang)


def _fast_dft_tables(n1, n2):
    n = n1 * n2
    k1 = jnp.arange(n1, dtype=jnp.int32)[:, None, None]
    k2 = jnp.arange(n2, dtype=jnp.int32)[None, :, None]
    f = jnp.arange(n2, dtype=jnp.int32)[None, None, :]
    ang = ((f * (k1 + n1 * k2)) % n).astype(F32) * (2.0 * math.pi / n)
    c, s = jnp.cos(ang), jnp.sin(ang)
    top = jnp.concatenate([c, s], axis=2)
    bot = jnp.concatenate([-s, c], axis=2)
    return jnp.concatenate([top, bot], axis=1)


def _left_matmul_kernel(f_ref, x_ref, o_ref):
    o = jnp.dot(f_ref[...], x_ref[...].astype(f_ref.dtype), preferred_element_type=F32)
    o_ref[...] = o.astype(o_ref.dtype)


def left_matmul(f, x, out_dtype, ln=4096):
    b, k, lanes = x.shape
    m = f.shape[0]
    ln = min(ln, lanes)
    assert lanes % ln == 0 and f.shape[1] == k
    return pl.pallas_call(
        _left_matmul_kernel,
        grid=(b, lanes // ln),
        in_specs=[pl.BlockSpec((m, k), lambda i, j: (0, 0)),
                  pl.BlockSpec((None, k, ln), lambda i, j: (i, 0, j))],
        out_specs=pl.BlockSpec((None, m, ln), lambda i, j: (i, 0, j)),
        out_shape=jax.ShapeDtypeStruct((b, m, lanes), out_dtype),
        compiler_params=_params("parallel", "parallel"),
        name="left_matmul",
    )(f, x)


def _spectrum_mid_kernel(g_ref, a_ref, sc_ref, o_ref):
    n2 = a_ref.shape[1]
    a = a_ref[...].reshape(2 * n2, a_ref.shape[2])
    x = jnp.dot(g_ref[...], a, preferred_element_type=F32) * sc_ref[...]
    o_ref[...] = x.reshape(o_ref.shape)


def spectrum_mid(g, a, scale, cb=512):
    _, n1, n2, c = a.shape
    cb = min(cb, c)
    return pl.pallas_call(
        _spectrum_mid_kernel,
        grid=(n1, c // cb),
        in_specs=[pl.BlockSpec((None, 2 * n2, 2 * n2), lambda i, j: (i, 0, 0)),
                  pl.BlockSpec((2, None, n2, cb), lambda i, j: (0, i, 0, j)),
                  pl.BlockSpec((1, cb), lambda i, j: (0, j))],
        out_specs=pl.BlockSpec((2, None, n2, cb), lambda i, j: (0, i, 0, j)),
        out_shape=jax.ShapeDtypeStruct(a.shape, F32),
        compiler_params=_params("parallel", "parallel"),
        name="spectrum_mid",
    )(g, a, scale)


def _conv_mid_kernel(g_ref, gi_ref, a_ref, k_ref, o_ref):
    bblk, _, n2, cb = a_ref.shape
    kr, ki = k_ref[0], k_ref[1]
    xs = [jnp.dot(g_ref[...], a_ref[bi].reshape(2 * n2, cb), preferred_element_type=F32) for bi in range(bblk)]
    ys = [jnp.concatenate([x[:n2] * kr - x[n2:] * ki, x[:n2] * ki + x[n2:] * kr], axis=0).astype(BF16)
          for x in xs]
    for bi, y in enumerate(ys):
        z = jnp.dot(gi_ref[...], y, preferred_element_type=F32)
        o_ref[bi] = z.reshape(2, n2, cb).astype(o_ref.dtype)


def conv_mid(g, gi, a, kf, block_bytes=2 << 20):
    b, _, n1, n2, c = a.shape
    bblk = max(1, min(b, block_bytes // (2 * n2 * c * 2)))
    assert b % bblk == 0
    return pl.pallas_call(
        _conv_mid_kernel,
        grid=(n1, b // bblk),
        in_specs=[pl.BlockSpec((None, 2 * n2, 2 * n2), lambda i, bb: (i, 0, 0)),
                  pl.BlockSpec((None, 2 * n2, 2 * n2), lambda i, bb: (i, 0, 0)),
                  pl.BlockSpec((bblk, 2, None, n2, c), lambda i, bb: (bb, 0, i, 0, 0)),
                  pl.BlockSpec((2, None, n2, c), lambda i, bb: (0, i, 0, 0))],
        out_specs=pl.BlockSpec((bblk, 2, None, n2, c), lambda i, bb: (bb, 0, i, 0, 0)),
        out_shape=jax.ShapeDtypeStruct(a.shape, BF16),
        compiler_params=_params("parallel", "parallel"),
        name="conv_mid",
    )(g, gi, a, kf)


def _conv_out_kernel(f_ref, z_ref, w_ref, x0_ref, bias_ref, o_ref):
    y = jnp.dot(f_ref[...], z_ref[...], preferred_element_type=F32)
    o_ref[...] = (x0_ref[...].astype(F32) * (y + w_ref[...].astype(F32) * bias_ref[...])).astype(o_ref.dtype)


def conv_out(f, z, w, x0, bias_l, ln=4096):
    b, k, lanes = z.shape
    m = f.shape[0]
    ln = min(ln, lanes)
    assert lanes % ln == 0 and bias_l.shape == (1, ln)
    return pl.pallas_call(
        _conv_out_kernel,
        grid=(b, lanes // ln),
        in_specs=[pl.BlockSpec((m, k), lambda i, j: (0, 0)),
                  pl.BlockSpec((None, k, ln), lambda i, j: (i, 0, j)),
                  pl.BlockSpec((None, m, ln), lambda i, j: (i, 0, j)),
                  pl.BlockSpec((None, m, ln), lambda i, j: (i, 0, j)),
                  pl.BlockSpec((1, ln), lambda i, j: (0, 0))],
        out_specs=pl.BlockSpec((None, m, ln), lambda i, j: (i, 0, j)),
        out_shape=jax.ShapeDtypeStruct((b, m, lanes), BF16),
        compiler_params=_params("parallel", "parallel"),
        name="conv_out",
    )(f, z, w, x0, bias_l)


def _hyena_pre_kernel(x0_ref, x1_ref, v_ref, w0_ref, w1_ref, wv_ref, b0_ref, b1_ref, bv_ref, x0o_ref, wo_ref):
    seq = x0_ref.shape[0]
    row = lax.broadcasted_iota(jnp.int32, x0_ref.shape, 0)
    first, last = row == 0, row == seq - 1

    def conv(u_ref, w_ref, b_ref):
        u = u_ref[...].astype(F32)
        prev = jnp.where(first, 0.0, pltpu.roll(u, 1, axis=0))
        nxt = jnp.where(last, 0.0, pltpu.roll(u, seq - 1, axis=0))
        return b_ref[...] + prev * w_ref[0:1, :] + u * w_ref[1:2, :] + nxt * w_ref[2:3, :]

    x0o_ref[...] = conv(x0_ref, w0_ref, b0_ref).astype(x0o_ref.dtype)
    wo_ref[...] = (conv(v_ref, wv_ref, bv_ref) * conv(x1_ref, w1_ref, b1_ref)).astype(wo_ref.dtype)


def hyena_pre(hproj, conv_w, conv_b):
    b, seq, _ = hproj.shape
    c = HYENA_WIDTH
    cb = max(LANES, min(c, (1 << 20) // seq))
    nb = c // cb
    u_spec = lambda g: pl.BlockSpec((None, seq, cb), lambda i, j: (i, 0, g * nb + j))
    w_spec = lambda g: pl.BlockSpec((3, cb), lambda i, j: (0, g * nb + j))
    b_spec = lambda g: pl.BlockSpec((1, cb), lambda i, j: (0, g * nb + j))
    o_spec = pl.BlockSpec((None, seq, cb), lambda i, j: (i, 0, j))
    cw = conv_w.astype(F32)
    cbias = conv_b.reshape(1, 3 * c).astype(F32)
    return pl.pallas_call(
        _hyena_pre_kernel,
        grid=(b, nb),
        in_specs=[u_spec(0), u_spec(1), u_spec(2), w_spec(0), w_spec(1), w_spec(2),
                  b_spec(0), b_spec(1), b_spec(2)],
        out_specs=[o_spec, o_spec],
        out_shape=[jax.ShapeDtypeStruct((b, seq, c), BF16)] * 2,
        compiler_params=_params("parallel", "parallel"),
        name="hyena_pre",
    )(hproj, hproj, hproj, cw, cw, cw, cbias, cbias, cbias)


def hyena_branch(hproj, conv_w, conv_b, fw1, fb1, fw2, fb2, fw3, fb3, fw4, freq, hyena_bias):
    b, seq, _ = hproj.shape
    c = HYENA_WIDTH
    n = 2 * seq
    n1, n2 = _fft_split(n)
    n1h = n1 // 2
    cos1, sin1 = _slow_dft_tables(n1)
    g = _fast_dft_tables(n1, n2)
    gi = jnp.swapaxes(g, 1, 2)

    k_full, l1 = filter_mlp(seq, fw1, fb1, fw2, fb2, fw3, fb3, fw4, freq)
    scale = 1.0 / ((l1 + EPS) * n)
    fa_full = jnp.concatenate([cos1, -sin1], axis=0)
    fa_full = fa_full.astype(BF16)
    g, gi = g.astype(BF16), gi.astype(BF16)
    ka = left_matmul(fa_full, k_full.reshape(1, n1, n2 * c), BF16)
    kf = spectrum_mid(g, ka.reshape(2, n1, n2, c), scale)

    x0c, w = hyena_pre(hproj, conv_w, conv_b)
    a = left_matmul(fa_full[:, :n1h], w.reshape(b, n1h, n2 * c), BF16)
    z = conv_mid(g, gi, a.reshape(b, 2, n1, n2, c), kf)
    fc = jnp.concatenate([cos1[:n1h], -sin1[:n1h]], axis=1).astype(BF16)
    ln = min(4096, n2 * c)
    bias_l = jnp.tile(hyena_bias.reshape(1, c).astype(F32), (1, ln // c))
    y = conv_out(fc, z.reshape(b, 2 * n1, n2 * c), w.reshape(b, n1h, n2 * c), x0c.reshape(b, n1h, n2 * c),
                 bias_l, ln=ln)
    return y.reshape(b, seq, c)


ATT_SUB = 128


def _dilated_attn_kernel(q_ref, kp_ref, kc_ref, kn_ref, vp_ref, vc_ref, vn_ref, o_ref, l_ref, *,
                         n_cls, dilation, slopes):
    tq = q_ref.shape[0]
    q0 = pl.program_id(1) * tq
    res = pl.program_id(2)
    sk = ATT_SUB + 2 * ATT_HALF
    scale = 1.0 / math.sqrt(HEAD_DIM)
    qi = lax.broadcasted_iota(jnp.int32, (ATT_SUB, sk), 0)
    kj = lax.broadcasted_iota(jnp.int32, (ATT_SUB, sk), 1) - ATT_HALF
    dist = jnp.abs(kj - qi)
    band = dist <= ATT_HALF
    adist = (dilation * dist).astype(F32)
    lane = lax.broadcasted_iota(jnp.int32, (sk, LANES), 1)
    low_k = lane < HEAD_DIM
    low_q = lax.broadcasted_iota(jnp.int32, (ATT_SUB, LANES), 1) < HEAD_DIM
    for sub in range(tq // ATT_SUB):
        r0 = sub * ATT_SUB
        kabs = q0 + r0 + kj
        mask = band & (kabs >= 0) & (kabs < n_cls)
        def keys(p_ref, c_ref, n_ref, cs):
            parts = []
            if r0 == 0:
                parts.append(p_ref[:, cs])
                parts.append(c_ref[0:min(tq, ATT_SUB + ATT_HALF), cs])
            else:
                parts.append(c_ref[r0 - ATT_HALF:min(tq, r0 + ATT_SUB + ATT_HALF), cs])
            if r0 + ATT_SUB + ATT_HALF > tq:
                parts.append(n_ref[:, cs])
            return jnp.concatenate(parts, axis=0)

        heads = [(pair, e) for pair in range(HEADS_PER_GROUP // 2) for e in range(2)]
        scores, values = [], []
        for pair, e in heads:
            cs = slice(pair * LANES, (pair + 1) * LANES)
            sel = low_k if e == 0 else ~low_k
            k = keys(kp_ref, kc_ref, kn_ref, cs)
            v = keys(vp_ref, vc_ref, vn_ref, cs)
            ke = jnp.where(sel, k, jnp.zeros_like(k))
            values.append(jnp.where(sel, v, jnp.zeros_like(v)))
            s = lax.dot_general(q_ref[r0:r0 + ATT_SUB, cs], ke, (((1,), (1,)), ((), ())),
                                preferred_element_type=F32) * scale
            scores.append(jnp.where(mask, s - slopes[2 * pair + e] * adist, NEG_INF))
        maxes = [jnp.max(s, axis=-1, keepdims=True) for s in scores]
        probs = [jnp.exp(s - m) for s, m in zip(scores, maxes)]
        sums = [jnp.sum(p, axis=-1, keepdims=True) for p in probs]
        outs = [jnp.dot(p.astype(BF16), v, preferred_element_type=F32) for p, v in zip(probs, values)]
        if dilation == 1:
            tok = slice(r0, r0 + ATT_SUB)
        else:
            tok = pl.ds(r0 * dilation + res, ATT_SUB, stride=dilation)
        for pair in range(HEADS_PER_GROUP // 2):
            h0, h1 = 2 * pair, 2 * pair + 1
            inv = 1.0 / jnp.where(low_q, sums[h0], sums[h1])
            o_ref[pair, tok, :] = (outs[h0] + outs[h1]) * inv
            l_ref[pair, tok, :] = jnp.where(low_q, maxes[h0] + jnp.log(sums[h0]), maxes[h1] + jnp.log(sums[h1]))


def dilated_attention(qkv, batch, group, out_block_bytes=4 << 20):
    window, dilation = ATT_GROUPS[group]
    assert window // (2 * dilation) == ATT_HALF
    n_cls = qkv.shape[0] // batch
    seq = n_cls * dilation
    tq = min(256, n_cls, max(ATT_SUB, out_block_bytes // (dilation * ATT_WIDTH * 4)))
    assert n_cls % tq == 0 and tq % ATT_SUB == 0
    hb = tq // ATT_HALF
    n_halo = n_cls // ATT_HALF
    slopes = tuple(2.0 ** (-8.0 * (group * HEADS_PER_GROUP + h + 1.0) / N_ATT_HEADS)
                   for h in range(HEADS_PER_GROUP))
    x = qkv.reshape(batch, n_cls, dilation * 3 * ATT_WIDTH)
    cur = lambda which: pl.BlockSpec((None, tq, ATT_WIDTH), lambda i, j, r: (i, j, 3 * r + which))
    prev = lambda which: pl.BlockSpec((None, ATT_HALF, ATT_WIDTH),
                                      lambda i, j, r: (i, jnp.maximum(j * hb - 1, 0), 3 * r + which))
    nxt = lambda which: pl.BlockSpec((None, ATT_HALF, ATT_WIDTH),
                                     lambda i, j, r: (i, jnp.minimum((j + 1) * hb, n_halo - 1), 3 * r + which))
    npair = ATT_WIDTH // LANES
    o_spec = pl.BlockSpec((None, npair, tq * dilation, LANES), lambda i, j, r: (i, 0, j, 0))
    return pl.pallas_call(
        functools.partial(_dilated_attn_kernel, n_cls=n_cls, dilation=dilation, slopes=slopes),
        grid=(batch, n_cls // tq, dilation),
        in_specs=[cur(0), prev(1), cur(1), nxt(1), prev(2), cur(2), nxt(2)],
        out_specs=[o_spec, o_spec],
        out_shape=[jax.ShapeDtypeStruct((batch, npair, seq, LANES), F32)] * 2,
        compiler_params=_params("parallel", "parallel", "arbitrary"),
        name="dilated_attn",
    )(x, x, x, x, x, x, x)


def _merge_kernel(x_ref, ga_ref, gb_ref, ya_ref, o0_ref, o1_ref, o2_ref, l0_ref, l1_ref, l2_ref,
                  wa_ref, wb_ref, wo_ref, g2_ref, x1_ref, xn_ref, xn8_ref):
    parts = []
    for pair in range(o0_ref.shape[0]):
        l0, l1, l2 = l0_ref[pair], l1_ref[pair], l2_ref[pair]
        m = jnp.maximum(jnp.maximum(l0, l1), l2)
        e0, e1, e2 = jnp.exp(l0 - m), jnp.exp(l1 - m), jnp.exp(l2 - m)
        yb = (e0 * o0_ref[pair] + e1 * o1_ref[pair] + e2 * o2_ref[pair]) / (e0 + e1 + e2)
        parts.append(yb.astype(BF16))
    pa = jnp.dot(ya_ref[...].astype(BF16), wa_ref[...], preferred_element_type=F32)
    pb = jnp.dot(jnp.concatenate(parts, axis=1), wb_ref[...], preferred_element_type=F32)
    merged = (jax.nn.sigmoid(ga_ref[...].astype(F32)) * pa
              + jax.nn.sigmoid(gb_ref[...].astype(F32)) * pb)
    x1 = x_ref[...] + jnp.dot(merged.astype(BF16), wo_ref[...], preferred_element_type=F32)
    x1_ref[...] = x1
    ms = jnp.mean(x1 * x1, axis=-1, keepdims=True)
    xn = x1 * lax.rsqrt(ms + EPS) * g2_ref[...]
    xn_ref[...] = xn.astype(xn_ref.dtype)
    xn8_ref[...] = xn.astype(xn8_ref.dtype)


def merge(x, glog, ya, outs, lses, wa, wb, wo, g2, tm=256):
    t, d = x.shape
    _, npair, seq, _ = outs[0].shape
    tm = min(tm, seq)
    assert seq % tm == 0
    tiles = seq // tm
    row = lambda c, blk=0: pl.BlockSpec((tm, c), lambda i: (i, blk))
    const = lambda r, c: pl.BlockSpec((r, c), lambda i: (0, 0), pipeline_mode=pl.Buffered(1))
    att = pl.BlockSpec((None, npair, tm, LANES), lambda i: (i // tiles, 0, i % tiles, 0))
    aw = ATT_WIDTH
    return pl.pallas_call(
        _merge_kernel,
        grid=(t // tm,),
        in_specs=[row(d), row(d, 0), row(d, 1), row(HYENA_WIDTH), att, att, att, att, att, att,
                  const(HYENA_WIDTH, d), const(aw, d), const(d, d), const(1, d)],
        out_specs=[row(d), row(d), row(d)],
        out_shape=[jax.ShapeDtypeStruct((t, d), F32), jax.ShapeDtypeStruct((t, d), BF16),
                   jax.ShapeDtypeStruct((t, d), F8)],
        compiler_params=_params("parallel"),
        name="merge",
    )(x, glog, glog, ya, *outs, *lses, wa, wb, wo, g2.reshape(1, d).astype(F32))


def _descending_max(curs, k, emit):
    curs = list(curs)
    for it in range(k):
        for c, cur in enumerate(curs):
            m = jnp.max(cur, axis=0, keepdims=True)
            emit(c, it, m)
            if it + 1 < k:
                curs[c] = jnp.where(cur == m, -jnp.inf, cur)


PEER_LOCKSTEP_HEADS = 2


def _peer_scores_kernel(xn_ref, wq_ref, sk_ref, g1_ref, g2_ref, top_ref, s_ref):
    k = PEER_TOPK
    sub = 8
    nh = PEER_HEADS * N_KEYS
    q = jnp.dot(xn_ref[...], wq_ref[...], preferred_element_type=F32).astype(BF16)
    low_rank = lax.broadcasted_iota(jnp.int32, (sub, LANES), 0) < 4
    for tc in range(xn_ref.shape[0] // LANES):
        cols = slice(tc * LANES, (tc + 1) * LANES)
        for h0 in range(0, PEER_HEADS, PEER_LOCKSTEP_HEADS):
            heads = range(h0, h0 + PEER_LOCKSTEP_HEADS)
            chains = [(hh, p) for hh in range(PEER_LOCKSTEP_HEADS) for p in range(2)]
            for c, (hh, p) in enumerate(chains):
                hp = 2 * (h0 + hh) + p
                qs = q[cols, hp * N_KEYS:(hp + 1) * N_KEYS]
                s_ref[c] = lax.dot_general(sk_ref[hp], qs, (((1,), (1,)), ((), ())), preferred_element_type=F32)

            def put_top(c, it, m):
                top_ref[c, it:it + 1, :] = m

            _descending_max([s_ref[c] for c in range(len(chains))], k, put_top)

            cands = []
            for hh in range(PEER_LOCKSTEP_HEADS):
                v1, v2 = top_ref.at[2 * hh], top_ref.at[2 * hh + 1]
                tiles = [v1[r:r + 1, :] + v2[0:sub, :] for r in range(4)]
                tiles += [v1[0:sub, :] + v2[r:r + 1, :] for r in range(4)]
                tiles += [v1[sub:k, :] + v2[0:1, :], v1[0:1, :] + v2[sub:k, :]]
                cands.append(jnp.concatenate(tiles, axis=0))
            repeated = [None] * 4 + [low_rank] * 4 + [None] * 2
            taus = [None] * PEER_LOCKSTEP_HEADS

            def put_tau(c, it, m):
                taus[c] = m

            _descending_max(cands, k, put_tau)
            for hh, h in enumerate(heads):
                rows = slice(h * N_KEYS, (h + 1) * N_KEYS)
                rows_e = slice(nh + h * N_KEYS, nh + (h + 1) * N_KEYS)
                v1, v2 = top_ref.at[2 * hh], top_ref.at[2 * hh + 1]
                tau = taus[hh]
                best = v1[0:1, :] + v2[0:1, :]
                zsum = None
                for ti, dup in enumerate(repeated):
                    tile = cands[hh][ti * sub:(ti + 1) * sub, :]
                    keep = tile >= tau
                    if dup is not None:
                        keep = keep & ~dup
                    part = jnp.sum(jnp.where(keep, jnp.exp(tile - best), 0.0), axis=0, keepdims=True)
                    zsum = part if zsum is None else zsum + part
                inv_z = 1.0 / zsum
                v2_all = v2[...]
                e2_top = jnp.exp(v2_all - v2[0:1, :]) * inv_z
                s1 = s_ref[2 * hh]
                theta = jnp.full(s1.shape, jnp.inf, F32)
                for r in range(k):
                    c = v1[r:r + 1, :] + v2_all
                    theta_r = jnp.min(jnp.where(c >= tau, e2_top, jnp.inf), axis=0, keepdims=True)
                    theta = jnp.where(s1 == v1[r:r + 1, :], theta_r, theta)
                g1_ref[rows, cols] = theta
                g1_ref[rows_e, cols] = jnp.exp(s1 - v1[0:1, :])
                g2_ref[tc, rows, :] = jnp.exp(s_ref[2 * hh + 1] - v2[0:1, :]) * inv_z


def peer_scores(xn, wq, subkeys, tm=256):
    t, d = xn.shape
    tm = min(tm, t)
    assert t % tm == 0 and tm % LANES == 0
    rows = 2 * PEER_HEADS * N_KEYS
    return pl.pallas_call(
        _peer_scores_kernel,
        grid=(t // tm,),
        in_specs=[pl.BlockSpec((tm, d), lambda i: (i, 0)),
                  pl.BlockSpec((d, rows), lambda i: (0, 0), pipeline_mode=pl.Buffered(1)),
                  pl.BlockSpec((2 * PEER_HEADS, N_KEYS, N_KEYS), lambda i: (0, 0, 0))],
        out_specs=[pl.BlockSpec((rows, tm), lambda i: (0, i)),
                   pl.BlockSpec((tm // LANES, rows // 2, LANES), lambda i: (i, 0, 0))],
        out_shape=[jax.ShapeDtypeStruct((rows, t), F32),
                   jax.ShapeDtypeStruct((t // LANES, rows // 2, LANES), F32)],
        scratch_shapes=[pltpu.VMEM((2 * PEER_LOCKSTEP_HEADS, PEER_TOPK, LANES), F32),
                        pltpu.VMEM((2 * PEER_LOCKSTEP_HEADS, N_KEYS, LANES), F32)],
        compiler_params=_params("parallel"),
        name="peer_scores",
    )(xn, wq, subkeys)


GELU_C0 = math.sqrt(2.0 / math.pi)
GELU_C1 = GELU_C0 * 0.044715


def _peer_experts_kernel(xn_ref, g1_ref, g2_ref, u_ref, vt_ref, x1_ref, g_ref, y_ref, acc_ref, a_ref, wa_ref):
    j = pl.program_id(1)
    nblk = pl.num_programs(1) - 2
    _, nchunk, eb, _ = a_ref.shape
    tm = nchunk * LANES
    a_even, a_odd = a_ref.at[0], a_ref.at[1]
    wa_even, wa_odd = wa_ref.at[0], wa_ref.at[1]
    nh = PEER_HEADS * N_KEYS
    per = eb // N_KEYS
    half = tm // 2

    mrows = 256
    arows = 256

    def activations(a_ref, m, c):
        rows = slice(m * arows, (m + 1) * arows)
        cols = slice(c * half, (c + 1) * half)
        a = lax.dot_general(u_ref[rows, :], xn_ref[cols, :], (((1,), (1,)), ((), ())),
                            preferred_element_type=F32)
        for k in range(half // LANES):
            a_ref[c * (half // LANES) + k, rows, :] = a[:, k * LANES:(k + 1) * LANES]

    def gate_rows(block, ii):
        key1 = block * per + ii
        fold = 0.5 * PEER_W_SCALE / PEER_U_SCALE
        return ([g1_ref[pl.ds(h * N_KEYS + key1, 1), :] for h in range(PEER_HEADS)],
                [g1_ref[pl.ds(nh + h * N_KEYS + key1, 1), :] * fold for h in range(PEER_HEADS)])

    def gates(a_ref, wa_ref, theta_rows, e1_rows, ii, tc):
        rows = slice(ii * N_KEYS, (ii + 1) * N_KEYS)
        cols = slice(tc * LANES, (tc + 1) * LANES)
        wsum = None
        for h in range(PEER_HEADS):
            e2 = g2_ref[tc, h * N_KEYS:(h + 1) * N_KEYS, :]
            term = e1_rows[h][:, cols] * jnp.where(e2 >= theta_rows[h][:, cols], e2, 0.0)
            wsum = term if wsum is None else wsum + term
        a = a_ref[tc, rows, :]
        inner = a * (GELU_C0 / PEER_U_SCALE + (GELU_C1 / PEER_U_SCALE ** 3) * (a * a))
        wa_ref[rows, cols] = ((a * wsum) * (1.0 + jnp.tanh(inner))).astype(wa_ref.dtype)

    def outputs(wa_ref, r, c):
        rows = slice(r * mrows, (r + 1) * mrows)
        cols = slice(c * half, (c + 1) * half)
        acc_ref[rows, cols] += jnp.dot(vt_ref[rows, :], wa_ref[:, cols], preferred_element_type=F32)

    def step(act, gate, out):
        vec, mxu_a, mxu_o = [], [], []
        rows_cache = {}

        def gate_piece(ii, tc):
            if ii not in rows_cache:
                rows_cache[ii] = gate_rows(j - 1, ii)
            gates(*gate, *rows_cache[ii], ii, tc)

        if gate:
            vec = [functools.partial(gate_piece, ii, tc) for ii in range(per) for tc in range(nchunk)]
        if act is not None:
            mxu_a = [functools.partial(activations, act, m, c) for m in range(eb // arows) for c in range(2)]
        if out is not None:
            mxu_o = [functools.partial(outputs, out, r, c) for c in range(2) for r in range(acc_ref.shape[0] // mrows)]
        n = max(len(vec), len(mxu_a), len(mxu_o))
        for k in range(n):
            for stage in (vec, mxu_a, mxu_o):
                for piece in stage[k * len(stage) // n:(k + 1) * len(stage) // n]:
                    piece()

    steady = (j > 1) & (j < nblk)

    @pl.when(j == 0)
    def _():
        acc_ref[...] = jnp.zeros_like(acc_ref)
        step(a_even, None, None)

    @pl.when(j == 1)
    def _():
        step(a_odd, (a_even, wa_even), None)

    @pl.when(steady & (j % 2 == 0))
    def _():
        step(a_even, (a_odd, wa_odd), wa_even)

    @pl.when(steady & (j % 2 == 1))
    def _():
        step(a_odd, (a_even, wa_even), wa_odd)

    @pl.when(j == nblk)
    def _():
        step(None, (a_odd, wa_odd), wa_even)

    @pl.when(j == nblk + 1)
    def _():
        step(None, None, wa_odd)
        x2 = x1_ref[...] + acc_ref[...].T * (1.0 / PEER_W_SCALE)
        ms = jnp.mean(x2 * x2, axis=-1, keepdims=True)
        y_ref[...] = x2 * lax.rsqrt(ms + EPS) * g_ref[...]


def peer_experts(xn, g1, g2, u, vt, x1, final_g, tm=512):
    t, d = xn.shape
    eb = vt.shape[2]
    tm = min(tm, t)
    assert t % tm == 0 and N_EXPERTS % (2 * eb) == 0 and eb % N_KEYS == 0 and tm % (2 * LANES) == 0
    rows = 2 * PEER_HEADS * N_KEYS
    nblk = N_EXPERTS // eb
    return pl.pallas_call(
        _peer_experts_kernel,
        grid=(t // tm, nblk + 2),
        in_specs=[pl.BlockSpec((tm, d), lambda i, j: (i, 0)),
                  pl.BlockSpec((rows, tm), lambda i, j: (0, i)),
                  pl.BlockSpec((tm // LANES, rows // 2, LANES), lambda i, j: (i, 0, 0)),
                  pl.BlockSpec((eb, d), lambda i, j: (jnp.minimum(j, nblk - 1), 0)),
                  pl.BlockSpec((None, d, eb), lambda i, j: (jnp.clip(j - 2, 0, nblk - 1), 0, 0)),
                  pl.BlockSpec((tm, d), lambda i, j: (i, 0), pipeline_mode=pl.Buffered(1)),
                  pl.BlockSpec((1, d), lambda i, j: (0, 0))],
        out_specs=pl.BlockSpec((tm, d), lambda i, j: (i, 0)),
        out_shape=jax.ShapeDtypeStruct((t, d), F32),
        scratch_shapes=[pltpu.VMEM((d, tm), F32), pltpu.VMEM((2, tm // LANES, eb, LANES), F32),
                        pltpu.VMEM((2, eb, tm), F8)],
        compiler_params=_params("parallel", "arbitrary"),
        name="peer_experts",
    )(xn, g1, g2, u, vt, x1, final_g.reshape(1, d).astype(F32))


def _encoder(x, p):
    b, seq, d = x.shape
    t = b * seq
    xt = x.reshape(t, d)
    xn1 = rms_norm_bf16(xt, p["norm1_g"])
    hproj = project(xn1, p["w_in"], 0, HYENA_COLS, BF16)
    glog = project(xn1, p["w_in"], HYENA_COLS + ATT_COLS, GATE_COLS, BF16)
    ya = hyena_branch(hproj.reshape(b, seq, HYENA_COLS), p["conv_w"], p["conv_b"], p["filt_w1"], p["filt_b1"],
                      p["filt_w2"], p["filt_b2"], p["filt_w3"], p["filt_b3"], p["filt_w4"], p["filt_freq"],
                      p["hyena_bias"])
    att = [dilated_attention(project_classes(xn1, p["w_in"], g), b, g) for g in range(N_GROUPS)]
    x1, xn2, xn2_f8 = merge(xt, glog, ya.reshape(t, HYENA_WIDTH), [o for o, _ in att], [l for _, l in att],
                            p["w_branch_a"], p["w_branch_b"], p["w_out"], p["norm2_g"])
    g1, g2 = peer_scores(xn2, p["peer_wq"], p["peer_subkeys"])
    y = peer_experts(xn2_f8, g1, g2, p["peer_u"], p["peer_vt"], x1, p["final_g"])
    return y.reshape(b, seq, d)


def kernel(x_prompt, x_sample, norm1_g, w_in, conv_w, conv_b, filt_w1, filt_b1, filt_w2, filt_b2, filt_w3,
           filt_b3, filt_w4, filt_freq, hyena_bias, w_branch_a, w_branch_b, w_out, norm2_g, peer_wq,
           peer_subkeys, peer_u, peer_v, final_g):
    assert norm1_g.shape[0] == 1, "single-layer encoder"
    p = dict(
        norm1_g=norm1_g[0].astype(F32), w_in=w_in[0].astype(BF16), conv_w=conv_w[0], conv_b=conv_b[0],
        filt_w1=filt_w1[0], filt_b1=filt_b1[0], filt_w2=filt_w2[0], filt_b2=filt_b2[0], filt_w3=filt_w3[0],
        filt_b3=filt_b3[0], filt_w4=filt_w4[0], filt_freq=filt_freq[0], hyena_bias=hyena_bias[0],
        w_branch_a=w_branch_a[0].astype(BF16), w_branch_b=w_branch_b[0].astype(BF16),
        w_out=w_out[0].astype(BF16), norm2_g=norm2_g[0], peer_wq=peer_wq[0].astype(BF16),
        peer_subkeys=peer_subkeys[0].reshape(2 * PEER_HEADS, N_KEYS, N_KEYS).astype(BF16),
        peer_u=(peer_u[0] * PEER_U_SCALE).astype(F8),
        peer_vt=peer_v[0].astype(F8).reshape(N_EXPERTS // PEER_EB, PEER_EB, D_MODEL).transpose(0, 2, 1),
        final_g=final_g,
    )
    return (_encoder(x_prompt, p), _encoder(x_sample, p))
```

```python
import functools
import math

import jax
import jax.numpy as jnp
from jax import lax
from jax.experimental import pallas as pl
from jax.experimental.pallas import tpu as pltpu

F32 = jnp.float32
BF16 = jnp.bfloat16
F8 = jnp.float8_e4m3fn
PEER_U_SCALE = 32.0
PEER_W_SCALE = 16.0

D_MODEL = 2048
HYENA_WIDTH = 1024
FILTER_EMB = 33
FILTER_EMB_PAD = 128
FILTER_HIDDEN = 64
DECAY_TARGET = 1e-2
FAST_DECAY_PCT = 0.3
SLOW_DECAY_PCT = 1.5
ATT_GROUPS = ((128, 1), (512, 4), (2048, 16))
N_GROUPS = 3
HEADS_PER_GROUP = 8
HEAD_DIM = 64
N_ATT_HEADS = N_GROUPS * HEADS_PER_GROUP
ATT_WIDTH = HEADS_PER_GROUP * HEAD_DIM
HYENA_COLS = 3 * HYENA_WIDTH
ATT_COLS = 3 * N_GROUPS * ATT_WIDTH
GATE_COLS = 2 * D_MODEL
PEER_HEADS = 8
N_KEYS = 128
N_EXPERTS = N_KEYS * N_KEYS
PEER_TOPK = 16
PEER_EB = 1024
EPS = 1e-6
NEG_INF = -1e30
ATT_HALF = 64
LANES = 128
VMEM_LIMIT = 56 * 1024 * 1024
HIGHEST = lax.Precision.HIGHEST


def _params(*sem, flags=None):
    return pltpu.CompilerParams(dimension_semantics=sem, vmem_limit_bytes=VMEM_LIMIT, flags=flags)


def _rms_norm_kernel(x_ref, g_ref, o_ref):
    x = x_ref[...]
    ms = jnp.mean(x * x, axis=-1, keepdims=True)
    o_ref[...] = (x * lax.rsqrt(ms + EPS) * g_ref[...]).astype(o_ref.dtype)


def rms_norm_bf16(x, g, tm=512):
    t, d = x.shape
    tm = min(tm, t)
    assert t % tm == 0
    return pl.pallas_call(
        _rms_norm_kernel,
        grid=(t // tm,),
        in_specs=[pl.BlockSpec((tm, d), lambda i: (i, 0)), pl.BlockSpec((1, d), lambda i: (0, 0))],
        out_specs=pl.BlockSpec((tm, d), lambda i: (i, 0)),
        out_shape=jax.ShapeDtypeStruct((t, d), BF16),
        compiler_params=_params("parallel"),
        name="rms_norm",
    )(x, g.reshape(1, d))


def _project_kernel(x_ref, w_ref, o_ref):
    o_ref[...] = jnp.dot(x_ref[...], w_ref[...], preferred_element_type=F32).astype(o_ref.dtype)


def project(xn, w, col0, ncols, out_dtype, tm=2048, tn=512):
    t, d = xn.shape
    tm = min(tm, t)
    assert t % tm == 0 and ncols % tn == 0 and col0 % tn == 0
    cb0 = col0 // tn
    return pl.pallas_call(
        _project_kernel,
        grid=(t // tm, ncols // tn),
        in_specs=[
            pl.BlockSpec((tm, d), lambda i, j: (i, 0)),
            pl.BlockSpec((d, tn), lambda i, j: (0, cb0 + j)),
        ],
        out_specs=pl.BlockSpec((tm, tn), lambda i, j: (i, j)),
        out_shape=jax.ShapeDtypeStruct((t, ncols), out_dtype),
        compiler_params=_params("parallel", "arbitrary"),
        name="project",
    )(xn, w)


def _project_classes_kernel(x_ref, w_ref, o_ref, acc_ref, *, dilation):
    which = pl.program_id(1)
    acc = jnp.dot(x_ref[...], w_ref[...], preferred_element_type=F32)
    nlane, tm, _ = acc_ref.shape
    for k in range(nlane):
        acc_ref[k] = acc[:, k * LANES:(k + 1) * LANES]
    rows = tm // dilation
    for part in range(3):
        @pl.when(which == part)
        def _(part=part):
            for r in range(dilation):
                for k in range(nlane):
                    piece = acc_ref[k] if dilation == 1 else acc_ref[k, pl.ds(r, rows, stride=dilation), :]
                    c0 = (3 * r + part) * nlane * LANES + k * LANES
                    o_ref[:, c0:c0 + LANES] = piece.astype(o_ref.dtype)


def project_classes(xn, w, group, tm=2048):
    t, d = xn.shape
    dilation = ATT_GROUPS[group][1]
    tm = min(tm, t)
    assert t % tm == 0 and tm % (8 * dilation) == 0
    cb0 = HYENA_COLS // ATT_WIDTH + group
    return pl.pallas_call(
        functools.partial(_project_classes_kernel, dilation=dilation),
        grid=(t // tm, 3),
        in_specs=[
            pl.BlockSpec((tm, d), lambda i, j: (i, 0)),
            pl.BlockSpec((d, ATT_WIDTH), lambda i, j: (0, cb0 + N_GROUPS * j)),
        ],
        out_specs=pl.BlockSpec((tm // dilation, dilation * 3 * ATT_WIDTH), lambda i, j: (i, 0)),
        out_shape=jax.ShapeDtypeStruct((t // dilation, dilation * 3 * ATT_WIDTH), BF16),
        scratch_shapes=[pltpu.VMEM((ATT_WIDTH // LANES, tm, LANES), F32)],
        compiler_params=_params("parallel", "arbitrary"),
        name="project_classes",
    )(xn, w)


def _filter_mlp_kernel(z_ref, t_ref, keep_ref, w1_ref, b1_ref, w2_ref, b2_ref, w3_ref, b3_ref, w4_ref, fr_ref,
                       dl_ref, h_ref, l1_ref):
    fr = fr_ref[...]
    h = jnp.sin(fr * (jnp.dot(z_ref[...], w1_ref[...], precision=HIGHEST, preferred_element_type=F32)
                      + b1_ref[...]))
    h = jnp.sin(fr * (jnp.dot(h, w2_ref[...], precision=HIGHEST, preferred_element_type=F32) + b2_ref[...]))
    h = jnp.sin(fr * (jnp.dot(h, w3_ref[...], precision=HIGHEST, preferred_element_type=F32) + b3_ref[...]))
    h = jnp.dot(h, w4_ref[...], precision=HIGHEST, preferred_element_type=F32)
    h = h * jnp.exp(-t_ref[...] * dl_ref[...])
    h_ref[...] = (h * keep_ref[...]).astype(h_ref.dtype)

    @pl.when(pl.program_id(0) == 0)
    def _():
        l1_ref[...] = jnp.zeros_like(l1_ref)

    l1_ref[...] += jnp.sum(jnp.abs(h), axis=0, keepdims=True)


def filter_mlp(seq, w1, b1, w2, b2, w3, b3, w4, freq, tl=512):
    t = jnp.linspace(0.0, 1.0, seq, dtype=F32)[:, None]
    bands = (FILTER_EMB - 1) // 2
    ang = 2.0 * math.pi * jnp.arange(seq, dtype=F32)[:, None] / seq
    fb = jnp.linspace(1e-4, bands - 1, bands, dtype=F32)[None, :]
    z = jnp.concatenate([t, jnp.cos(fb * ang), -jnp.sin(fb * ang)], axis=-1)
    z = jnp.pad(z, ((0, 0), (0, FILTER_EMB_PAD - FILTER_EMB)))
    by_row = lambda v: jnp.concatenate([v, v[0:1], v[1:][::-1]], axis=0)
    keep = (jnp.arange(2 * seq) != seq).astype(F32)[:, None]
    w1p = jnp.pad(w1.astype(F32), ((0, FILTER_EMB_PAD - FILTER_EMB), (0, 0)))
    min_decay = math.log(DECAY_TARGET) / SLOW_DECAY_PCT
    max_decay = math.log(DECAY_TARGET) / FAST_DECAY_PCT
    deltas = jnp.abs(jnp.linspace(min_decay, max_decay, HYENA_WIDTH, dtype=F32))[None, :]
    tl = min(tl, seq)
    assert seq % tl == 0
    c = HYENA_WIDTH
    hid = FILTER_HIDDEN
    per_dir = seq // tl
    full = lambda r, cc: pl.BlockSpec((r, cc), lambda i: (0, 0))
    return pl.pallas_call(
        _filter_mlp_kernel,
        grid=(2 * per_dir,),
        in_specs=[
            pl.BlockSpec((tl, FILTER_EMB_PAD), lambda i: (i, 0)),
            pl.BlockSpec((tl, 1), lambda i: (i, 0)),
            pl.BlockSpec((tl, 1), lambda i: (i, 0)),
            full(FILTER_EMB_PAD, hid), full(1, hid), full(hid, hid), full(1, hid), full(hid, hid),
            full(1, hid), pl.BlockSpec((hid, c), lambda i: (0, i // per_dir)), full(1, hid), full(1, c),
        ],
        out_specs=[pl.BlockSpec((tl, c), lambda i: (i, 0)), pl.BlockSpec((1, c), lambda i: (0, 0))],
        out_shape=[jax.ShapeDtypeStruct((2 * seq, c), BF16), jax.ShapeDtypeStruct((1, c), F32)],
        compiler_params=_params("arbitrary"),
        name="filter_mlp",
    )(by_row(z), by_row(t), keep, w1p, b1.reshape(1, hid).astype(F32), w2.astype(F32), b2.reshape(1, hid).astype(F32),
      w3.astype(F32), b3.reshape(1, hid).astype(F32), w4.astype(F32), freq.reshape(1, hid).astype(F32), deltas)


def _fft_split(n):
    lg = int(math.log2(n))
    assert 1 << lg == n
    n1 = 1 << ((lg + 1) // 2)
    return n1, n // n1


def _slow_dft_tables(n1):
    a = jnp.arange(n1, dtype=jnp.int32)
    ang = ((a[:, None] * a[None, :]) % n1).astype(F32) * (2.0 * math.pi / n1)
    return jnp.cos(ang), jnp.sin(ang)


def _fast_dft_tables(n1, n2):
    n = n1 * n2
    k1 = jnp.arange(n1, dtype=jnp.int32)[:, None, None]
    k2 = jnp.arange(n2, dtype=jnp.int32)[None, :, None]
    f = jnp.arange(n2, dtype=jnp.int32)[None, None, :]
    ang = ((f * (k1 + n1 * k2)) % n).astype(F32) * (2.0 * math.pi / n)
    c, s = jnp.cos(ang), jnp.sin(ang)
    top = jnp.concatenate([c, s], axis=2)
    bot = jnp.concatenate([-s, c], axis=2)
    return jnp.concatenate([top, bot], axis=1)


def _left_matmul_kernel(f_ref, x_ref, o_ref):
    o = jnp.dot(f_ref[...], x_ref[...].astype(f_ref.dtype), preferred_element_type=F32)
    o_ref[...] = o.astype(o_ref.dtype)


def left_matmul(f, x, out_dtype, ln=4096):
    b, k, lanes = x.shape
    m = f.shape[0]
    ln = min(ln, lanes)
    assert lanes % ln == 0 and f.shape[1] == k
    return pl.pallas_call(
        _left_matmul_kernel,
        grid=(b, lanes // ln),
        in_specs=[pl.BlockSpec((m, k), lambda i, j: (0, 0)),
                  pl.BlockSpec((None, k, ln), lambda i, j: (i, 0, j))],
        out_specs=pl.BlockSpec((None, m, ln), lambda i, j: (i, 0, j)),
        out_shape=jax.ShapeDtypeStruct((b, m, lanes), out_dtype),
        compiler_params=_params("parallel", "parallel"),
        name="left_matmul",
    )(f, x)


def _spectrum_mid_kernel(g_ref, a_ref, sc_ref, o_ref):
    n2 = a_ref.shape[1]
    a = a_ref[...].reshape(2 * n2, a_ref.shape[2])
    x = jnp.dot(g_ref[...], a, preferred_element_type=F32) * sc_ref[...]
    o_ref[...] = x.reshape(o_ref.shape)


def spectrum_mid(g, a, scale, cb=512):
    _, n1, n2, c = a.shape
    cb = min(cb, c)
    return pl.pallas_call(
        _spectrum_mid_kernel,
        grid=(n1, c // cb),
        in_specs=[pl.BlockSpec((None, 2 * n2, 2 * n2), lambda i, j: (i, 0, 0)),
                  pl.BlockSpec((2, None, n2, cb), lambda i, j: (0, i, 0, j)),
                  pl.BlockSpec((1, cb), lambda i, j: (0, j))],
        out_specs=pl.BlockSpec((2, None, n2, cb), lambda i, j: (0, i, 0, j)),
        out_shape=jax.ShapeDtypeStruct(a.shape, F32),
        compiler_params=_params("parallel", "parallel"),
        name="spectrum_mid",
    )(g, a, scale)


def _conv_mid_kernel(g_ref, gi_ref, a_ref, k_ref, o_ref):
    bblk, _, n2, cb = a_ref.shape
    kr, ki = k_ref[0], k_ref[1]
    xs = [jnp.dot(g_ref[...], a_ref[bi].reshape(2 * n2, cb), preferred_element_type=F32) for bi in range(bblk)]
    ys = [jnp.concatenate([x[:n2] * kr - x[n2:] * ki, x[:n2] * ki + x[n2:] * kr], axis=0).astype(BF16)
          for x in xs]
    for bi, y in enumerate(ys):
        z = jnp.dot(gi_ref[...], y, preferred_element_type=F32)
        o_ref[bi] = z.reshape(2, n2, cb).astype(o_ref.dtype)


def conv_mid(g, gi, a, kf, block_bytes=2 << 20):
    b, _, n1, n2, c = a.shape
    bblk = max(1, min(b, block_bytes // (2 * n2 * c * 2)))
    assert b % bblk == 0
    return pl.pallas_call(
        _conv_mid_kernel,
        grid=(n1, b // bblk),
        in_specs=[pl.BlockSpec((None, 2 * n2, 2 * n2), lambda i, bb: (i, 0, 0)),
                  pl.BlockSpec((None, 2 * n2, 2 * n2), lambda i, bb: (i, 0, 0)),
                  pl.BlockSpec((bblk, 2, None, n2, c), lambda i, bb: (bb, 0, i, 0, 0)),
                  pl.BlockSpec((2, None, n2, c), lambda i, bb: (0, i, 0, 0))],
        out_specs=pl.BlockSpec((bblk, 2, None, n2, c), lambda i, bb: (bb, 0, i, 0, 0)),
        out_shape=jax.ShapeDtypeStruct(a.shape, BF16),
        compiler_params=_params("parallel", "parallel"),
        name="conv_mid",
    )(g, gi, a, kf)


def _conv_out_kernel(f_ref, z_ref, w_ref, x0_ref, bias_ref, o_ref):
    y = jnp.dot(f_ref[...], z_ref[...], preferred_element_type=F32)
    o_ref[...] = (x0_ref[...].astype(F32) * (y + w_ref[...].astype(F32) * bias_ref[...])).astype(o_ref.dtype)


def conv_out(f, z, w, x0, bias_l, ln=4096):
    b, k, lanes = z.shape
    m = f.shape[0]
    ln = min(ln, lanes)
    assert lanes % ln == 0 and bias_l.shape == (1, ln)
    return pl.pallas_call(
        _conv_out_kernel,
        grid=(b, lanes // ln),
        in_specs=[pl.BlockSpec((m, k), lambda i, j: (0, 0)),
                  pl.BlockSpec((None, k, ln), lambda i, j: (i, 0, j)),
                  pl.BlockSpec((None, m, ln), lambda i, j: (i, 0, j)),
                  pl.BlockSpec((None, m, ln), lambda i, j: (i, 0, j)),
                  pl.BlockSpec((1, ln), lambda i, j: (0, 0))],
        out_specs=pl.BlockSpec((None, m, ln), lambda i, j: (i, 0, j)),
        out_shape=jax.ShapeDtypeStruct((b, m, lanes), BF16),
        compiler_params=_params("parallel", "parallel"),
        name="conv_out",
    )(f, z, w, x0, bias_l)


def _hyena_pre_kernel(x0_ref, x1_ref, v_ref, w0_ref, w1_ref, wv_ref, b0_ref, b1_ref, bv_ref, x0o_ref, wo_ref):
    seq = x0_ref.shape[0]
    row = lax.broadcasted_iota(jnp.int32, x0_ref.shape, 0)
    first, last = row == 0, row == seq - 1

    def conv(u_ref, w_ref, b_ref):
        u = u_ref[...].astype(F32)
        prev = jnp.where(first, 0.0, pltpu.roll(u, 1, axis=0))
        nxt = jnp.where(last, 0.0, pltpu.roll(u, seq - 1, axis=0))
        return b_ref[...] + prev * w_ref[0:1, :] + u * w_ref[1:2, :] + nxt * w_ref[2:3, :]

    x0o_ref[...] = conv(x0_ref, w0_ref, b0_ref).astype(x0o_ref.dtype)
    wo_ref[...] = (conv(v_ref, wv_ref, bv_ref) * conv(x1_ref, w1_ref, b1_ref)).astype(wo_ref.dtype)


def hyena_pre(hproj, conv_w, conv_b):
    b, seq, _ = hproj.shape
    c = HYENA_WIDTH
    cb = max(LANES, min(c, (1 << 20) // seq))
    nb = c // cb
    u_spec = lambda g: pl.BlockSpec((None, seq, cb), lambda i, j: (i, 0, g * nb + j))
    w_spec = lambda g: pl.BlockSpec((3, cb), lambda i, j: (0, g * nb + j))
    b_spec = lambda g: pl.BlockSpec((1, cb), lambda i, j: (0, g * nb + j))
    o_spec = pl.BlockSpec((None, seq, cb), lambda i, j: (i, 0, j))
    cw = conv_w.astype(F32)
    cbias = conv_b.reshape(1, 3 * c).astype(F32)
    return pl.pallas_call(
        _hyena_pre_kernel,
        grid=(b, nb),
        in_specs=[u_spec(0), u_spec(1), u_spec(2), w_spec(0), w_spec(1), w_spec(2),
                  b_spec(0), b_spec(1), b_spec(2)],
        out_specs=[o_spec, o_spec],
        out_shape=[jax.ShapeDtypeStruct((b, seq, c), BF16)] * 2,
        compiler_params=_params("parallel", "parallel"),
        name="hyena_pre",
    )(hproj, hproj, hproj, cw, cw, cw, cbias, cbias, cbias)


def hyena_branch(hproj, conv_w, conv_b, fw1, fb1, fw2, fb2, fw3, fb3, fw4, freq, hyena_bias):
    b, seq, _ = hproj.shape
    c = HYENA_WIDTH
    n = 2 * seq
    n1, n2 = _fft_split(n)
    n1h = n1 // 2
    cos1, sin1 = _slow_dft_tables(n1)
    g = _fast_dft_tables(n1, n2)
    gi = jnp.swapaxes(g, 1, 2)

    k_full, l1 = filter_mlp(seq, fw1, fb1, fw2, fb2, fw3, fb3, fw4, freq)
    scale = 1.0 / ((l1 + EPS) * n)
    fa_full = jnp.concatenate([cos1, -sin1], axis=0)
    fa_full = fa_full.astype(BF16)
    g, gi = g.astype(BF16), gi.astype(BF16)
    ka = left_matmul(fa_full, k_full.reshape(1, n1, n2 * c), BF16)
    kf = spectrum_mid(g, ka.reshape(2, n1, n2, c), scale)

    x0c, w = hyena_pre(hproj, conv_w, conv_b)
    a = left_matmul(fa_full[:, :n1h], w.reshape(b, n1h, n2 * c), BF16)
    z = conv_mid(g, gi, a.reshape(b, 2, n1, n2, c), kf)
    fc = jnp.concatenate([cos1[:n1h], -sin1[:n1h]], axis=1).astype(BF16)
    ln = min(4096, n2 * c)
    bias_l = jnp.tile(hyena_bias.reshape(1, c).astype(F32), (1, ln // c))
    y = conv_out(fc, z.reshape(b, 2 * n1, n2 * c), w.reshape(b, n1h, n2 * c), x0c.reshape(b, n1h, n2 * c),
                 bias_l, ln=ln)
    return y.reshape(b, seq, c)


ATT_SUB = 128


def _dilated_attn_kernel(q_ref, kp_ref, kc_ref, kn_ref, vp_ref, vc_ref, vn_ref, o_ref, l_ref, *,
                         n_cls, dilation, slopes):
    tq = q_ref.shape[0]
    q0 = pl.program_id(1) * tq
    res = pl.program_id(2)
    sk = ATT_SUB + 2 * ATT_HALF
    scale = 1.0 / math.sqrt(HEAD_DIM)
    qi = lax.broadcasted_iota(jnp.int32, (ATT_SUB, sk), 0)
    kj = lax.broadcasted_iota(jnp.int32, (ATT_SUB, sk), 1) - ATT_HALF
    dist = jnp.abs(kj - qi)
    band = dist <= ATT_HALF
    adist = (dilation * dist).astype(F32)
    lane = lax.broadcasted_iota(jnp.int32, (sk, LANES), 1)
    low_k = lane < HEAD_DIM
    low_q = lax.broadcasted_iota(jnp.int32, (ATT_SUB, LANES), 1) < HEAD_DIM
    for sub in range(tq // ATT_SUB):
        r0 = sub * ATT_SUB
        kabs = q0 + r0 + kj
        mask = band & (kabs >= 0) & (kabs < n_cls)
        def keys(p_ref, c_ref, n_ref, cs):
            parts = []
            if r0 == 0:
                parts.append(p_ref[:, cs])
                parts.append(c_ref[0:min(tq, ATT_SUB + ATT_HALF), cs])
            else:
                parts.append(c_ref[r0 - ATT_HALF:min(tq, r0 + ATT_SUB + ATT_HALF), cs])
            if r0 + ATT_SUB + ATT_HALF > tq:
                parts.append(n_ref[:, cs])
            return jnp.concatenate(parts, axis=0)

        heads = [(pair, e) for pair in range(HEADS_PER_GROUP // 2) for e in range(2)]
        scores, values = [], []
        for pair, e in heads:
            cs = slice(pair * LANES, (pair + 1) * LANES)
            sel = low_k if e == 0 else ~low_k
            k = keys(kp_ref, kc_ref, kn_ref, cs)
            v = keys(vp_ref, vc_ref, vn_ref, cs)
            ke = jnp.where(sel, k, jnp.zeros_like(k))
            values.append(jnp.where(sel, v, jnp.zeros_like(v)))
            s = lax.dot_general(q_ref[r0:r0 + ATT_SUB, cs], ke, (((1,), (1,)), ((), ())),
                                preferred_element_type=F32) * scale
            scores.append(jnp.where(mask, s - slopes[2 * pair + e] * adist, NEG_INF))
        maxes = [jnp.max(s, axis=-1, keepdims=True) for s in scores]
        probs = [jnp.exp(s - m) for s, m in zip(scores, maxes)]
        sums = [jnp.sum(p, axis=-1, keepdims=True) for p in probs]
        outs = [jnp.dot(p.astype(BF16), v, preferred_element_type=F32) for p, v in zip(probs, values)]
        if dilation == 1:
            tok = slice(r0, r0 + ATT_SUB)
        else:
            tok = pl.ds(r0 * dilation + res, ATT_SUB, stride=dilation)
        for pair in range(HEADS_PER_GROUP // 2):
            h0, h1 = 2 * pair, 2 * pair + 1
            inv = 1.0 / jnp.where(low_q, sums[h0], sums[h1])
            o_ref[pair, tok, :] = (outs[h0] + outs[h1]) * inv
            l_ref[pair, tok, :] = jnp.where(low_q, maxes[h0] + jnp.log(sums[h0]), maxes[h1] + jnp.log(sums[h1]))


def dilated_attention(qkv, batch, group, out_block_bytes=4 << 20):
    window, dilation = ATT_GROUPS[group]
    assert window // (2 * dilation) == ATT_HALF
    n_cls = qkv.shape[0] // batch
    seq = n_cls * dilation
    tq = min(256, n_cls, max(ATT_SUB, out_block_bytes // (dilation * ATT_WIDTH * 4)))
    assert n_cls % tq == 0 and tq % ATT_SUB == 0
    hb = tq // ATT_HALF
    n_halo = n_cls // ATT_HALF
    slopes = tuple(2.0 ** (-8.0 * (group * HEADS_PER_GROUP + h + 1.0) / N_ATT_HEADS)
                   for h in range(HEADS_PER_GROUP))
    x = qkv.reshape(batch, n_cls, dilation * 3 * ATT_WIDTH)
    cur = lambda which: pl.BlockSpec((None, tq, ATT_WIDTH), lambda i, j, r: (i, j, 3 * r + which))
    prev = lambda which: pl.BlockSpec((None, ATT_HALF, ATT_WIDTH),
                                      lambda i, j, r: (i, jnp.maximum(j * hb - 1, 0), 3 * r + which))
    nxt = lambda which: pl.BlockSpec((None, ATT_HALF, ATT_WIDTH),
                                     lambda i, j, r: (i, jnp.minimum((j + 1) * hb, n_halo - 1), 3 * r + which))
    npair = ATT_WIDTH // LANES
    o_spec = pl.BlockSpec((None, npair, tq * dilation, LANES), lambda i, j, r: (i, 0, j, 0))
    return pl.pallas_call(
        functools.partial(_dilated_attn_kernel, n_cls=n_cls, dilation=dilation, slopes=slopes),
        grid=(batch, n_cls // tq, dilation),
        in_specs=[cur(0), prev(1), cur(1), nxt(1), prev(2), cur(2), nxt(2)],
        out_specs=[o_spec, o_spec],
        out_shape=[jax.ShapeDtypeStruct((batch, npair, seq, LANES), F32)] * 2,
        compiler_params=_params("parallel", "parallel", "arbitrary"),
        name="dilated_attn",
    )(x, x, x, x, x, x, x)


def _merge_kernel(x_ref, ga_ref, gb_ref, ya_ref, o0_ref, o1_ref, o2_ref, l0_ref, l1_ref, l2_ref,
                  wa_ref, wb_ref, wo_ref, g2_ref, x1_ref, xn_ref, xn8_ref):
    parts = []
    for pair in range(o0_ref.shape[0]):
        l0, l1, l2 = l0_ref[pair], l1_ref[pair], l2_ref[pair]
        m = jnp.maximum(jnp.maximum(l0, l1), l2)
        e0, e1, e2 = jnp.exp(l0 - m), jnp.exp(l1 - m), jnp.exp(l2 - m)
        yb = (e0 * o0_ref[pair] + e1 * o1_ref[pair] + e2 * o2_ref[pair]) / (e0 + e1 + e2)
        parts.append(yb.astype(BF16))
    pa = jnp.dot(ya_ref[...].astype(BF16), wa_ref[...], preferred_element_type=F32)
    pb = jnp.dot(jnp.concatenate(parts, axis=1), wb_ref[...], preferred_element_type=F32)
    merged = (jax.nn.sigmoid(ga_ref[...].astype(F32)) * pa
              + jax.nn.sigmoid(gb_ref[...].astype(F32)) * pb)
    x1 = x_ref[...] + jnp.dot(merged.astype(BF16), wo_ref[...], preferred_element_type=F32)
    x1_ref[...] = x1
    ms = jnp.mean(x1 * x1, axis=-1, keepdims=True)
    xn = x1 * lax.rsqrt(ms + EPS) * g2_ref[...]
    xn_ref[...] = xn.astype(xn_ref.dtype)
    xn8_ref[...] = xn.astype(xn8_ref.dtype)


def merge(x, glog, ya, outs, lses, wa, wb, wo, g2, tm=256):
    t, d = x.shape
    _, npair, seq, _ = outs[0].shape
    tm = min(tm, seq)
    assert seq % tm == 0
    tiles = seq // tm
    row = lambda c, blk=0: pl.BlockSpec((tm, c), lambda i: (i, blk))
    const = lambda r, c: pl.BlockSpec((r, c), lambda i: (0, 0), pipeline_mode=pl.Buffered(1))
    att = pl.BlockSpec((None, npair, tm, LANES), lambda i: (i // tiles, 0, i % tiles, 0))
    aw = ATT_WIDTH
    return pl.pallas_call(
        _merge_kernel,
        grid=(t // tm,),
        in_specs=[row(d), row(d, 0), row(d, 1), row(HYENA_WIDTH), att, att, att, att, att, att,
                  const(HYENA_WIDTH, d), const(aw, d), const(d, d), const(1, d)],
        out_specs=[row(d), row(d), row(d)],
        out_shape=[jax.ShapeDtypeStruct((t, d), F32), jax.ShapeDtypeStruct((t, d), BF16),
                   jax.ShapeDtypeStruct((t, d), F8)],
        compiler_params=_params("parallel"),
        name="merge",
    )(x, glog, glog, ya, *outs, *lses, wa, wb, wo, g2.reshape(1, d).astype(F32))


def _descending_max(curs, k, emit):
    curs = list(curs)
    for it in range(k):
        for c, cur in enumerate(curs):
            m = jnp.max(cur, axis=0, keepdims=True)
            emit(c, it, m)
            if it + 1 < k:
                curs[c] = jnp.where(cur == m, -jnp.inf, cur)


PEER_LOCKSTEP_HEADS = 2


def _peer_scores_kernel(xn_ref, wq_ref, sk_ref, g1_ref, g2_ref, top_ref, s_ref):
    k = PEER_TOPK
    sub = 8
    nh = PEER_HEADS * N_KEYS
    q = jnp.dot(xn_ref[...], wq_ref[...], preferred_element_type=F32).astype(BF16)
    low_rank = lax.broadcasted_iota(jnp.int32, (sub, LANES), 0) < 4
    for tc in range(xn_ref.shape[0] // LANES):
        cols = slice(tc * LANES, (tc + 1) * LANES)
        for h0 in range(0, PEER_HEADS, PEER_LOCKSTEP_HEADS):
            heads = range(h0, h0 + PEER_LOCKSTEP_HEADS)
            chains = [(hh, p) for hh in range(PEER_LOCKSTEP_HEADS) for p in range(2)]
            for c, (hh, p) in enumerate(chains):
                hp = 2 * (h0 + hh) + p
                qs = q[cols, hp * N_KEYS:(hp + 1) * N_KEYS]
                s_ref[c] = lax.dot_general(sk_ref[hp], qs, (((1,), (1,)), ((), ())), preferred_element_type=F32)

            def put_top(c, it, m):
                top_ref[c, it:it + 1, :] = m

            _descending_max([s_ref[c] for c in range(len(chains))], k, put_top)

            cands = []
            for hh in range(PEER_LOCKSTEP_HEADS):
                v1, v2 = top_ref.at[2 * hh], top_ref.at[2 * hh + 1]
                tiles = [v1[r:r + 1, :] + v2[0:sub, :] for r in range(4)]
                tiles += [v1[0:sub, :] + v2[r:r + 1, :] for r in range(4)]
                tiles += [v1[sub:k, :] + v2[0:1, :], v1[0:1, :] + v2[sub:k, :]]
                cands.append(jnp.concatenate(tiles, axis=0))
            repeated = [None] * 4 + [low_rank] * 4 + [None] * 2
            taus = [None] * PEER_LOCKSTEP_HEADS

            def put_tau(c, it, m):
                taus[c] = m

            _descending_max(cands, k, put_tau)
            for hh, h in enumerate(heads):
                rows = slice(h * N_KEYS, (h + 1) * N_KEYS)
                rows_e = slice(nh + h * N_KEYS, nh + (h + 1) * N_KEYS)
                v1, v2 = top_ref.at[2 * hh], top_ref.at[2 * hh + 1]
                tau = taus[hh]
                best = v1[0:1, :] + v2[0:1, :]
                zsum = None
                for ti, dup in enumerate(repeated):
                    tile = cands[hh][ti * sub:(ti + 1) * sub, :]
                    keep = tile >= tau
                    if dup is not None:
                        keep = keep & ~dup
                    part = jnp.sum(jnp.where(keep, jnp.exp(tile - best), 0.0), axis=0, keepdims=True)
                    zsum = part if zsum is None else zsum + part
                inv_z = 1.0 / zsum
                v2_all = v2[...]
                e2_top = jnp.exp(v2_all - v2[0:1, :]) * inv_z
                s1 = s_ref[2 * hh]
                theta = jnp.full(s1.shape, jnp.inf, F32)
                for r in range(k):
                    c = v1[r:r + 1, :] + v2_all
                    theta_r = jnp.min(jnp.where(c >= tau, e2_top, jnp.inf), axis=0, keepdims=True)
                    theta = jnp.where(s1 == v1[r:r + 1, :], theta_r, theta)
                g1_ref[rows, cols] = theta
                g1_ref[rows_e, cols] = jnp.exp(s1 - v1[0:1, :])
                g2_ref[tc, rows, :] = jnp.exp(s_ref[2 * hh + 1] - v2[0:1, :]) * inv_z


def peer_scores(xn, wq, subkeys, tm=256):
    t, d = xn.shape
    tm = min(tm, t)
    assert t % tm == 0 and tm % LANES == 0
    rows = 2 * PEER_HEADS * N_KEYS
    return pl.pallas_call(
        _peer_scores_kernel,
        grid=(t // tm,),
        in_specs=[pl.BlockSpec((tm, d), lambda i: (i, 0)),
                  pl.BlockSpec((d, rows), lambda i: (0, 0), pipeline_mode=pl.Buffered(1)),
                  pl.BlockSpec((2 * PEER_HEADS, N_KEYS, N_KEYS), lambda i: (0, 0, 0))],
        out_specs=[pl.BlockSpec((rows, tm), lambda i: (0, i)),
                   pl.BlockSpec((tm // LANES, rows // 2, LANES), lambda i: (i, 0, 0))],
        out_shape=[jax.ShapeDtypeStruct((rows, t), F32),
                   jax.ShapeDtypeStruct((t // LANES, rows // 2, LANES), F32)],
        scratch_shapes=[pltpu.VMEM((2 * PEER_LOCKSTEP_HEADS, PEER_TOPK, LANES), F32),
                        pltpu.VMEM((2 * PEER_LOCKSTEP_HEADS, N_KEYS, LANES), F32)],
        compiler_params=_params("parallel"),
        name="peer_scores",
    )(xn, wq, subkeys)


GELU_C0 = math.sqrt(2.0 / math.pi)
GELU_C1 = GELU_C0 * 0.044715


def _peer_experts_kernel(xn_ref, g1_ref, g2_ref, u_ref, vt_ref, x1_ref, g_ref, y_ref, acc_ref, a_ref, wa_ref):
    j = pl.program_id(1)
    nblk = pl.num_programs(1) - 2
    _, nchunk, eb, _ = a_ref.shape
    tm = nchunk * LANES
    a_even, a_odd = a_ref.at[0], a_ref.at[1]
    wa_even, wa_odd = wa_ref.at[0], wa_ref.at[1]
    nh = PEER_HEADS * N_KEYS
    per = eb // N_KEYS
    half = tm // 2

    mrows = 256
    arows = 256

    def activations(a_ref, m, c):
        rows = slice(m * arows, (m + 1) * arows)
        cols = slice(c * half, (c + 1) * half)
        a = lax.dot_general(u_ref[rows, :], xn_ref[cols, :], (((1,), (1,)), ((), ())),
                            preferred_element_type=F32)
        for k in range(half // LANES):
            a_ref[c * (half // LANES) + k, rows, :] = a[:, k * LANES:(k + 1) * LANES]

    def gate_rows(block, ii):
        key1 = block * per + ii
        fold = 0.5 * PEER_W_SCALE / PEER_U_SCALE
        return ([g1_ref[pl.ds(h * N_KEYS + key1, 1), :] for h in range(PEER_HEADS)],
                [g1_ref[pl.ds(nh + h * N_KEYS + key1, 1), :] * fold for h in range(PEER_HEADS)])

    def gates(a_ref, wa_ref, theta_rows, e1_rows, ii, tc):
        rows = slice(ii * N_KEYS, (ii + 1) * N_KEYS)
        cols = slice(tc * LANES, (tc + 1) * LANES)
        wsum = None
        for h in range(PEER_HEADS):
            e2 = g2_ref[tc, h * N_KEYS:(h + 1) * N_KEYS, :]
            term = e1_rows[h][:, cols] * jnp.where(e2 >= theta_rows[h][:, cols], e2, 0.0)
            wsum = term if wsum is None else wsum + term
        a = a_ref[tc, rows, :]
        inner = a * (GELU_C0 / PEER_U_SCALE + (GELU_C1 / PEER_U_SCALE ** 3) * (a * a))
        wa_ref[rows, cols] = ((a * wsum) * (1.0 + jnp.tanh(inner))).astype(wa_ref.dtype)

    def outputs(wa_ref, r, c):
        rows = slice(r * mrows, (r + 1) * mrows)
        cols = slice(c * half, (c + 1) * half)
        acc_ref[rows, cols] += jnp.dot(vt_ref[rows, :], wa_ref[:, cols], preferred_element_type=F32)

    def step(act, gate, out):
        vec, mxu_a, mxu_o = [], [], []
        rows_cache = {}

        def gate_piece(ii, tc):
            if ii not in rows_cache:
                rows_cache[ii] = gate_rows(j - 1, ii)
            gates(*gate, *rows_cache[ii], ii, tc)

        if gate:
            vec = [functools.partial(gate_piece, ii, tc) for ii in range(per) for tc in range(nchunk)]
        if act is not None:
            mxu_a = [functools.partial(activations, act, m, c) for m in range(eb // arows) for c in range(2)]
        if out is not None:
            mxu_o = [functools.partial(outputs, out, r, c) for c in range(2) for r in range(acc_ref.shape[0] // mrows)]
        n = max(len(vec), len(mxu_a), len(mxu_o))
        for k in range(n):
            for stage in (vec, mxu_a, mxu_o):
                for piece in stage[k * len(stage) // n:(k + 1) * len(stage) // n]:
                    piece()

    steady = (j > 1) & (j < nblk)

    @pl.when(j == 0)
    def _():
        acc_ref[...] = jnp.zeros_like(acc_ref)
        step(a_even, None, None)

    @pl.when(j == 1)
    def _():
        step(a_odd, (a_even, wa_even), None)

    @pl.when(steady & (j % 2 == 0))
    def _():
        step(a_even, (a_odd, wa_odd), wa_even)

    @pl.when(steady & (j % 2 == 1))
    def _():
        step(a_odd, (a_even, wa_even), wa_odd)

    @pl.when(j == nblk)
    def _():
        step(None, (a_odd, wa_odd), wa_even)

    @pl.when(j == nblk + 1)
    def _():
        step(None, None, wa_odd)
        x2 = x1_ref[...] + acc_ref[...].T * (1.0 / PEER_W_SCALE)
        ms = jnp.mean(x2 * x2, axis=-1, keepdims=True)
        y_ref[...] = x2 * lax.rsqrt(ms + EPS) * g_ref[...]


def peer_experts(xn, g1, g2, u, vt, x1, final_g, tm=512):
    t, d = xn.shape
    eb = vt.shape[2]
    tm = min(tm, t)
    assert t % tm == 0 and N_EXPERTS % (2 * eb) == 0 and eb % N_KEYS == 0 and tm % (2 * LANES) == 0
    rows = 2 * PEER_HEADS * N_KEYS
    nblk = N_EXPERTS // eb
    return pl.pallas_call(
        _peer_experts_kernel,
        grid=(t // tm, nblk + 2),
        in_specs=[pl.BlockSpec((tm, d), lambda i, j: (i, 0)),
                  pl.BlockSpec((rows, tm), lambda i, j: (0, i)),
                  pl.BlockSpec((tm // LANES, rows // 2, LANES), lambda i, j: (i, 0, 0)),
                  pl.BlockSpec((eb, d), lambda i, j: (jnp.minimum(j, nblk - 1), 0)),
                  pl.BlockSpec((None, d, eb), lambda i, j: (jnp.clip(j - 2, 0, nblk - 1), 0, 0)),
                  pl.BlockSpec((tm, d), lambda i, j: (i, 0), pipeline_mode=pl.Buffered(1)),
                  pl.BlockSpec((1, d), lambda i, j: (0, 0))],
        out_specs=pl.BlockSpec((tm, d), lambda i, j: (i, 0)),
        out_shape=jax.ShapeDtypeStruct((t, d), F32),
        scratch_shapes=[pltpu.VMEM((d, tm), F32), pltpu.VMEM((2, tm // LANES, eb, LANES), F32),
                        pltpu.VMEM((2, eb, tm), F8)],
        compiler_params=_params("parallel", "arbitrary"),
        name="peer_experts",
    )(xn, g1, g2, u, vt, x1, final_g.reshape(1, d).astype(F32))


def _encoder(x, p):
    b, seq, d = x.shape
    t = b * seq
    xt = x.reshape(t, d)
    xn1 = rms_norm_bf16(xt, p["norm1_g"])
    hproj = project(xn1, p["w_in"], 0, HYENA_COLS, BF16)
    glog = project(xn1, p["w_in"], HYENA_COLS + ATT_COLS, GATE_COLS, BF16)
    ya = hyena_branch(hproj.reshape(b, seq, HYENA_COLS), p["conv_w"], p["conv_b"], p["filt_w1"], p["filt_b1"],
                      p["filt_w2"], p["filt_b2"], p["filt_w3"], p["filt_b3"], p["filt_w4"], p["filt_freq"],
                      p["hyena_bias"])
    att = [dilated_attention(project_classes(xn1, p["w_in"], g), b, g) for g in range(N_GROUPS)]
    x1, xn2, xn2_f8 = merge(xt, glog, ya.reshape(t, HYENA_WIDTH), [o for o, _ in att], [l for _, l in att],
                            p["w_branch_a"], p["w_branch_b"], p["w_out"], p["norm2_g"])
    g1, g2 = peer_scores(xn2, p["peer_wq"], p["peer_subkeys"])
    y = peer_experts(xn2_f8, g1, g2, p["peer_u"], p["peer_vt"], x1, p["final_g"])
    return y.reshape(b, seq, d)


def kernel(x_prompt, x_sample, norm1_g, w_in, conv_w, conv_b, filt_w1, filt_b1, filt_w2, filt_b2, filt_w3,
           filt_b3, filt_w4, filt_freq, hyena_bias, w_branch_a, w_branch_b, w_out, norm2_g, peer_wq,
           peer_subkeys, peer_u, peer_v, final_g):
    assert norm1_g.shape[0] == 1, "single-layer encoder"
    p = dict(
        norm1_g=norm1_g[0].astype(F32), w_in=w_in[0].astype(BF16), conv_w=conv_w[0], conv_b=conv_b[0],
        filt_w1=filt_w1[0], filt_b1=filt_b1[0], filt_w2=filt_w2[0], filt_b2=filt_b2[0], filt_w3=filt_w3[0],
        filt_b3=filt_b3[0], filt_w4=filt_w4[0], filt_freq=filt_freq[0], hyena_bias=hyena_bias[0],
        w_branch_a=w_branch_a[0].astype(BF16), w_branch_b=w_branch_b[0].astype(BF16),
        w_out=w_out[0].astype(BF16), norm2_g=norm2_g[0], peer_wq=peer_wq[0].astype(BF16),
        peer_subkeys=peer_subkeys[0].reshape(2 * PEER_HEADS, N_KEYS, N_KEYS).astype(BF16),
        peer_u=(peer_u[0] * PEER_U_SCALE).astype(F8),
        peer_vt=peer_v[0].astype(F8).reshape(N_EXPERTS // PEER_EB, PEER_EB, D_MODEL).transpose(0, 2, 1),
        final_g=final_g,
    )
    return (_encoder(x_prompt, p), _encoder(x_sample, p))
```

```python
import functools
import math

import jax
import jax.numpy as jnp
from jax import lax
from jax.experimental import pallas as pl
from jax.experimental.pallas import tpu as pltpu

F32 = jnp.float32
BF16 = jnp.bfloat16
F8 = jnp.float8_e4m3fn
PEER_U_SCALE = 32.0
PEER_W_SCALE = 16.0

D_MODEL = 2048
HYENA_WIDTH = 1024
FILTER_EMB = 33
FILTER_EMB_PAD = 128
FILTER_HIDDEN = 64
DECAY_TARGET = 1e-2
FAST_DECAY_PCT = 0.3
SLOW_DECAY_PCT = 1.5
ATT_GROUPS = ((128, 1), (512, 4), (2048, 16))
N_GROUPS = 3
HEADS_PER_GROUP = 8
HEAD_DIM = 64
N_ATT_HEADS = N_GROUPS * HEADS_PER_GROUP
ATT_WIDTH = HEADS_PER_GROUP * HEAD_DIM
HYENA_COLS = 3 * HYENA_WIDTH
ATT_COLS = 3 * N_GROUPS * ATT_WIDTH
GATE_COLS = 2 * D_MODEL
PEER_HEADS = 8
N_KEYS = 128
N_EXPERTS = N_KEYS * N_KEYS
PEER_TOPK = 16
PEER_EB = 1024
EPS = 1e-6
NEG_INF = -1e30
ATT_HALF = 64
LANES = 128
VMEM_LIMIT = 56 * 1024 * 1024
HIGHEST = lax.Precision.HIGHEST


def _params(*sem, flags=None):
    return pltpu.CompilerParams(dimension_semantics=sem, vmem_limit_bytes=VMEM_LIMIT, flags=flags)


def _rms_norm_kernel(x_ref, g_ref, o_ref):
    x = x_ref[...]
    ms = jnp.mean(x * x, axis=-1, keepdims=True)
    o_ref[...] = (x * lax.rsqrt(ms + EPS) * g_ref[...]).astype(o_ref.dtype)


def rms_norm_bf16(x, g, tm=512):
    t, d = x.shape
    tm = min(tm, t)
    assert t % tm == 0
    return pl.pallas_call(
        _rms_norm_kernel,
        grid=(t // tm,),
        in_specs=[pl.BlockSpec((tm, d), lambda i: (i, 0)), pl.BlockSpec((1, d), lambda i: (0, 0))],
        out_specs=pl.BlockSpec((tm, d), lambda i: (i, 0)),
        out_shape=jax.ShapeDtypeStruct((t, d), BF16),
        compiler_params=_params("parallel"),
        name="rms_norm",
    )(x, g.reshape(1, d))


def _project_kernel(x_ref, w_ref, o_ref):
    o_ref[...] = jnp.dot(x_ref[...], w_ref[...], preferred_element_type=F32).astype(o_ref.dtype)


def project(xn, w, col0, ncols, out_dtype, tm=2048, tn=512):
    t, d = xn.shape
    tm = min(tm, t)
    assert t % tm == 0 and ncols % tn == 0 and col0 % tn == 0
    cb0 = col0 // tn
    return pl.pallas_call(
        _project_kernel,
        grid=(t // tm, ncols // tn),
        in_specs=[
            pl.BlockSpec((tm, d), lambda i, j: (i, 0)),
            pl.BlockSpec((d, tn), lambda i, j: (0, cb0 + j)),
        ],
        out_specs=pl.BlockSpec((tm, tn), lambda i, j: (i, j)),
        out_shape=jax.ShapeDtypeStruct((t, ncols), out_dtype),
        compiler_params=_params("parallel", "arbitrary"),
        name="project",
    )(xn, w)


def _project_classes_kernel(x_ref, w_ref, o_ref, acc_ref, *, dilation):
    which = pl.program_id(1)
    acc = jnp.dot(x_ref[...], w_ref[...], preferred_element_type=F32)
    nlane, tm, _ = acc_ref.shape
    for k in range(nlane):
        acc_ref[k] = acc[:, k * LANES:(k + 1) * LANES]
    rows = tm // dilation
    for part in range(3):
        @pl.when(which == part)
        def _(part=part):
            for r in range(dilation):
                for k in range(nlane):
                    piece = acc_ref[k] if dilation == 1 else acc_ref[k, pl.ds(r, rows, stride=dilation), :]
                    c0 = (3 * r + part) * nlane * LANES + k * LANES
                    o_ref[:, c0:c0 + LANES] = piece.astype(o_ref.dtype)


def project_classes(xn, w, group, tm=2048):
    t, d = xn.shape
    dilation = ATT_GROUPS[group][1]
    tm = min(tm, t)
    assert t % tm == 0 and tm % (8 * dilation) == 0
    cb0 = HYENA_COLS // ATT_WIDTH + group
    return pl.pallas_call(
        functools.partial(_project_classes_kernel, dilation=dilation),
        grid=(t // tm, 3),
        in_specs=[
            pl.BlockSpec((tm, d), lambda i, j: (i, 0)),
            pl.BlockSpec((d, ATT_WIDTH), lambda i, j: (0, cb0 + N_GROUPS * j)),
        ],
        out_specs=pl.BlockSpec((tm // dilation, dilation * 3 * ATT_WIDTH), lambda i, j: (i, 0)),
        out_shape=jax.ShapeDtypeStruct((t // dilation, dilation * 3 * ATT_WIDTH), BF16),
        scratch_shapes=[pltpu.VMEM((ATT_WIDTH // LANES, tm, LANES), F32)],
        compiler_params=_params("parallel", "arbitrary"),
        name="project_classes",
    )(xn, w)


def _filter_mlp_kernel(z_ref, t_ref, keep_ref, w1_ref, b1_ref, w2_ref, b2_ref, w3_ref, b3_ref, w4_ref, fr_ref,
                       dl_ref, h_ref, l1_ref):
    fr = fr_ref[...]
    h = jnp.sin(fr * (jnp.dot(z_ref[...], w1_ref[...], precision=HIGHEST, preferred_element_type=F32)
                      + b1_ref[...]))
    h = jnp.sin(fr * (jnp.dot(h, w2_ref[...], precision=HIGHEST, preferred_element_type=F32) + b2_ref[...]))
    h = jnp.sin(fr * (jnp.dot(h, w3_ref[...], precision=HIGHEST, preferred_element_type=F32) + b3_ref[...]))
    h = jnp.dot(h, w4_ref[...], precision=HIGHEST, preferred_element_type=F32)
    h = h * jnp.exp(-t_ref[...] * dl_ref[...])
    h_ref[...] = (h * keep_ref[...]).astype(h_ref.dtype)

    @pl.when(pl.program_id(0) == 0)
    def _():
        l1_ref[...] = jnp.zeros_like(l1_ref)

    l1_ref[...] += jnp.sum(jnp.abs(h), axis=0, keepdims=True)


def filter_mlp(seq, w1, b1, w2, b2, w3, b3, w4, freq, tl=512):
    t = jnp.linspace(0.0, 1.0, seq, dtype=F32)[:, None]
    bands = (FILTER_EMB - 1) // 2
    ang = 2.0 * math.pi * jnp.arange(seq, dtype=F32)[:, None] / seq
    fb = jnp.linspace(1e-4, bands - 1, bands, dtype=F32)[None, :]
    z = jnp.concatenate([t, jnp.cos(fb * ang), -jnp.sin(fb * ang)], axis=-1)
    z = jnp.pad(z, ((0, 0), (0, FILTER_EMB_PAD - FILTER_EMB)))
    by_row = lambda v: jnp.concatenate([v, v[0:1], v[1:][::-1]], axis=0)
    keep = (jnp.arange(2 * seq) != seq).astype(F32)[:, None]
    w1p = jnp.pad(w1.astype(F32), ((0, FILTER_EMB_PAD - FILTER_EMB), (0, 0)))
    min_decay = math.log(DECAY_TARGET) / SLOW_DECAY_PCT
    max_decay = math.log(DECAY_TARGET) / FAST_DECAY_PCT
    deltas = jnp.abs(jnp.linspace(min_decay, max_decay, HYENA_WIDTH, dtype=F32))[None, :]
    tl = min(tl, seq)
    assert seq % tl == 0
    c = HYENA_WIDTH
    hid = FILTER_HIDDEN
    per_dir = seq // tl
    full = lambda r, cc: pl.BlockSpec((r, cc), lambda i: (0, 0))
    return pl.pallas_call(
        _filter_mlp_kernel,
        grid=(2 * per_dir,),
        in_specs=[
            pl.BlockSpec((tl, FILTER_EMB_PAD), lambda i: (i, 0)),
            pl.BlockSpec((tl, 1), lambda i: (i, 0)),
            pl.BlockSpec((tl, 1), lambda i: (i, 0)),
            full(FILTER_EMB_PAD, hid), full(1, hid), full(hid, hid), full(1, hid), full(hid, hid),
            full(1, hid), pl.BlockSpec((hid, c), lambda i: (0, i // per_dir)), full(1, hid), full(1, c),
        ],
        out_specs=[pl.BlockSpec((tl, c), lambda i: (i, 0)), pl.BlockSpec((1, c), lambda i: (0, 0))],
        out_shape=[jax.ShapeDtypeStruct((2 * seq, c), BF16), jax.ShapeDtypeStruct((1, c), F32)],
        compiler_params=_params("arbitrary"),
        name="filter_mlp",
    )(by_row(z), by_row(t), keep, w1p, b1.reshape(1, hid).astype(F32), w2.astype(F32), b2.reshape(1, hid).astype(F32),
      w3.astype(F32), b3.reshape(1, hid).astype(F32), w4.astype(F32), freq.reshape(1, hid).astype(F32), deltas)


def _fft_split(n):
    lg = int(math.log2(n))
    assert 1 << lg == n
    n1 = 1 << ((lg + 1) // 2)
    return n1, n // n1


def _slow_dft_tables(n1):
    a = jnp.arange(n1, dtype=jnp.int32)
    ang = ((a[:, None] * a[None, :]) % n1).astype(F32) * (2.0 * math.pi / n1)
    return jnp.cos(ang), jnp.sin(ang)


def _fast_dft_tables(n1, n2):
    n = n1 * n2
    k1 = jnp.arange(n1, dtype=jnp.int32)[:, None, None]
    k2 = jnp.arange(n2, dtype=jnp.int32)[None, :, None]
    f = jnp.arange(n2, dtype=jnp.int32)[None, None, :]
    ang = ((f * (k1 + n1 * k2)) % n).astype(F32) * (2.0 * math.pi / n)
    c, s = jnp.cos(ang), jnp.sin(ang)
    top = jnp.concatenate([c, s], axis=2)
    bot = jnp.concatenate([-s, c], axis=2)
    return jnp.concatenate([top, bot], axis=1)


def _left_matmul_kernel(f_ref, x_ref, o_ref):
    o = jnp.dot(f_ref[...], x_ref[...].astype(f_ref.dtype), preferred_element_type=F32)
    o_ref[...] = o.astype(o_ref.dtype)


def left_matmul(f, x, out_dtype, ln=4096):
    b, k, lanes = x.shape
    m = f.shape[0]
    ln = min(ln, lanes)
    assert lanes % ln == 0 and f.shape[1] == k
    return pl.pallas_call(
        _left_matmul_kernel,
        grid=(b, lanes // ln),
        in_specs=[pl.BlockSpec((m, k), lambda i, j: (0, 0)),
                  pl.BlockSpec((None, k, ln), lambda i, j: (i, 0, j))],
        out_specs=pl.BlockSpec((None, m, ln), lambda i, j: (i, 0, j)),
        out_shape=jax.ShapeDtypeStruct((b, m, lanes), out_dtype),
        compiler_params=_params("parallel", "parallel"),
        name="left_matmul",
    )(f, x)


def _spectrum_mid_kernel(g_ref, a_ref, sc_ref, o_ref):
    n2 = a_ref.shape[1]
    a = a_ref[...].reshape(2 * n2, a_ref.shape[2])
    x = jnp.dot(g_ref[...], a, preferred_element_type=F32) * sc_ref[...]
    o_ref[...] = x.reshape(o_ref.shape)


def spectrum_mid(g, a, scale, cb=512):
    _, n1, n2, c = a.shape
    cb = min(cb, c)
    return pl.pallas_call(
        _spectrum_mid_kernel,
        grid=(n1, c // cb),
        in_specs=[pl.BlockSpec((None, 2 * n2, 2 * n2), lambda i, j: (i, 0, 0)),
                  pl.BlockSpec((2, None, n2, cb), lambda i, j: (0, i, 0, j)),
                  pl.BlockSpec((1, cb), lambda i, j: (0, j))],
        out_specs=pl.BlockSpec((2, None, n2, cb), lambda i, j: (0, i, 0, j)),
        out_shape=jax.ShapeDtypeStruct(a.shape, F32),
        compiler_params=_params("parallel", "parallel"),
        name="spectrum_mid",
    )(g, a, scale)


def _conv_mid_kernel(g_ref, gi_ref, a_ref, k_ref, o_ref):
    bblk, _, n2, cb = a_ref.shape
    kr, ki = k_ref[0], k_ref[1]
    xs = [jnp.dot(g_ref[...], a_ref[bi].reshape(2 * n2, cb), preferred_element_type=F32) for bi in range(bblk)]
    ys = [jnp.concatenate([x[:n2] * kr - x[n2:] * ki, x[:n2] * ki + x[n2:] * kr], axis=0).astype(BF16)
          for x in xs]
    for bi, y in enumerate(ys):
        z = jnp.dot(gi_ref[...], y, preferred_element_type=F32)
        o_ref[bi] = z.reshape(2, n2, cb).astype(o_ref.dtype)


def conv_mid(g, gi, a, kf, block_bytes=2 << 20):
    b, _, n1, n2, c = a.shape
    bblk = max(1, min(b, block_bytes // (2 * n2 * c * 2)))
    assert b % bblk == 0
    return pl.pallas_call(
        _conv_mid_kernel,
        grid=(n1, b // bblk),
        in_specs=[pl.BlockSpec((None, 2 * n2, 2 * n2), lambda i, bb: (i, 0, 0)),
                  pl.BlockSpec((None, 2 * n2, 2 * n2), lambda i, bb: (i, 0, 0)),
                  pl.BlockSpec((bblk, 2, None, n2, c), lambda i, bb: (bb, 0, i, 0, 0)),
                  pl.BlockSpec((2, None, n2, c), lambda i, bb: (0, i, 0, 0))],
        out_specs=pl.BlockSpec((bblk, 2, None, n2, c), lambda i, bb: (bb, 0, i, 0, 0)),
        out_shape=jax.ShapeDtypeStruct(a.shape, BF16),
        compiler_params=_params("parallel", "parallel"),
        name="conv_mid",
    )(g, gi, a, kf)


def _conv_out_kernel(f_ref, z_ref, w_ref, x0_ref, bias_ref, o_ref):
    y = jnp.dot(f_ref[...], z_ref[...], preferred_element_type=F32)
    o_ref[...] = (x0_ref[...].astype(F32) * (y + w_ref[...].astype(F32) * bias_ref[...])).astype(o_ref.dtype)


def conv_out(f, z, w, x0, bias_l, ln=4096):
    b, k, lanes = z.shape
    m = f.shape[0]
    ln = min(ln, lanes)
    assert lanes % ln == 0 and bias_l.shape == (1, ln)
    return pl.pallas_call(
        _conv_out_kernel,
        grid=(b, lanes // ln),
        in_specs=[pl.BlockSpec((m, k), lambda i, j: (0, 0)),
                  pl.BlockSpec((None, k, ln), lambda i, j: (i, 0, j)),
                  pl.BlockSpec((None, m, ln), lambda i, j: (i, 0, j)),
                  pl.BlockSpec((None, m, ln), lambda i, j: (i, 0, j)),
                  pl.BlockSpec((1, ln), lambda i, j: (0, 0))],
        out_specs=pl.BlockSpec((None, m, ln), lambda i, j: (i, 0, j)),
        out_shape=jax.ShapeDtypeStruct((b, m, lanes), BF16),
        compiler_params=_params("parallel", "parallel"),
        name="conv_out",
    )(f, z, w, x0, bias_l)


def _hyena_pre_kernel(x0_ref, x1_ref, v_ref, w0_ref, w1_ref, wv_ref, b0_ref, b1_ref, bv_ref, x0o_ref, wo_ref):
    seq = x0_ref.shape[0]
    row = lax.broadcasted_iota(jnp.int32, x0_ref.shape, 0)
    first, last = row == 0, row == seq - 1

    def conv(u_ref, w_ref, b_ref):
        u = u_ref[...].astype(F32)
        prev = jnp.where(first, 0.0, pltpu.roll(u, 1, axis=0))
        nxt = jnp.where(last, 0.0, pltpu.roll(u, seq - 1, axis=0))
        return b_ref[...] + prev * w_ref[0:1, :] + u * w_ref[1:2, :] + nxt * w_ref[2:3, :]

    x0o_ref[...] = conv(x0_ref, w0_ref, b0_ref).astype(x0o_ref.dtype)
    wo_ref[...] = (conv(v_ref, wv_ref, bv_ref) * conv(x1_ref, w1_ref, b1_ref)).astype(wo_ref.dtype)


def hyena_pre(hproj, conv_w, conv_b):
    b, seq, _ = hproj.shape
    c = HYENA_WIDTH
    cb = max(LANES, min(c, (1 << 20) // seq))
    nb = c // cb
    u_spec = lambda g: pl.BlockSpec((None, seq, cb), lambda i, j: (i, 0, g * nb + j))
    w_spec = lambda g: pl.BlockSpec((3, cb), lambda i, j: (0, g * nb + j))
    b_spec = lambda g: pl.BlockSpec((1, cb), lambda i, j: (0, g * nb + j))
    o_spec = pl.BlockSpec((None, seq, cb), lambda i, j: (i, 0, j))
    cw = conv_w.astype(F32)
    cbias = conv_b.reshape(1, 3 * c).astype(F32)
    return pl.pallas_call(
        _hyena_pre_kernel,
        grid=(b, nb),
        in_specs=[u_spec(0), u_spec(1), u_spec(2), w_spec(0), w_spec(1), w_spec(2),
                  b_spec(0), b_spec(1), b_spec(2)],
        out_specs=[o_spec, o_spec],
        out_shape=[jax.ShapeDtypeStruct((b, seq, c), BF16)] * 2,
        compiler_params=_params("parallel", "parallel"),
        name="hyena_pre",
    )(hproj, hproj, hproj, cw, cw, cw, cbias, cbias, cbias)


def hyena_branch(hproj, conv_w, conv_b, fw1, fb1, fw2, fb2, fw3, fb3, fw4, freq, hyena_bias):
    b, seq, _ = hproj.shape
    c = HYENA_WIDTH
    n = 2 * seq
    n1, n2 = _fft_split(n)
    n1h = n1 // 2
    cos1, sin1 = _slow_dft_tables(n1)
    g = _fast_dft_tables(n1, n2)
    gi = jnp.swapaxes(g, 1, 2)

    k_full, l1 = filter_mlp(seq, fw1, fb1, fw2, fb2, fw3, fb3, fw4, freq)
    scale = 1.0 / ((l1 + EPS) * n)
    fa_full = jnp.concatenate([cos1, -sin1], axis=0)
    fa_full = fa_full.astype(BF16)
    g, gi = g.astype(BF16), gi.astype(BF16)
    ka = left_matmul(fa_full, k_full.reshape(1, n1, n2 * c), BF16)
    kf = spectrum_mid(g, ka.reshape(2, n1, n2, c), scale)

    x0c, w = hyena_pre(hproj, conv_w, conv_b)
    a = left_matmul(fa_full[:, :n1h], w.reshape(b, n1h, n2 * c), BF16)
    z = conv_mid(g, gi, a.reshape(b, 2, n1, n2, c), kf)
    fc = jnp.concatenate([cos1[:n1h], -sin1[:n1h]], axis=1).astype(BF16)
    ln = min(4096, n2 * c)
    bias_l = jnp.tile(hyena_bias.reshape(1, c).astype(F32), (1, ln // c))
    y = conv_out(fc, z.reshape(b, 2 * n1, n2 * c), w.reshape(b, n1h, n2 * c), x0c.reshape(b, n1h, n2 * c),
                 bias_l, ln=ln)
    return y.reshape(b, seq, c)


ATT_SUB = 128


def _dilated_attn_kernel(q_ref, kp_ref, kc_ref, kn_ref, vp_ref, vc_ref, vn_ref, o_ref, l_ref, *,
                         n_cls, dilation, slopes):
    tq = q_ref.shape[0]
    q0 = pl.program_id(1) * tq
    res = pl.program_id(2)
    sk = ATT_SUB + 2 * ATT_HALF
    scale = 1.0 / math.sqrt(HEAD_DIM)
    qi = lax.broadcasted_iota(jnp.int32, (ATT_SUB, sk), 0)
    kj = lax.broadcasted_iota(jnp.int32, (ATT_SUB, sk), 1) - ATT_HALF
    dist = jnp.abs(kj - qi)
    band = dist <= ATT_HALF
    adist = (dilation * dist).astype(F32)
    lane = lax.broadcasted_iota(jnp.int32, (sk, LANES), 1)
    low_k = lane < HEAD_DIM
    low_q = lax.broadcasted_iota(jnp.int32, (ATT_SUB, LANES), 1) < HEAD_DIM
    for sub in range(tq // ATT_SUB):
        r0 = sub * ATT_SUB
        kabs = q0 + r0 + kj
        mask = band & (kabs >= 0) & (kabs < n_cls)
        def keys(p_ref, c_ref, n_ref, cs):
            parts = []
            if r0 == 0:
                parts.append(p_ref[:, cs])
                parts.append(c_ref[0:min(tq, ATT_SUB + ATT_HALF), cs])
            else:
                parts.append(c_ref[r0 - ATT_HALF:min(tq, r0 + ATT_SUB + ATT_HALF), cs])
            if r0 + ATT_SUB + ATT_HALF > tq:
                parts.append(n_ref[:, cs])
            return jnp.concatenate(parts, axis=0)

        heads = [(pair, e) for pair in range(HEADS_PER_GROUP // 2) for e in range(2)]
        scores, values = [], []
        for pair, e in heads:
            cs = slice(pair * LANES, (pair + 1) * LANES)
            sel = low_k if e == 0 else ~low_k
            k = keys(kp_ref, kc_ref, kn_ref, cs)
            v = keys(vp_ref, vc_ref, vn_ref, cs)
            ke = jnp.where(sel, k, jnp.zeros_like(k))
            values.append(jnp.where(sel, v, jnp.zeros_like(v)))
            s = lax.dot_general(q_ref[r0:r0 + ATT_SUB, cs], ke, (((1,), (1,)), ((), ())),
                                preferred_element_type=F32) * scale
            scores.append(jnp.where(mask, s - slopes[2 * pair + e] * adist, NEG_INF))
        maxes = [jnp.max(s, axis=-1, keepdims=True) for s in scores]
        probs = [jnp.exp(s - m) for s, m in zip(scores, maxes)]
        sums = [jnp.sum(p, axis=-1, keepdims=True) for p in probs]
        outs = [jnp.dot(p.astype(BF16), v, preferred_element_type=F32) for p, v in zip(probs, values)]
        if dilation == 1:
            tok = slice(r0, r0 + ATT_SUB)
        else:
            tok = pl.ds(r0 * dilation + res, ATT_SUB, stride=dilation)
        for pair in range(HEADS_PER_GROUP // 2):
            h0, h1 = 2 * pair, 2 * pair + 1
            inv = 1.0 / jnp.where(low_q, sums[h0], sums[h1])
            o_ref[pair, tok, :] = (outs[h0] + outs[h1]) * inv
            l_ref[pair, tok, :] = jnp.where(low_q, maxes[h0] + jnp.log(sums[h0]), maxes[h1] + jnp.log(sums[h1]))


def dilated_attention(qkv, batch, group, out_block_bytes=4 << 20):
    window, dilation = ATT_GROUPS[group]
    assert window // (2 * dilation) == ATT_HALF
    n_cls = qkv.shape[0] // batch
    seq = n_cls * dilation
    tq = min(256, n_cls, max(ATT_SUB, out_block_bytes // (dilation * ATT_WIDTH * 4)))
    assert n_cls % tq == 0 and tq % ATT_SUB == 0
    hb = tq // ATT_HALF
    n_halo = n_cls // ATT_HALF
    slopes = tuple(2.0 ** (-8.0 * (group * HEADS_PER_GROUP + h + 1.0) / N_ATT_HEADS)
                   for h in range(HEADS_PER_GROUP))
    x = qkv.reshape(batch, n_cls, dilation * 3 * ATT_WIDTH)
    cur = lambda which: pl.BlockSpec((None, tq, ATT_WIDTH), lambda i, j, r: (i, j, 3 * r + which))
    prev = lambda which: pl.BlockSpec((None, ATT_HALF, ATT_WIDTH),
                                      lambda i, j, r: (i, jnp.maximum(j * hb - 1, 0), 3 * r + which))
    nxt = lambda which: pl.BlockSpec((None, ATT_HALF, ATT_WIDTH),
                                     lambda i, j, r: (i, jnp.minimum((j + 1) * hb, n_halo - 1), 3 * r + which))
    npair = ATT_WIDTH // LANES
    o_spec = pl.BlockSpec((None, npair, tq * dilation, LANES), lambda i, j, r: (i, 0, j, 0))
    return pl.pallas_call(
        functools.partial(_dilated_attn_kernel, n_cls=n_cls, dilation=dilation, slopes=slopes),
        grid=(batch, n_cls // tq, dilation),
        in_specs=[cur(0), prev(1), cur(1), nxt(1), prev(2), cur(2), nxt(2)],
        out_specs=[o_spec, o_spec],
        out_shape=[jax.ShapeDtypeStruct((batch, npair, seq, LANES), F32)] * 2,
        compiler_params=_params("parallel", "parallel", "arbitrary"),
        name="dilated_attn",
    )(x, x, x, x, x, x, x)


def _merge_kernel(x_ref, ga_ref, gb_ref, ya_ref, o0_ref, o1_ref, o2_ref, l0_ref, l1_ref, l2_ref,
                  wa_ref, wb_ref, wo_ref, g2_ref, x1_ref, xn_ref, xn8_ref):
    parts = []
    for pair in range(o0_ref.shape[0]):
        l0, l1, l2 = l0_ref[pair], l1_ref[pair], l2_ref[pair]
        m = jnp.maximum(jnp.maximum(l0, l1), l2)
        e0, e1, e2 = jnp.exp(l0 - m), jnp.exp(l1 - m), jnp.exp(l2 - m)
        yb = (e0 * o0_ref[pair] + e1 * o1_ref[pair] + e2 * o2_ref[pair]) / (e0 + e1 + e2)
        parts.append(yb.astype(BF16))
    pa = jnp.dot(ya_ref[...].astype(BF16), wa_ref[...], preferred_element_type=F32)
    pb = jnp.dot(jnp.concatenate(parts, axis=1), wb_ref[...], preferred_element_type=F32)
    merged = (jax.nn.sigmoid(ga_ref[...].astype(F32)) * pa
              + jax.nn.sigmoid(gb_ref[...].astype(F32)) * pb)
    x1 = x_ref[...] + jnp.dot(merged.astype(BF16), wo_ref[...], preferred_element_type=F32)
    x1_ref[...] = x1
    ms = jnp.mean(x1 * x1, axis=-1, keepdims=True)
    xn = x1 * lax.rsqrt(ms + EPS) * g2_ref[...]
    xn_ref[...] = xn.astype(xn_ref.dtype)
    xn8_ref[...] = xn.astype(xn8_ref.dtype)


def merge(x, glog, ya, outs, lses, wa, wb, wo, g2, tm=256):
    t, d = x.shape
    _, npair, seq, _ = outs[0].shape
    tm = min(tm, seq)
    assert seq % tm == 0
    tiles = seq // tm
    row = lambda c, blk=0: pl.BlockSpec((tm, c), lambda i: (i, blk))
    const = lambda r, c: pl.BlockSpec((r, c), lambda i: (0, 0), pipeline_mode=pl.Buffered(1))
    att = pl.BlockSpec((None, npair, tm, LANES), lambda i: (i // tiles, 0, i % tiles, 0))
    aw = ATT_WIDTH
    return pl.pallas_call(
        _merge_kernel,
        grid=(t // tm,),
        in_specs=[row(d), row(d, 0), row(d, 1), row(HYENA_WIDTH), att, att, att, att, att, att,
                  const(HYENA_WIDTH, d), const(aw, d), const(d, d), const(1, d)],
        out_specs=[row(d), row(d), row(d)],
        out_shape=[jax.ShapeDtypeStruct((t, d), F32), jax.ShapeDtypeStruct((t, d), BF16),
                   jax.ShapeDtypeStruct((t, d), F8)],
        compiler_params=_params("parallel"),
        name="merge",
    )(x, glog, glog, ya, *outs, *lses, wa, wb, wo, g2.reshape(1, d).astype(F32))


def _descending_max(curs, k, emit):
    curs = list(curs)
    for it in range(k):
        for c, cur in enumerate(curs):
            m = jnp.max(cur, axis=0, keepdims=True)
            emit(c, it, m)
            if it + 1 < k:
                curs[c] = jnp.where(cur == m, -jnp.inf, cur)


PEER_LOCKSTEP_HEADS = 2


def _peer_scores_kernel(xn_ref, wq_ref, sk_ref, g1_ref, g2_ref, top_ref, s_ref):
    k = PEER_TOPK
    sub = 8
    nh = PEER_HEADS * N_KEYS
    q = jnp.dot(xn_ref[...], wq_ref[...], preferred_element_type=F32).astype(BF16)
    low_rank = lax.broadcasted_iota(jnp.int32, (sub, LANES), 0) < 4
    for tc in range(xn_ref.shape[0] // LANES):
        cols = slice(tc * LANES, (tc + 1) * LANES)
        for h0 in range(0, PEER_HEADS, PEER_LOCKSTEP_HEADS):
            heads = range(h0, h0 + PEER_LOCKSTEP_HEADS)
            chains = [(hh, p) for hh in range(PEER_LOCKSTEP_HEADS) for p in range(2)]
            for c, (hh, p) in enumerate(chains):
                hp = 2 * (h0 + hh) + p
                qs = q[cols, hp * N_KEYS:(hp + 1) * N_KEYS]
                s_ref[c] = lax.dot_general(sk_ref[hp], qs, (((1,), (1,)), ((), ())), preferred_element_type=F32)

            def put_top(c, it, m):
                top_ref[c, it:it + 1, :] = m

            _descending_max([s_ref[c] for c in range(len(chains))], k, put_top)

            cands = []
            for hh in range(PEER_LOCKSTEP_HEADS):
                v1, v2 = top_ref.at[2 * hh], top_ref.at[2 * hh + 1]
                tiles = [v1[r:r + 1, :] + v2[0:sub, :] for r in range(4)]
                tiles += [v1[0:sub, :] + v2[r:r + 1, :] for r in range(4)]
                tiles += [v1[sub:k, :] + v2[0:1, :], v1[0:1, :] + v2[sub:k, :]]
                cands.append(jnp.concatenate(tiles, axis=0))
            repeated = [None] * 4 + [low_rank] * 4 + [None] * 2
            taus = [None] * PEER_LOCKSTEP_HEADS

            def put_tau(c, it, m):
                taus[c] = m

            _descending_max(cands, k, put_tau)
            for hh, h in enumerate(heads):
                rows = slice(h * N_KEYS, (h + 1) * N_KEYS)
                rows_e = slice(nh + h * N_KEYS, nh + (h + 1) * N_KEYS)
                v1, v2 = top_ref.at[2 * hh], top_ref.at[2 * hh + 1]
                tau = taus[hh]
                best = v1[0:1, :] + v2[0:1, :]
                zsum = None
                for ti, dup in enumerate(repeated):
                    tile = cands[hh][ti * sub:(ti + 1) * sub, :]
                    keep = tile >= tau
                    if dup is not None:
                        keep = keep & ~dup
                    part = jnp.sum(jnp.where(keep, jnp.exp(tile - best), 0.0), axis=0, keepdims=True)
                    zsum = part if zsum is None else zsum + part
                inv_z = 1.0 / zsum
                v2_all = v2[...]
                e2_top = jnp.exp(v2_all - v2[0:1, :]) * inv_z
                s1 = s_ref[2 * hh]
                theta = jnp.full(s1.shape, jnp.inf, F32)
                for r in range(k):
                    c = v1[r:r + 1, :] + v2_all
                    theta_r = jnp.min(jnp.where(c >= tau, e2_top, jnp.inf), axis=0, keepdims=True)
                    theta = jnp.where(s1 == v1[r:r + 1, :], theta_r, theta)
                g1_ref[rows, cols] = theta
                g1_ref[rows_e, cols] = jnp.exp(s1 - v1[0:1, :])
                g2_ref[tc, rows, :] = jnp.exp(s_ref[2 * hh + 1] - v2[0:1, :]) * inv_z


def peer_scores(xn, wq, subkeys, tm=256):
    t, d = xn.shape
    tm = min(tm, t)
    assert t % tm == 0 and tm % LANES == 0
    rows = 2 * PEER_HEADS * N_KEYS
    return pl.pallas_call(
        _peer_scores_kernel,
        grid=(t // tm,),
        in_specs=[pl.BlockSpec((tm, d), lambda i: (i, 0)),
                  pl.BlockSpec((d, rows), lambda i: (0, 0), pipeline_mode=pl.Buffered(1)),
                  pl.BlockSpec((2 * PEER_HEADS, N_KEYS, N_KEYS), lambda i: (0, 0, 0))],
        out_specs=[pl.BlockSpec((rows, tm), lambda i: (0, i)),
                   pl.BlockSpec((tm // LANES, rows // 2, LANES), lambda i: (i, 0, 0))],
        out_shape=[jax.ShapeDtypeStruct((rows, t), F32),
                   jax.ShapeDtypeStruct((t // LANES, rows // 2, LANES), F32)],
        scratch_shapes=[pltpu.VMEM((2 * PEER_LOCKSTEP_HEADS, PEER_TOPK, LANES), F32),
                        pltpu.VMEM((2 * PEER_LOCKSTEP_HEADS, N_KEYS, LANES), F32)],
        compiler_params=_params("parallel"),
        name="peer_scores",
    )(xn, wq, subkeys)


GELU_C0 = math.sqrt(2.0 / math.pi)
GELU_C1 = GELU_C0 * 0.044715


def _peer_experts_kernel(xn_ref, g1_ref, g2_ref, u_ref, vt_ref, x1_ref, g_ref, y_ref, acc_ref, a_ref, wa_ref):
    j = pl.program_id(1)
    nblk = pl.num_programs(1) - 2
    _, nchunk, eb, _ = a_ref.shape
    tm = nchunk * LANES
    a_even, a_odd = a_ref.at[0], a_ref.at[1]
    wa_even, wa_odd = wa_ref.at[0], wa_ref.at[1]
    nh = PEER_HEADS * N_KEYS
    per = eb // N_KEYS
    half = tm // 2

    mrows = 128
    arows = 256

    def activations(a_ref, m, c):
        rows = slice(m * arows, (m + 1) * arows)
        cols = slice(c * half, (c + 1) * half)
        a = lax.dot_general(u_ref[rows, :], xn_ref[cols, :], (((1,), (1,)), ((), ())),
                            preferred_element_type=F32)
        for k in range(half // LANES):
            a_ref[c * (half // LANES) + k, rows, :] = a[:, k * LANES:(k + 1) * LANES]

    def gate_rows(block, ii):
        key1 = block * per + ii
        fold = 0.5 * PEER_W_SCALE / PEER_U_SCALE
        return ([g1_ref[pl.ds(h * N_KEYS + key1, 1), :] for h in range(PEER_HEADS)],
                [g1_ref[pl.ds(nh + h * N_KEYS + key1, 1), :] * fold for h in range(PEER_HEADS)])

    def gates(a_ref, wa_ref, theta_rows, e1_rows, ii, tc):
        rows = slice(ii * N_KEYS, (ii + 1) * N_KEYS)
        cols = slice(tc * LANES, (tc + 1) * LANES)
        wsum = None
        for h in range(PEER_HEADS):
            e2 = g2_ref[tc, h * N_KEYS:(h + 1) * N_KEYS, :]
            term = e1_rows[h][:, cols] * jnp.where(e2 >= theta_rows[h][:, cols], e2, 0.0)
            wsum = term if wsum is None else wsum + term
        a = a_ref[tc, rows, :]
        inner = a * (GELU_C0 / PEER_U_SCALE + (GELU_C1 / PEER_U_SCALE ** 3) * (a * a))
        wa_ref[rows, cols] = ((a * wsum) * (1.0 + jnp.tanh(inner))).astype(wa_ref.dtype)

    def outputs(wa_ref, r, c):
        rows = slice(r * mrows, (r + 1) * mrows)
        cols = slice(c * half, (c + 1) * half)
        acc_ref[rows, cols] += jnp.dot(vt_ref[rows, :], wa_ref[:, cols], preferred_element_type=F32)

    def step(act, gate, out):
        vec, mxu_a, mxu_o = [], [], []
        rows_cache = {}

        def gate_piece(ii, tc):
            if ii not in rows_cache:
                rows_cache[ii] = gate_rows(j - 1, ii)
            gates(*gate, *rows_cache[ii], ii, tc)

        if gate:
            vec = [functools.partial(gate_piece, ii, tc) for ii in range(per) for tc in range(nchunk)]
        if act is not None:
            mxu_a = [functools.partial(activations, act, m, c) for m in range(eb // arows) for c in range(2)]
        if out is not None:
            mxu_o = [functools.partial(outputs, out, r, c) for c in range(2) for r in range(acc_ref.shape[0] // mrows)]
        n = max(len(vec), len(mxu_a), len(mxu_o))
        for k in range(n):
            for stage in (vec, mxu_a, mxu_o):
                for piece in stage[k * len(stage) // n:(k + 1) * len(stage) // n]:
                    piece()

    steady = (j > 1) & (j < nblk)

    @pl.when(j == 0)
    def _():
        acc_ref[...] = jnp.zeros_like(acc_ref)
        step(a_even, None, None)

    @pl.when(j == 1)
    def _():
        step(a_odd, (a_even, wa_even), None)

    @pl.when(steady & (j % 2 == 0))
    def _():
        step(a_even, (a_odd, wa_odd), wa_even)

    @pl.when(steady & (j % 2 == 1))
    def _():
        step(a_odd, (a_even, wa_even), wa_odd)

    @pl.when(j == nblk)
    def _():
        step(None, (a_odd, wa_odd), wa_even)

    @pl.when(j == nblk + 1)
    def _():
        step(None, None, wa_odd)
        x2 = x1_ref[...] + acc_ref[...].T * (1.0 / PEER_W_SCALE)
        ms = jnp.mean(x2 * x2, axis=-1, keepdims=True)
        y_ref[...] = x2 * lax.rsqrt(ms + EPS) * g_ref[...]


def peer_experts(xn, g1, g2, u, vt, x1, final_g, tm=512):
    t, d = xn.shape
    eb = vt.shape[2]
    tm = min(tm, t)
    assert t % tm == 0 and N_EXPERTS % (2 * eb) == 0 and eb % N_KEYS == 0 and tm % (2 * LANES) == 0
    rows = 2 * PEER_HEADS * N_KEYS
    nblk = N_EXPERTS // eb
    return pl.pallas_call(
        _peer_experts_kernel,
        grid=(t // tm, nblk + 2),
        in_specs=[pl.BlockSpec((tm, d), lambda i, j: (i, 0)),
                  pl.BlockSpec((rows, tm), lambda i, j: (0, i)),
                  pl.BlockSpec((tm // LANES, rows // 2, LANES), lambda i, j: (i, 0, 0)),
                  pl.BlockSpec((eb, d), lambda i, j: (jnp.minimum(j, nblk - 1), 0)),
                  pl.BlockSpec((None, d, eb), lambda i, j: (jnp.clip(j - 2, 0, nblk - 1), 0, 0)),
                  pl.BlockSpec((tm, d), lambda i, j: (i, 0), pipeline_mode=pl.Buffered(1)),
                  pl.BlockSpec((1, d), lambda i, j: (0, 0))],
        out_specs=pl.BlockSpec((tm, d), lambda i, j: (i, 0)),
        out_shape=jax.ShapeDtypeStruct((t, d), F32),
        scratch_shapes=[pltpu.VMEM((d, tm), F32), pltpu.VMEM((2, tm // LANES, eb, LANES), F32),
                        pltpu.VMEM((2, eb, tm), F8)],
        compiler_params=_params("parallel", "arbitrary"),
        name="peer_experts",
    )(xn, g1, g2, u, vt, x1, final_g.reshape(1, d).astype(F32))


def _encoder(x, p):
    b, seq, d = x.shape
    t = b * seq
    xt = x.reshape(t, d)
    xn1 = rms_norm_bf16(xt, p["norm1_g"])
    hproj = project(xn1, p["w_in"], 0, HYENA_COLS, BF16)
    glog = project(xn1, p["w_in"], HYENA_COLS + ATT_COLS, GATE_COLS, BF16)
    ya = hyena_branch(hproj.reshape(b, seq, HYENA_COLS), p["conv_w"], p["conv_b"], p["filt_w1"], p["filt_b1"],
                      p["filt_w2"], p["filt_b2"], p["filt_w3"], p["filt_b3"], p["filt_w4"], p["filt_freq"],
                      p["hyena_bias"])
    att = [dilated_attention(project_classes(xn1, p["w_in"], g), b, g) for g in range(N_GROUPS)]
    x1, xn2, xn2_f8 = merge(xt, glog, ya.reshape(t, HYENA_WIDTH), [o for o, _ in att], [l for _, l in att],
                            p["w_branch_a"], p["w_branch_b"], p["w_out"], p["norm2_g"])
    g1, g2 = peer_scores(xn2, p["peer_wq"], p["peer_subkeys"])
    y = peer_experts(xn2_f8, g1, g2, p["peer_u"], p["peer_vt"], x1, p["final_g"])
    return y.reshape(b, seq, d)


def kernel(x_prompt, x_sample, norm1_g, w_in, conv_w, conv_b, filt_w1, filt_b1, filt_w2, filt_b2, filt_w3,
           filt_b3, filt_w4, filt_freq, hyena_bias, w_branch_a, w_branch_b, w_out, norm2_g, peer_wq,
           peer_subkeys, peer_u, peer_v, final_g):
    assert norm1_g.shape[0] == 1, "single-layer encoder"
    p = dict(
        norm1_g=norm1_g[0].astype(F32), w_in=w_in[0].astype(BF16), conv_w=conv_w[0], conv_b=conv_b[0],
        filt_w1=filt_w1[0], filt_b1=filt_b1[0], filt_w2=filt_w2[0], filt_b2=filt_b2[0], filt_w3=filt_w3[0],
        filt_b3=filt_b3[0], filt_w4=filt_w4[0], filt_freq=filt_freq[0], hyena_bias=hyena_bias[0],
        w_branch_a=w_branch_a[0].astype(BF16), w_branch_b=w_branch_b[0].astype(BF16),
        w_out=w_out[0].astype(BF16), norm2_g=norm2_g[0], peer_wq=peer_wq[0].astype(BF16),
        peer_subkeys=peer_subkeys[0].reshape(2 * PEER_HEADS, N_KEYS, N_KEYS).astype(BF16),
        peer_u=(peer_u[0] * PEER_U_SCALE).astype(F8),
        peer_vt=peer_v[0].astype(F8).reshape(N_EXPERTS // PEER_EB, PEER_EB, D_MODEL).transpose(0, 2, 1),
        final_g=final_g,
    )
    return (_encoder(x_prompt, p), _encoder(x_sample, p))
```
